```python
import math
import jax
import jax.numpy as jnp
from jax import lax
import numpy as np

D_MODEL = 1024
BATCH = 8
SEQ = 4096
DEPTH = 1

D_MIX = D_MODEL
SB_HEADS = 8
SB_HEAD_DIM = 64
SB_WIDTH = SB_HEADS * SB_HEAD_DIM
POOL_WINDOWS = (2, 4, 8, 16)
POOL_GROUPS = len(POOL_WINDOWS)
POOL_WIDTH = D_MIX - SB_WIDTH
POOL_GROUP_DIM = POOL_WIDTH // POOL_GROUPS
IN_PROJ_COLS = 3 * SB_WIDTH + POOL_WIDTH
Q_BLOCK = 128
N_MEM = 256
MEM_HEADS = 4
MEM_HEAD_DIM = D_MODEL // MEM_HEADS
N_EXPERTS = 32
TOP_K = 4
D_FF = D_MODEL
SWIGLU_LIMIT = 7.0
SWIGLU_ALPHA = 1.702
MOE_BLOCK = 128
RMS_EPS = 1e-5

kernel_name = "hybrid_stickbreak_pool_memxattn_moe"


def rmsnorm(x, g):
    xf = x.astype(jnp.float32)
    y = xf * lax.rsqrt(jnp.mean(xf * xf, axis=-1, keepdims=True) + RMS_EPS)
    return (y * g.astype(jnp.float32)).astype(x.dtype)


def stick_breaking_attention(q, k, v):
    seq = q.shape[2]
    scale = 1.0 / math.sqrt(q.shape[-1])
    outs = []
    for i in range(seq // Q_BLOCK):
        t0 = i * Q_BLOCK
        kv_len = t0 + Q_BLOCK
        qb = q[:, :, t0:kv_len]
        kb = k[:, :, :kv_len]
        vb = v[:, :, :kv_len]
        z = jnp.einsum('bhqd,bhkd->bhqk', qb, kb).astype(jnp.float32) * scale
        t_pos = t0 + jnp.arange(Q_BLOCK)
        s_pos = jnp.arange(kv_len)
        mask = s_pos[None, :] < t_pos[:, None]
        log_stay = jnp.where(mask, jax.nn.log_sigmoid(-z), 0.0)
        rev = lax.cumsum(log_stay, axis=3, reverse=True)
        after = jnp.concatenate([rev[..., 1:], jnp.zeros_like(rev[..., :1])], axis=-1)
        w = jnp.where(mask, jnp.exp(jax.nn.log_sigmoid(z) + after), 0.0)
        outs.append(jnp.einsum('bhqk,bhkd->bhqd', w.astype(vb.dtype), vb))
    return jnp.concatenate(outs, axis=2)


def multiscale_pool(u):
    seq = u.shape[1]
    uf = u.astype(jnp.float32)
    csum = jnp.cumsum(uf, axis=1)
    outs = []
    for g, win in enumerate(POOL_WINDOWS):
        cg = csum[:, :, g]
        shifted = jnp.pad(cg, ((0, 0), (win, 0), (0, 0)))[:, :seq]
        count = jnp.minimum(jnp.arange(seq) + 1, win).astype(jnp.float32)
        outs.append((cg - shifted) / count[None, :, None] - uf[:, :, g])
    return jnp.stack(outs, axis=2).astype(u.dtype)


def memory_cross_attention(h, m, w_q, w_kv, w_o):
    b, s, d = h.shape
    q = (h @ w_q).reshape(b, s, MEM_HEADS, MEM_HEAD_DIM)
    kv = m @ w_kv
    k = kv[..., :d].reshape(b, -1, MEM_HEADS, MEM_HEAD_DIM)
    v = kv[..., d:].reshape(b, -1, MEM_HEADS, MEM_HEAD_DIM)
    scores = jnp.einsum('bshd,bmhd->bhsm', q, k).astype(jnp.float32) / math.sqrt(MEM_HEAD_DIM)
    p = jax.nn.softmax(scores, axis=-1).astype(v.dtype)
    o = jnp.einsum('bhsm,bmhd->bshd', p, v).reshape(b, s, d)
    return o @ w_o


def moe_ffn(h, w_router, b_router, w1, b1, w2, b2):
    b, s, d = h.shape
    n_tok = b * s
    hf = h.reshape(n_tok, d)
    logits = (hf @ w_router + b_router).astype(jnp.float32)
    top_val, top_idx = lax.top_k(logits, TOP_K)
    gates = jax.nn.softmax(top_val, axis=-1)
    nk = n_tok * TOP_K
    flat_e = top_idx.reshape(nk)
    flat_tok = (jnp.arange(nk, dtype=jnp.int32) // TOP_K)
    flat_g = gates.reshape(nk)
    order = jnp.argsort(flat_e)
    sorted_e = flat_e[order]
    counts = jnp.bincount(flat_e, length=N_EXPERTS)
    padded = (counts + MOE_BLOCK - 1) // MOE_BLOCK * MOE_BLOCK
    start = jnp.cumsum(counts) - counts
    pend = jnp.cumsum(padded)
    pstart = pend - padded
    dest = pstart[sorted_e] + jnp.arange(nk) - start[sorted_e]
    n_rows = (nk + MOE_BLOCK - 1) // MOE_BLOCK * MOE_BLOCK + N_EXPERTS * MOE_BLOCK
    n_blk = n_rows // MOE_BLOCK
    row_tok = jnp.zeros((n_rows,), jnp.int32).at[dest].set(flat_tok[order])
    row_g = jnp.zeros((n_rows,), h.dtype).at[dest].set(flat_g[order].astype(h.dtype))
    blk_e = jnp.minimum(jnp.searchsorted(pend, jnp.arange(n_blk) * MOE_BLOCK, side='right'),
                        N_EXPERTS - 1)

    def expert_block(args):
        e, tok, g = args
        xb = hf[tok]
        gu = xb @ w1[e] + b1[e]
        gate = jnp.minimum(gu[:, :D_FF], SWIGLU_LIMIT)
        up = jnp.clip(gu[:, D_FF:], -SWIGLU_LIMIT, SWIGLU_LIMIT)
        act = (up + 1.0) * gate * jax.nn.sigmoid(SWIGLU_ALPHA * gate)
        return (act @ w2[e] + b2[e]) * g[:, None]

    y = lax.map(expert_block, (blk_e, row_tok.reshape(n_blk, MOE_BLOCK),
                               row_g.reshape(n_blk, MOE_BLOCK)))
    out = jnp.zeros((n_tok, d), h.dtype).at[row_tok].add(y.reshape(n_rows, d))
    return out.reshape(b, s, d)


def setup_inputs(seed: int = 0) -> dict:
    key = jax.random.key(seed)
    ks = jax.random.split(key, 24)
    f32 = jnp.float32
    nrm = lambda k, shape, fan_in: jax.random.normal(k, shape, f32) * (fan_in ** -0.5)
    gain = lambda k, shape: 1.0 + 0.05 * jax.random.normal(k, shape, f32)
    L = DEPTH
    return {
        "x": jax.random.normal(ks[0], (BATCH, SEQ, D_MODEL), f32),
        "mem": jax.random.normal(ks[1], (BATCH, N_MEM, D_MODEL), f32),
        "g_mix": gain(ks[2], (L, D_MODEL)),
        "w_in": nrm(ks[3], (L, D_MODEL, IN_PROJ_COLS), D_MODEL),
        "g_sb_out": gain(ks[4], (L, SB_WIDTH)),
        "g_pool_out": gain(ks[5], (L, POOL_WIDTH)),
        "w_pool": nrm(ks[6], (L, POOL_GROUPS, POOL_GROUP_DIM, POOL_GROUP_DIM), POOL_GROUP_DIM),
        "pool_scale": 1.0 + 0.1 * jax.random.normal(ks[7], (L, POOL_WIDTH), f32),
        "w_out": nrm(ks[8], (L, D_MIX, D_MODEL), D_MIX),
        "g_mem_q": gain(ks[9], (L, D_MODEL)),
        "g_mem_kv": gain(ks[10], (L, D_MODEL)),
        "w_mem_q": nrm(ks[11], (L, D_MODEL, D_MODEL), D_MODEL),
        "w_mem_kv": nrm(ks[12], (L, D_MODEL, 2 * D_MODEL), D_MODEL),
        "w_mem_o": nrm(ks[13], (L, D_MODEL, D_MODEL), D_MODEL),
        "g_ffn": gain(ks[14], (L, D_MODEL)),
        "w_router": nrm(ks[15], (L, D_MODEL, N_EXPERTS), D_MODEL),
        "b_router": 0.01 * jax.random.normal(ks[16], (L, N_EXPERTS), f32),
        "w_expert_in": nrm(ks[17], (L, N_EXPERTS, D_MODEL, 2 * D_FF), D_MODEL),
        "b_expert_in": 0.01 * jax.random.normal(ks[18], (L, N_EXPERTS, 2 * D_FF), f32),
        "w_expert_out": nrm(ks[19], (L, N_EXPERTS, D_FF, D_MODEL), D_FF),
        "b_expert_out": 0.01 * jax.random.normal(ks[20], (L, N_EXPERTS, D_MODEL), f32),
        "g_final": gain(ks[21], (D_MODEL,)),
    }


def reference(x, mem, g_mix, w_in, g_sb_out, g_pool_out, w_pool, pool_scale, w_out,
              g_mem_q, g_mem_kv, w_mem_q, w_mem_kv, w_mem_o, g_ffn, w_router, b_router,
              w_expert_in, b_expert_in, w_expert_out, b_expert_out, g_final):
    b, s, d = x.shape
    for l in range(DEPTH):
        h = rmsnorm(x, g_mix[l])
        proj = h @ w_in[l]
        q = proj[..., :SB_WIDTH].reshape(b, s, SB_HEADS, SB_HEAD_DIM).transpose(0, 2, 1, 3)
        k = proj[..., SB_WIDTH:2 * SB_WIDTH].reshape(b, s, SB_HEADS, SB_HEAD_DIM).transpose(0, 2, 1, 3)
        v = proj[..., 2 * SB_WIDTH:3 * SB_WIDTH].reshape(b, s, SB_HEADS, SB_HEAD_DIM).transpose(0, 2, 1, 3)
        u = proj[..., 3 * SB_WIDTH:].reshape(b, s, POOL_GROUPS, POOL_GROUP_DIM)
        sb = stick_breaking_attention(q, k, v).transpose(0, 2, 1, 3).reshape(b, s, SB_WIDTH)
        pooled = multiscale_pool(u)
        pooled = jnp.einsum('bsgc,gce->bsge', pooled, w_pool[l]).reshape(b, s, POOL_WIDTH) * pool_scale[l]
        mixed = jnp.concatenate([rmsnorm(sb, g_sb_out[l]), rmsnorm(pooled, g_pool_out[l])], axis=-1)
        x = x + mixed @ w_out[l]
        x = x + memory_cross_attention(rmsnorm(x, g_mem_q[l]), rmsnorm(mem, g_mem_kv[l]),
                                       w_mem_q[l], w_mem_kv[l], w_mem_o[l])
        x = x + moe_ffn(rmsnorm(x, g_ffn[l]), w_router[l], b_router[l], w_expert_in[l],
                        b_expert_in[l], w_expert_out[l], b_expert_out[l])
    return rmsnorm(x, g_final)
```

```python
import functools
import math

import jax
import jax.numpy as jnp
from jax import lax
from jax.experimental import pallas as pl
from jax.experimental.pallas import tpu as pltpu

F32 = jnp.float32
BF16 = jnp.bfloat16

D_MODEL = 1024
SB_HEADS = 8
SB_HEAD_DIM = 64
SB_WIDTH = SB_HEADS * SB_HEAD_DIM
POOL_WINDOWS = (2, 4, 8, 16)
POOL_GROUP_DIM = 128
POOL_WIDTH = len(POOL_WINDOWS) * POOL_GROUP_DIM
IN_PROJ_COLS = 3 * SB_WIDTH + POOL_WIDTH
N_MEM = 256
MEM_HEADS = 4
MEM_HEAD_DIM = D_MODEL // MEM_HEADS
N_EXPERTS = 32
TOP_K = 4
D_FF = D_MODEL
SWIGLU_LIMIT = 7.0
SWIGLU_ALPHA = 1.702
RMS_EPS = 1e-5

LANES = 128
POOL_HALO = 16
SB_BLOCK = 256
ROW_BLOCK = 256
TOK_BLOCK = 256
PROJ_BLOCK = 512
VMEM_LIMIT = 48 * 1024 * 1024
NEG_BIG = -1e30


def _rms(x, g):
    ms = jnp.mean(x * x, axis=-1, keepdims=True)
    return x * lax.rsqrt(ms + RMS_EPS) * g


def _params(*sem):
    return pltpu.CompilerParams(dimension_semantics=sem, vmem_limit_bytes=VMEM_LIMIT)


def _in_proj_kernel(x_ref, g_ref, w_ref, o_ref):
    h = _rms(x_ref[...], g_ref[...]).astype(BF16)
    acc = jnp.dot(h, w_ref[...], preferred_element_type=F32)
    scale = 1.0 / math.sqrt(SB_HEAD_DIM)
    o_ref[:, :SB_WIDTH] = (acc[:, :SB_WIDTH] * scale).astype(BF16)
    o_ref[:, SB_WIDTH:] = acc[:, SB_WIDTH:].astype(BF16)


def _in_proj(x2d, g, w):
    n = x2d.shape[0]
    tm = PROJ_BLOCK
    return pl.pallas_call(
        _in_proj_kernel,
        grid=(n // tm,),
        in_specs=[
            pl.BlockSpec((tm, D_MODEL), lambda i: (i, 0)),
            pl.BlockSpec((1, D_MODEL), lambda i: (0, 0)),
            pl.BlockSpec((D_MODEL, IN_PROJ_COLS), lambda i: (0, 0)),
        ],
        out_specs=pl.BlockSpec((tm, IN_PROJ_COLS), lambda i: (i, 0)),
        out_shape=jax.ShapeDtypeStruct((n, IN_PROJ_COLS), BF16),
        compiler_params=_params("arbitrary"),
        name="in_proj",
    )(x2d, g, w)


def _sb_kernel(q_ref, k_ref, v_ref, tri_ref, o_ref, acc_ref, c_ref):
    tq = q_ref.shape[0]
    i = pl.program_id(2)
    lane = lax.broadcasted_iota(jnp.int32, (tq, LANES), 1)
    q = q_ref[...]
    zero = jnp.zeros_like(q)
    qs = (jnp.where(lane < SB_HEAD_DIM, q, zero), jnp.where(lane >= SB_HEAD_DIM, q, zero))
    tri = tri_ref[...]
    row = lax.broadcasted_iota(jnp.int32, (tq, tq), 0)
    col = lax.broadcasted_iota(jnp.int32, (tq, tq), 1)
    causal = col < row
    acc_ref[...] = jnp.zeros_like(acc_ref)
    c_ref[...] = jnp.zeros_like(c_ref)

    def step(j, masked):
        off = pl.multiple_of(j * tq, tq)
        kb = k_ref[pl.ds(off, tq), :]
        vb = v_ref[pl.ds(off, tq), :]
        for h in range(2):
            z = lax.dot_general(qs[h], kb, (((1,), (1,)), ((), ())), preferred_element_type=F32)
            lg = jnp.log(1.0 + jnp.exp(-jnp.abs(z)))
            log_stay = jnp.minimum(-z, 0.0) - lg
            log_beta = jnp.minimum(z, 0.0) - lg
            if masked:
                log_stay = jnp.where(causal, log_stay, 0.0)
            c = c_ref[h]
            after = jnp.dot(log_stay.astype(BF16), tri, preferred_element_type=F32)
            after = after + jnp.concatenate([c] * (tq // LANES), axis=1)
            w = jnp.exp(log_beta + after)
            if masked:
                w = jnp.where(causal, w, 0.0)
            acc_ref[h] += jnp.dot(w.astype(BF16), vb, preferred_element_type=F32)
            c_ref[h] = c + jnp.sum(log_stay, axis=-1, keepdims=True)

    step(i, True)

    def body(jj, carry):
        step(i - 1 - jj, False)
        return carry

    lax.fori_loop(0, i, body, 0)
    o_ref[...] = jnp.where(lane < SB_HEAD_DIM, acc_ref[0], acc_ref[1]).astype(o_ref.dtype)


def _sb_attention(proj3, tri):
    b, s, _ = proj3.shape
    tq = SB_BLOCK
    pairs = SB_WIDTH // LANES
    return pl.pallas_call(
        _sb_kernel,
        grid=(b, pairs, s // tq),
        in_specs=[
            pl.BlockSpec((None, tq, LANES), lambda bi, p, i: (bi, i, p)),
            pl.BlockSpec((None, s, LANES), lambda bi, p, i: (bi, 0, pairs + p)),
            pl.BlockSpec((None, s, LANES), lambda bi, p, i: (bi, 0, 2 * pairs + p)),
            pl.BlockSpec((tq, tq), lambda bi, p, i: (0, 0)),
        ],
        out_specs=pl.BlockSpec((None, tq, LANES), lambda bi, p, i: (bi, i, p)),
        out_shape=jax.ShapeDtypeStruct((b, s, SB_WIDTH), BF16),
        scratch_shapes=[pltpu.VMEM((2, tq, LANES), F32), pltpu.VMEM((2, tq, LANES), F32)],
        compiler_params=_params("arbitrary", "arbitrary", "arbitrary"),
        name="sb_attn",
    )(proj3, proj3, proj3, tri)


def _mix_kernel(x_ref, sb_ref, u_ref, halo_ref, wp_ref, ps_ref, gsb_ref, gpool_ref, wout_ref,
                o_ref, *, seq):
    tm = x_ref.shape[0]
    t0 = (pl.program_id(0) * tm) % seq
    u = u_ref[...].astype(F32)
    halo = jnp.where(t0 == 0, 0.0, halo_ref[...].astype(F32))
    a = jnp.concatenate([halo, u], axis=0)
    pos = t0 + lax.broadcasted_iota(jnp.int32, (tm, POOL_GROUP_DIM), 0)
    outs = []
    for g, win in enumerate(POOL_WINDOWS):
        lo, hi = g * POOL_GROUP_DIM, (g + 1) * POOL_GROUP_DIM
        s = a[:, lo:hi]
        sh = 1
        while sh < win:
            s = s + pltpu.roll(s, sh, axis=0)
            sh *= 2
        cnt = jnp.minimum(pos + 1, win).astype(F32)
        pooled = s[POOL_HALO:, :] / cnt - u[:, lo:hi]
        outs.append(jnp.dot(pooled.astype(BF16), wp_ref[g], preferred_element_type=F32))
    pooled = jnp.concatenate(outs, axis=1) * ps_ref[...]
    pool_n = _rms(pooled, gpool_ref[...]).astype(BF16)
    sb_n = _rms(sb_ref[...].astype(F32), gsb_ref[...]).astype(BF16)
    y = jnp.dot(sb_n, wout_ref[:SB_WIDTH, :], preferred_element_type=F32)
    y = y + jnp.dot(pool_n, wout_ref[SB_WIDTH:, :], preferred_element_type=F32)
    o_ref[...] = x_ref[...] + y


def _mix_out(x2d, sb2d, proj2d, wp, ps, gsb, gpool, wout, seq):
    n = x2d.shape[0]
    tm = PROJ_BLOCK
    ucol = 3 * SB_WIDTH // POOL_WIDTH
    hb = tm // POOL_HALO
    return pl.pallas_call(
        functools.partial(_mix_kernel, seq=seq),
        grid=(n // tm,),
        in_specs=[
            pl.BlockSpec((tm, D_MODEL), lambda i: (i, 0)),
            pl.BlockSpec((tm, SB_WIDTH), lambda i: (i, 0)),
            pl.BlockSpec((tm, POOL_WIDTH), lambda i: (i, ucol)),
            pl.BlockSpec((POOL_HALO, POOL_WIDTH), lambda i: (jnp.maximum(i * hb - 1, 0), ucol)),
            pl.BlockSpec((len(POOL_WINDOWS), POOL_GROUP_DIM, POOL_GROUP_DIM), lambda i: (0, 0, 0)),
            pl.BlockSpec((1, POOL_WIDTH), lambda i: (0, 0)),
            pl.BlockSpec((1, SB_WIDTH), lambda i: (0, 0)),
            pl.BlockSpec((1, POOL_WIDTH), lambda i: (0, 0)),
            pl.BlockSpec((D_MODEL, D_MODEL), lambda i: (0, 0)),
        ],
        out_specs=pl.BlockSpec((tm, D_MODEL), lambda i: (i, 0)),
        out_shape=jax.ShapeDtypeStruct((n, D_MODEL), F32),
        compiler_params=_params("arbitrary"),
        name="mix_out",
    )(x2d, sb2d, proj2d, proj2d, wp, ps, gsb, gpool, wout)


def _mem_kv_kernel(m_ref, g_ref, w_ref, o_ref):
    h = _rms(m_ref[...], g_ref[...]).astype(BF16)
    o_ref[...] = jnp.dot(h, w_ref[...], preferred_element_type=F32).astype(BF16)


def _mem_kv(mem, g, w):
    b = mem.shape[0]
    return pl.pallas_call(
        _mem_kv_kernel,
        grid=(b,),
        in_specs=[
            pl.BlockSpec((None, N_MEM, D_MODEL), lambda i: (i, 0, 0)),
            pl.BlockSpec((1, D_MODEL), lambda i: (0, 0)),
            pl.BlockSpec((D_MODEL, 2 * D_MODEL), lambda i: (0, 0)),
        ],
        out_specs=pl.BlockSpec((None, N_MEM, 2 * D_MODEL), lambda i: (i, 0, 0)),
        out_shape=jax.ShapeDtypeStruct((b, N_MEM, 2 * D_MODEL), BF16),
        compiler_params=_params("arbitrary"),
        name="mem_kv",
    )(mem, g, w)


def _xattn_kernel(x_ref, kv_ref, gq_ref, wq_ref, wo_ref, gf_ref, wr_ref, br_ref, tri_ref,
                  x2_ref, h_ref, meta_ref, gate_ref, cnt_ref, base_ref):
    tm = x_ref.shape[0]

    @pl.when(pl.program_id(0) == 0)
    def _():
        base_ref[...] = jnp.zeros_like(base_ref)

    x = x_ref[...]
    hq = _rms(x, gq_ref[...]).astype(BF16)
    q = jnp.dot(hq, wq_ref[...], preferred_element_type=F32) * (1.0 / math.sqrt(MEM_HEAD_DIM))
    q = q.astype(BF16)
    outs = []
    for h in range(MEM_HEADS):
        lo, hi = h * MEM_HEAD_DIM, (h + 1) * MEM_HEAD_DIM
        s = lax.dot_general(q[:, lo:hi], kv_ref[:, lo:hi], (((1,), (1,)), ((), ())),
                            preferred_element_type=F32)
        p = jnp.exp(s - jnp.max(s, axis=-1, keepdims=True))
        denom = jnp.sum(p, axis=-1, keepdims=True)
        o = jnp.dot(p.astype(BF16), kv_ref[:, D_MODEL + lo:D_MODEL + hi], preferred_element_type=F32)
        outs.append(o / denom)
    o = jnp.concatenate(outs, axis=1).astype(BF16)
    x2 = x + jnp.dot(o, wo_ref[...], preferred_element_type=F32)
    x2_ref[...] = x2
    hf = _rms(x2, gf_ref[...])
    h_ref[...] = hf

    logits = jnp.dot(hf, wr_ref[...], preferred_element_type=F32,
                     precision=lax.Precision.HIGHEST) + br_ref[...]
    lane = lax.broadcasted_iota(jnp.int32, (tm, LANES), 1).astype(F32)
    work = logits
    vals, idxs, sels = [], [], []
    for _ in range(TOP_K):
        m = jnp.max(work, axis=-1, keepdims=True)
        idx = jnp.min(jnp.where(work == m, lane, float(LANES)), axis=-1, keepdims=True)
        sel = lane == idx
        vals.append(m)
        idxs.append(idx)
        sels.append(sel)
        work = jnp.where(sel, -3e38, work)
    exps = [jnp.exp(v - vals[0]) for v in vals]
    den = exps[0] + exps[1] + exps[2] + exps[3]

    onehot = jnp.zeros((tm, LANES), F32)
    for sel in sels:
        onehot = onehot + sel.astype(F32)
    before = jnp.dot(tri_ref[...], onehot.astype(BF16), preferred_element_type=F32)
    total = before + base_ref[...]
    meta = jnp.zeros((tm, LANES), F32)
    gates = jnp.zeros((tm, LANES), F32)
    for k in range(TOP_K):
        rank = jnp.sum(jnp.where(sels[k], total, 0.0), axis=-1, keepdims=True)
        meta = jnp.where(lane == float(k), idxs[k], meta)
        meta = jnp.where(lane == float(TOP_K + k), rank, meta)
        gates = jnp.where(lane == float(k), exps[k] / den, gates)
    meta_ref[...] = meta.astype(jnp.int32)
    gate_ref[...] = gates
    base = base_ref[...] + jnp.sum(onehot, axis=0, keepdims=True)
    base_ref[...] = base
    cnt_ref[...] = base


def _xattn_router(x1, kv, gq, wq, wo, gf, wr, br, tri, seq):
    n = x1.shape[0]
    tm = TOK_BLOCK
    per_seq = seq // tm
    const = lambda i: (0, 0)
    return pl.pallas_call(
        _xattn_kernel,
        grid=(n // tm,),
        in_specs=[
            pl.BlockSpec((tm, D_MODEL), lambda i: (i, 0)),
            pl.BlockSpec((None, N_MEM, 2 * D_MODEL), lambda i: (i // per_seq, 0, 0)),
            pl.BlockSpec((1, D_MODEL), const),
            pl.BlockSpec((D_MODEL, D_MODEL), const),
            pl.BlockSpec((D_MODEL, D_MODEL), const),
            pl.BlockSpec((1, D_MODEL), const),
            pl.BlockSpec((D_MODEL, LANES), const),
            pl.BlockSpec((1, LANES), const),
            pl.BlockSpec((tm, tm), const),
        ],
        out_specs=[
            pl.BlockSpec((tm, D_MODEL), lambda i: (i, 0)),
            pl.BlockSpec((tm, D_MODEL), lambda i: (i, 0)),
            pl.BlockSpec((tm, LANES), lambda i: (i, 0)),
            pl.BlockSpec((tm, LANES), lambda i: (i, 0)),
            pl.BlockSpec((1, LANES), const),
        ],
        out_shape=[
            jax.ShapeDtypeStruct((n, D_MODEL), F32),
            jax.ShapeDtypeStruct((n, D_MODEL), F32),
            jax.ShapeDtypeStruct((n, LANES), jnp.int32),
            jax.ShapeDtypeStruct((n, LANES), F32),
            jax.ShapeDtypeStruct((1, LANES), F32),
        ],
        scratch_shapes=[pltpu.VMEM((1, LANES), F32)],
        compiler_params=_params("arbitrary"),
        name="xattn_router",
    )(x1, kv, gq, wq, wo, gf, wr, br, tri)


def _row_copy(src_ref, src_row, dst_ref, dst_row, sem):
    return pltpu.make_async_copy(src_ref.at[pl.ds(src_row, 1), :], dst_ref.at[pl.ds(dst_row, 1), :], sem)


def _dispatch_kernel(dest_ref, h_ref, xs_in_ref, xs_ref, sem):
    del xs_in_ref
    td = h_ref.shape[0]

    def issue(r, carry):
        for k in range(TOP_K):
            _row_copy(h_ref, r, xs_ref, dest_ref[TOP_K * r + k], sem).start()
        return carry

    lax.fori_loop(0, td, issue, 0)

    def drain(r, carry):
        for k in range(TOP_K):
            _row_copy(h_ref, r, xs_ref, dest_ref[TOP_K * r + k], sem).wait()
        return carry

    lax.fori_loop(0, td, drain, 0)


def _dispatch(dest, hf, n_rows):
    n = hf.shape[0]
    td = TOK_BLOCK
    xs0 = jnp.zeros((n_rows, D_MODEL), F32)
    return pl.pallas_call(
        _dispatch_kernel,
        grid=(n // td,),
        in_specs=[
            pl.BlockSpec((td * TOP_K,), lambda i: (i,), memory_space=pltpu.SMEM),
            pl.BlockSpec((td, D_MODEL), lambda i: (i, 0)),
            pl.BlockSpec(memory_space=pl.ANY),
        ],
        out_specs=pl.BlockSpec(memory_space=pl.ANY),
        out_shape=jax.ShapeDtypeStruct((n_rows, D_MODEL), F32),
        scratch_shapes=[pltpu.SemaphoreType.DMA],
        input_output_aliases={2: 0},
        compiler_params=_params("arbitrary"),
        name="dispatch",
    )(dest, hf, xs0)


def _expert_kernel(blk_e_ref, n_used_ref, xs_ref, w1_ref, b1_ref, w2_ref, b2_ref, ys_ref):
    del blk_e_ref
    used = pl.program_id(0) < n_used_ref[0]

    @pl.when(jnp.logical_not(used))
    def _():
        ys_ref[...] = jnp.zeros_like(ys_ref)

    @pl.when(used)
    def _():
        xb = xs_ref[...].astype(BF16)
        gu = jnp.dot(xb, w1_ref[...], preferred_element_type=F32) + b1_ref[...]
        gate = jnp.minimum(gu[:, :D_FF], SWIGLU_LIMIT)
        up = jnp.clip(gu[:, D_FF:], -SWIGLU_LIMIT, SWIGLU_LIMIT)
        act = (up + 1.0) * gate * jax.nn.sigmoid(SWIGLU_ALPHA * gate)
        ys_ref[...] = jnp.dot(act.astype(BF16), w2_ref[...], preferred_element_type=F32) + b2_ref[...]


def _experts(blk_e, n_used, xs, w1, b1, w2, b2):
    n_rows = xs.shape[0]
    bm = ROW_BLOCK
    row = lambda i, be, nu: (jnp.minimum(i, nu[0] - 1), 0)
    exp3 = lambda i, be, nu: (be[jnp.minimum(i, nu[0] - 1)], 0, 0)
    grid_spec = pltpu.PrefetchScalarGridSpec(
        num_scalar_prefetch=2,
        grid=(n_rows // bm,),
        in_specs=[
            pl.BlockSpec((bm, D_MODEL), row),
            pl.BlockSpec((None, D_MODEL, 2 * D_FF), exp3),
            pl.BlockSpec((None, 1, 2 * D_FF), exp3),
            pl.BlockSpec((None, D_FF, D_MODEL), exp3),
            pl.BlockSpec((None, 1, D_MODEL), exp3),
        ],
        out_specs=pl.BlockSpec((bm, D_MODEL), lambda i, be, nu: (i, 0)),
    )
    return pl.pallas_call(
        _expert_kernel,
        grid_spec=grid_spec,
        out_shape=jax.ShapeDtypeStruct((n_rows, D_MODEL), F32),
        compiler_params=_params("arbitrary"),
        name="experts",
    )(blk_e, n_used, xs, w1, b1, w2, b2)


def _combine_kernel(dest_ref, x_ref, gate_ref, g_ref, ys_ref, o_ref, buf_ref, sem):
    tc = x_ref.shape[0]

    def issue(r, carry):
        for k in range(TOP_K):
            _row_copy(ys_ref, dest_ref[TOP_K * r + k], buf_ref.at[k], r, sem).start()
        return carry

    lax.fori_loop(0, tc, issue, 0)

    def drain(r, carry):
        for k in range(TOP_K):
            _row_copy(ys_ref, dest_ref[TOP_K * r + k], buf_ref.at[k], r, sem).wait()
        return carry

    lax.fori_loop(0, tc, drain, 0)

    gates = gate_ref[...]
    y = x_ref[...]
    for k in range(TOP_K):
        y = y + gates[:, k:k + 1] * buf_ref[k]
    o_ref[...] = _rms(y, g_ref[...])


def _combine(dest, x2, gates, g_final, ys):
    n = x2.shape[0]
    tc = TOK_BLOCK
    return pl.pallas_call(
        _combine_kernel,
        grid=(n // tc,),
        in_specs=[
            pl.BlockSpec((tc * TOP_K,), lambda i: (i,), memory_space=pltpu.SMEM),
            pl.BlockSpec((tc, D_MODEL), lambda i: (i, 0)),
            pl.BlockSpec((tc, LANES), lambda i: (i, 0)),
            pl.BlockSpec((1, D_MODEL), lambda i: (0, 0)),
            pl.BlockSpec(memory_space=pl.ANY),
        ],
        out_specs=pl.BlockSpec((tc, D_MODEL), lambda i: (i, 0)),
        out_shape=jax.ShapeDtypeStruct((n, D_MODEL), F32),
        scratch_shapes=[pltpu.VMEM((TOP_K, tc, D_MODEL), F32), pltpu.SemaphoreType.DMA],
        compiler_params=_params("arbitrary"),
        name="combine",
    )(dest, x2, gates, g_final, ys)


def _strict_lower(n):
    r = lax.broadcasted_iota(jnp.int32, (n, n), 0)
    c = lax.broadcasted_iota(jnp.int32, (n, n), 1)
    return (r > c).astype(BF16)


def _layer(x2d, mem, b, s, g_mix, w_in, g_sb_out, g_pool_out, w_pool, pool_scale, w_out,
           g_mem_q, g_mem_kv, w_mem_q, w_mem_kv, w_mem_o, g_ffn, w_router, b_router,
           w_e_in, b_e_in, w_e_out, b_e_out, g_out):
    n = b * s
    row = lambda v: v.reshape(1, -1).astype(F32)

    proj = _in_proj(x2d, row(g_mix), w_in.astype(BF16))
    sb = _sb_attention(proj.reshape(b, s, IN_PROJ_COLS), _strict_lower(SB_BLOCK))
    x1 = _mix_out(x2d, sb.reshape(n, SB_WIDTH), proj, w_pool.astype(BF16), row(pool_scale),
                  row(g_sb_out), row(g_pool_out), w_out.astype(BF16), s)

    kv = _mem_kv(mem, row(g_mem_kv), w_mem_kv.astype(BF16))
    wr = jnp.zeros((D_MODEL, LANES), F32).at[:, :N_EXPERTS].set(w_router)
    br = jnp.full((1, LANES), NEG_BIG, F32).at[0, :N_EXPERTS].set(b_router)
    x2, hf, meta, gates, cnt = _xattn_router(
        x1, kv, row(g_mem_q), w_mem_q.astype(BF16), w_mem_o.astype(BF16), row(g_ffn), wr, br,
        _strict_lower(TOK_BLOCK), s)

    bm = ROW_BLOCK
    n_rows = n * TOP_K + N_EXPERTS * bm
    n_blk = n_rows // bm
    counts = cnt[0, :N_EXPERTS].astype(jnp.int32)
    padded = (counts + bm - 1) // bm * bm
    pend = jnp.cumsum(padded)
    pstart = pend - padded
    e_idx = meta[:, :TOP_K]
    rank = meta[:, TOP_K:2 * TOP_K]
    onehot = e_idx[:, :, None] == jnp.arange(N_EXPERTS, dtype=jnp.int32)[None, None, :]
    dest = (jnp.sum(jnp.where(onehot, pstart[None, None, :], 0), axis=-1) + rank).reshape(-1)
    n_used = (pend[-1:] // bm).astype(jnp.int32)
    blk_start = jnp.arange(n_blk, dtype=jnp.int32) * bm
    blk_e = jnp.sum(blk_start[:, None] >= pend[None, :], axis=1).astype(jnp.int32)
    blk_e = jnp.minimum(blk_e, N_EXPERTS - 1)

    xs = _dispatch(dest, hf, n_rows)
    ys = _experts(blk_e, n_used, xs, w_e_in.astype(BF16), b_e_in.reshape(N_EXPERTS, 1, -1),
                  w_e_out.astype(BF16), b_e_out.reshape(N_EXPERTS, 1, -1))
    return _combine(dest, x2, gates, row(g_out), ys)


def kernel(x, mem, g_mix, w_in, g_sb_out, g_pool_out, w_pool, pool_scale, w_out, g_mem_q, g_mem_kv,
           w_mem_q, w_mem_kv, w_mem_o, g_ffn, w_router, b_router, w_expert_in, b_expert_in,
           w_expert_out, b_expert_out, g_final):
    b, s, d = x.shape
    depth = g_mix.shape[0]
    assert d == D_MODEL and depth == 1, "the final RMSNorm is fused into the single layer's combine"
    assert s % PROJ_BLOCK == 0 and s % SB_BLOCK == 0 and s % TOK_BLOCK == 0
    out = _layer(x.reshape(b * s, d), mem, b, s, g_mix[0], w_in[0], g_sb_out[0], g_pool_out[0],
                 w_pool[0], pool_scale[0], w_out[0], g_mem_q[0], g_mem_kv[0], w_mem_q[0],
                 w_mem_kv[0], w_mem_o[0], g_ffn[0], w_router[0], b_router[0], w_expert_in[0],
                 b_expert_in[0], w_expert_out[0], b_expert_out[0], g_final)
    return out.reshape(b, s, d)
```

```python
import functools
import math

import jax
import jax.numpy as jnp
from jax import lax
from jax.experimental import pallas as pl
from jax.experimental.pallas import tpu as pltpu

F32 = jnp.float32
BF16 = jnp.bfloat16
I32 = jnp.int32

D_MODEL = 1024
SB_HEADS = 8
SB_HEAD_DIM = 64
SB_WIDTH = SB_HEADS * SB_HEAD_DIM
POOL_WINDOWS = (2, 4, 8, 16)
POOL_GROUP_DIM = 128
POOL_WIDTH = len(POOL_WINDOWS) * POOL_GROUP_DIM
IN_PROJ_COLS = 3 * SB_WIDTH + POOL_WIDTH
N_MEM = 256
MEM_HEADS = 4
MEM_HEAD_DIM = D_MODEL // MEM_HEADS
N_EXPERTS = 32
TOP_K = 4
D_FF = D_MODEL
SWIGLU_LIMIT = 7.0
SWIGLU_ALPHA = 1.702
RMS_EPS = 1e-5

LANES = 128
SUBLANES = 8
POOL_HALO = 16
SB_BLOCK = 256
SB_DEAD_LOG = -110.0
ROW_BLOCK = 512
TOK_BLOCK = 256
XATTN_BLOCKS = 4
XATTN_CHAIN = 2
SORT_BLOCKS = 2
PROJ_BLOCK = 512
SORT_ROWS = -(-(TOK_BLOCK * TOP_K + N_EXPERTS * SUBLANES) // LANES) * LANES
VMEM_LIMIT = 48 * 1024 * 1024
NEG_BIG = -1e30


def _rms(x, g):
    ms = jnp.mean(x * x, axis=-1, keepdims=True)
    return x * lax.rsqrt(ms + RMS_EPS) * g


def _params(*sem):
    return pltpu.CompilerParams(dimension_semantics=sem, vmem_limit_bytes=VMEM_LIMIT)


def _in_proj_kernel(x_ref, g_ref, w_ref, o_ref):
    h = _rms(x_ref[...], g_ref[...]).astype(BF16)
    acc = jnp.dot(h, w_ref[...], preferred_element_type=F32)
    scale = 1.0 / math.sqrt(SB_HEAD_DIM)
    o_ref[:, :SB_WIDTH] = (acc[:, :SB_WIDTH] * scale).astype(BF16)
    o_ref[:, SB_WIDTH:] = acc[:, SB_WIDTH:].astype(BF16)


def _in_proj(x2d, g, w):
    n = x2d.shape[0]
    tm = PROJ_BLOCK
    return pl.pallas_call(
        _in_proj_kernel,
        grid=(n // tm,),
        in_specs=[
            pl.BlockSpec((tm, D_MODEL), lambda i: (i, 0)),
            pl.BlockSpec((1, D_MODEL), lambda i: (0, 0)),
            pl.BlockSpec((D_MODEL, IN_PROJ_COLS), lambda i: (0, 0)),
        ],
        out_specs=pl.BlockSpec((tm, IN_PROJ_COLS), lambda i: (i, 0)),
        out_shape=jax.ShapeDtypeStruct((n, IN_PROJ_COLS), BF16),
        compiler_params=_params("arbitrary"),
        name="in_proj",
    )(x2d, g, w)


def _sb_chain(qh, kb, vb, neg_tri, c, causal):
    z = lax.dot_general(qh, kb, (((1,), (1,)), ((), ())), preferred_element_type=F32)
    sp = jnp.maximum(z, 0.0) + jnp.log(1.0 + jnp.exp(-jnp.abs(z)))
    if causal is not None:
        sp = jnp.where(causal, sp, 0.0)
    after = jnp.dot(sp.astype(BF16), neg_tri, preferred_element_type=F32)
    w = jnp.exp((z - sp) + after)
    if causal is not None:
        w = jnp.where(causal, w, 0.0)
    pv = jnp.dot(w.astype(BF16), vb, preferred_element_type=F32) * jnp.exp(c)
    return pv, c + (after[:, :1] - sp[:, :1])


def _sb_kernel(q_ref, k_ref, v_ref, tri_ref, o_ref, acc_ref, c_ref):
    tq = q_ref.shape[0]
    i = pl.program_id(2)
    lane = lax.broadcasted_iota(I32, (tq, LANES), 1)
    q = q_ref[...]
    zero = jnp.zeros_like(q)
    qs = jnp.concatenate(
        [jnp.where(lane < SB_HEAD_DIM, q, zero), jnp.where(lane >= SB_HEAD_DIM, q, zero)], axis=0)
    neg_tri = tri_ref[...]
    row = lax.broadcasted_iota(I32, (2 * tq, tq), 0)
    col = lax.broadcasted_iota(I32, (2 * tq, tq), 1)
    causal = col < jnp.where(row >= tq, row - tq, row)

    def step(j, c, diagonal):
        off = pl.multiple_of(j * tq, tq)
        return _sb_chain(qs, k_ref[pl.ds(off, tq), :], v_ref[pl.ds(off, tq), :], neg_tri, c,
                         causal if diagonal else None)

    pv, c = step(i, jnp.zeros((2 * tq, LANES), F32), True)
    pv_left, c_left = step(jnp.maximum(i - 1, 0), c, False)
    has_left = i > 0
    acc_ref[...] = pv + jnp.where(has_left, pv_left, 0.0)
    c_ref[...] = jnp.where(has_left, c_left, c)

    def alive():
        return jnp.max(c_ref[...]) > SB_DEAD_LOG

    def cond(carry):
        j, live = carry
        return jnp.logical_and(j >= 0, live)

    def body(carry):
        j, _ = carry
        pv, c = step(j, c_ref[...], False)
        acc_ref[...] += pv
        c_ref[...] = c
        return j - 1, alive()

    lax.while_loop(cond, body, (i - 2, alive()))
    o_ref[...] = jnp.where(lane < SB_HEAD_DIM, acc_ref[:tq, :], acc_ref[tq:, :]).astype(o_ref.dtype)


def _sb_attention(proj3, neg_tri):
    b, s, _ = proj3.shape
    tq = SB_BLOCK
    pairs = SB_WIDTH // LANES
    return pl.pallas_call(
        _sb_kernel,
        grid=(b, pairs, s // tq),
        in_specs=[
            pl.BlockSpec((None, tq, LANES), lambda bi, p, i: (bi, i, p)),
            pl.BlockSpec((None, s, LANES), lambda bi, p, i: (bi, 0, pairs + p)),
            pl.BlockSpec((None, s, LANES), lambda bi, p, i: (bi, 0, 2 * pairs + p)),
            pl.BlockSpec((tq, tq), lambda bi, p, i: (0, 0)),
        ],
        out_specs=pl.BlockSpec((None, tq, LANES), lambda bi, p, i: (bi, i, p)),
        out_shape=jax.ShapeDtypeStruct((b, s, SB_WIDTH), BF16),
        scratch_shapes=[pltpu.VMEM((2 * tq, LANES), F32), pltpu.VMEM((2 * tq, LANES), F32)],
        compiler_params=_params("arbitrary", "arbitrary", "arbitrary"),
        name="sb_attn",
    )(proj3, proj3, proj3, neg_tri)


def _mix_kernel(x_ref, sb_ref, u_ref, halo_ref, wp_ref, ps_ref, gsb_ref, gpool_ref, wout_ref,
                o_ref, *, seq):
    tm = x_ref.shape[0]
    t0 = (pl.program_id(0) * tm) % seq
    u = u_ref[...].astype(F32)
    halo = jnp.where(t0 == 0, 0.0, halo_ref[...].astype(F32))
    a = jnp.concatenate([halo, u], axis=0)
    pos = t0 + lax.broadcasted_iota(I32, (tm, POOL_GROUP_DIM), 0)
    outs = []
    for g, win in enumerate(POOL_WINDOWS):
        lo, hi = g * POOL_GROUP_DIM, (g + 1) * POOL_GROUP_DIM
        s = a[:, lo:hi]
        sh = 1
        while sh < win:
            s = s + pltpu.roll(s, sh, axis=0)
            sh *= 2
        cnt = jnp.minimum(pos + 1, win).astype(F32)
        pooled = s[POOL_HALO:, :] / cnt - u[:, lo:hi]
        outs.append(jnp.dot(pooled.astype(BF16), wp_ref[g], preferred_element_type=F32))
    pooled = jnp.concatenate(outs, axis=1) * ps_ref[...]
    pool_n = _rms(pooled, gpool_ref[...]).astype(BF16)
    sb_n = _rms(sb_ref[...].astype(F32), gsb_ref[...]).astype(BF16)
    y = jnp.dot(sb_n, wout_ref[:SB_WIDTH, :], preferred_element_type=F32)
    y = y + jnp.dot(pool_n, wout_ref[SB_WIDTH:, :], preferred_element_type=F32)
    o_ref[...] = x_ref[...] + y


def _mix_out(x2d, sb2d, proj2d, wp, ps, gsb, gpool, wout, seq):
    n = x2d.shape[0]
    tm = PROJ_BLOCK
    ucol = 3 * SB_WIDTH // POOL_WIDTH
    hb = tm // POOL_HALO
    return pl.pallas_call(
        functools.partial(_mix_kernel, seq=seq),
        grid=(n // tm,),
        in_specs=[
            pl.BlockSpec((tm, D_MODEL), lambda i: (i, 0)),
            pl.BlockSpec((tm, SB_WIDTH), lambda i: (i, 0)),
            pl.BlockSpec((tm, POOL_WIDTH), lambda i: (i, ucol)),
            pl.BlockSpec((POOL_HALO, POOL_WIDTH), lambda i: (jnp.maximum(i * hb - 1, 0), ucol)),
            pl.BlockSpec((len(POOL_WINDOWS), POOL_GROUP_DIM, POOL_GROUP_DIM), lambda i: (0, 0, 0)),
            pl.BlockSpec((1, POOL_WIDTH), lambda i: (0, 0)),
            pl.BlockSpec((1, SB_WIDTH), lambda i: (0, 0)),
            pl.BlockSpec((1, POOL_WIDTH), lambda i: (0, 0)),
            pl.BlockSpec((D_MODEL, D_MODEL), lambda i: (0, 0)),
        ],
        out_specs=pl.BlockSpec((tm, D_MODEL), lambda i: (i, 0)),
        out_shape=jax.ShapeDtypeStruct((n, D_MODEL), F32),
        compiler_params=_params("arbitrary"),
        name="mix_out",
    )(x2d, sb2d, proj2d, proj2d, wp, ps, gsb, gpool, wout)


def _mem_kv_kernel(m_ref, g_ref, w_ref, o_ref):
    h = _rms(m_ref[...], g_ref[...]).astype(BF16)
    o_ref[...] = jnp.dot(h, w_ref[...], preferred_element_type=F32).astype(BF16)


def _mem_kv(mem, g, w):
    b = mem.shape[0]
    return pl.pallas_call(
        _mem_kv_kernel,
        grid=(b,),
        in_specs=[
            pl.BlockSpec((None, N_MEM, D_MODEL), lambda i: (i, 0, 0)),
            pl.BlockSpec((1, D_MODEL), lambda i: (0, 0)),
            pl.BlockSpec((D_MODEL, 2 * D_MODEL), lambda i: (0, 0)),
        ],
        out_specs=pl.BlockSpec((None, N_MEM, 2 * D_MODEL), lambda i: (i, 0, 0)),
        out_shape=jax.ShapeDtypeStruct((b, N_MEM, 2 * D_MODEL), BF16),
        compiler_params=_params("arbitrary"),
        name="mem_kv",
    )(mem, g, w)


def _xattn_block(x, kv_ref, gq, wq_ref, wo_ref, gf, wr_ref, br, tri):
    tm = x.shape[0]
    hq = _rms(x, gq).astype(BF16)
    q = jnp.dot(hq, wq_ref[...], preferred_element_type=F32) * (1.0 / math.sqrt(MEM_HEAD_DIM))
    q = q.astype(BF16)
    outs = []
    for h in range(MEM_HEADS):
        lo, hi = h * MEM_HEAD_DIM, (h + 1) * MEM_HEAD_DIM
        s = lax.dot_general(q[:, lo:hi], kv_ref[:, lo:hi], (((1,), (1,)), ((), ())),
                            preferred_element_type=F32)
        p = jnp.exp(s - jnp.max(s, axis=-1, keepdims=True))
        denom = jnp.sum(p, axis=-1, keepdims=True)
        o = jnp.dot(p.astype(BF16), kv_ref[:, D_MODEL + lo:D_MODEL + hi], preferred_element_type=F32)
        outs.append(o / denom)
    o = jnp.concatenate(outs, axis=1).astype(BF16)
    x2 = x + jnp.dot(o, wo_ref[...], preferred_element_type=F32)
    hf = _rms(x2, gf)

    h_hi = hf.astype(BF16)
    h_lo = (hf - h_hi.astype(F32)).astype(BF16)
    hw = jnp.dot(h_hi, wr_ref[...], preferred_element_type=F32)
    lw = jnp.dot(h_lo, wr_ref[:, :LANES], preferred_element_type=F32)
    logits = (hw[:, :LANES] + (hw[:, LANES:] + lw)) + br

    lane = lax.broadcasted_iota(I32, (tm, LANES), 1).astype(F32)
    work = logits
    vals, idxs, sels = [], [], []
    for _ in range(TOP_K):
        m = jnp.max(work, axis=-1, keepdims=True)
        idx = jnp.min(jnp.where(work == m, lane, float(LANES)), axis=-1, keepdims=True)
        sel = lane == idx
        vals.append(m)
        idxs.append(idx)
        sels.append(sel)
        work = jnp.where(sel, -3e38, work)
    exps = [jnp.exp(v - vals[0]) for v in vals]
    den = exps[0] + exps[1] + exps[2] + exps[3]

    onehot = jnp.zeros((tm, LANES), F32)
    for sel in sels:
        onehot = onehot + sel.astype(F32)
    blocks = [onehot[r:r + TOK_BLOCK, :] for r in range(0, tm, TOK_BLOCK)]
    before = jnp.concatenate(
        [jnp.dot(tri, blk.astype(BF16), preferred_element_type=F32) for blk in blocks], axis=0)
    meta = jnp.zeros((tm, LANES), F32)
    gates = jnp.zeros((tm, LANES), F32)
    for k in range(TOP_K):
        rank = jnp.sum(jnp.where(sels[k], before, 0.0), axis=-1, keepdims=True)
        meta = jnp.where(lane == float(k), idxs[k], meta)
        meta = jnp.where(lane == float(TOP_K + k), rank, meta)
        gates = jnp.where(lane == float(k), exps[k] / den, gates)
    counts = [jnp.sum(blk, axis=0, keepdims=True).astype(I32) for blk in blocks]
    return x2, hf.astype(BF16), meta.astype(I32), gates, counts


def _xattn_kernel(x_ref, kv_ref, gq_ref, wq_ref, wo_ref, gf_ref, wr_ref, br_ref, tri_ref,
                  x2_ref, h_ref, meta_ref, gate_ref, cnt_ref):
    tb = TOK_BLOCK * XATTN_CHAIN
    for chain in range(XATTN_BLOCKS // XATTN_CHAIN):
        rows = slice(chain * tb, (chain + 1) * tb)
        x2, hf, meta, gates, counts = _xattn_block(
            x_ref[rows, :], kv_ref, gq_ref[...], wq_ref, wo_ref, gf_ref[...], wr_ref, br_ref[...],
            tri_ref[...])
        x2_ref[rows, :] = x2
        h_ref[rows, :] = hf
        meta_ref[rows, :] = meta
        gate_ref[rows, :] = gates
        for r, c in enumerate(counts):
            cnt_ref[chain * XATTN_CHAIN + r] = c


def _xattn_router(x1, kv, gq, wq, wo, gf, wr, br, tri, seq):
    n = x1.shape[0]
    tm = TOK_BLOCK * XATTN_BLOCKS
    per_seq = seq // tm
    const = lambda i: (0, 0)
    return pl.pallas_call(
        _xattn_kernel,
        grid=(n // tm,),
        in_specs=[
            pl.BlockSpec((tm, D_MODEL), lambda i: (i, 0)),
            pl.BlockSpec((None, N_MEM, 2 * D_MODEL), lambda i: (i // per_seq, 0, 0)),
            pl.BlockSpec((1, D_MODEL), const),
            pl.BlockSpec((D_MODEL, D_MODEL), const),
            pl.BlockSpec((D_MODEL, D_MODEL), const),
            pl.BlockSpec((1, D_MODEL), const),
            pl.BlockSpec((D_MODEL, 2 * LANES), const),
            pl.BlockSpec((1, LANES), const),
            pl.BlockSpec((TOK_BLOCK, TOK_BLOCK), const),
        ],
        out_specs=[
            pl.BlockSpec((tm, D_MODEL), lambda i: (i, 0)),
            pl.BlockSpec((tm, D_MODEL), lambda i: (i, 0)),
            pl.BlockSpec((tm, LANES), lambda i: (i, 0)),
            pl.BlockSpec((tm, LANES), lambda i: (i, 0)),
            pl.BlockSpec((XATTN_BLOCKS, 1, LANES), lambda i: (i, 0, 0)),
        ],
        out_shape=[
            jax.ShapeDtypeStruct((n, D_MODEL), F32),
            jax.ShapeDtypeStruct((n, D_MODEL), BF16),
            jax.ShapeDtypeStruct((n, LANES), I32),
            jax.ShapeDtypeStruct((n, LANES), F32),
            jax.ShapeDtypeStruct((n // TOK_BLOCK, 1, LANES), I32),
        ],
        compiler_params=_params("arbitrary"),
        name="xattn_router",
    )(x1, kv, gq, wq, wo, gf, wr, br, tri)


def _segment_copy(tables, step, e, vmem_ref, hbm_ref, sem, to_hbm):
    src_ref, dst_ref, cnt_ref = tables
    at = step * N_EXPERTS + e
    n = pl.multiple_of(cnt_ref[at], SUBLANES)
    local = vmem_ref.at[pl.ds(pl.multiple_of(src_ref[at], SUBLANES), n), :]
    remote = hbm_ref.at[pl.ds(pl.multiple_of(dst_ref[at], SUBLANES), n), :]
    return pltpu.make_async_copy(local, remote, sem) if to_hbm else pltpu.make_async_copy(remote, local, sem)


def _each_segment(fn):
    for r in range(SORT_BLOCKS):
        for e in range(N_EXPERTS):
            fn(r, e)


def _dispatch_kernel(src_ref, dst_ref, cnt_ref, fill_ref, h_ref, pos_ref, xs_ref, stage_ref, sem,
                     fill_sem):
    tables = (src_ref, dst_ref, cnt_ref)
    step = pl.program_id(0)
    last = pl.num_programs(0) - 1
    parity = step % 2
    where = lax.broadcasted_iota(I32, (TOK_BLOCK, SORT_ROWS), 1)
    for r in range(SORT_BLOCKS):
        rows = slice(r * TOK_BLOCK, (r + 1) * TOK_BLOCK)
        pos = pos_ref[rows, :]
        hit = where == pos[:, 0:1]
        for k in range(1, TOP_K):
            hit = jnp.logical_or(hit, where == pos[:, k:k + 1])
        onehot = jnp.where(hit, 1.0, 0.0).astype(BF16)
        stage_ref[parity * SORT_BLOCKS + r] = lax.dot_general(
            onehot, h_ref[rows, :], (((0,), (0,)), ((), ())), preferred_element_type=F32)

    def copy(at_step, at_parity, r, e):
        return _segment_copy(tables, at_step * SORT_BLOCKS + r, e,
                             stage_ref.at[at_parity * SORT_BLOCKS + r], xs_ref, sem.at[at_parity], True)

    _each_segment(lambda r, e: copy(step, parity, r, e).start())

    @pl.when(step > 0)
    def _():
        _each_segment(lambda r, e: copy(step - 1, 1 - parity, r, e).wait())

    @pl.when(step == last)
    def _():
        _each_segment(lambda r, e: copy(step, parity, r, e).wait())

    @pl.when(step == 0)
    def _():
        stage_ref[SORT_BLOCKS] = jnp.zeros((SORT_ROWS, D_MODEL), F32)
        zeros = stage_ref.at[SORT_BLOCKS]

        def pad_copy(e):
            n = pl.multiple_of(fill_ref[N_EXPERTS + e], SUBLANES)
            at = pl.multiple_of(fill_ref[e], SUBLANES)
            return pltpu.make_async_copy(zeros.at[pl.ds(0, n), :], xs_ref.at[pl.ds(at, n), :], fill_sem)

        def tail_copy(t):
            at = pl.multiple_of(fill_ref[2 * N_EXPERTS] + t * ROW_BLOCK, ROW_BLOCK)
            return pltpu.make_async_copy(zeros.at[pl.ds(0, ROW_BLOCK), :],
                                         xs_ref.at[pl.ds(at, ROW_BLOCK), :], fill_sem)

        def each_pad(fn):
            def body(e, carry):
                @pl.when(fill_ref[N_EXPERTS + e] > 0)
                def _():
                    fn(e)
                return carry
            lax.fori_loop(0, N_EXPERTS, body, 0)

        def each_tail(fn):
            def body(t, carry):
                fn(t)
                return carry
            lax.fori_loop(0, fill_ref[2 * N_EXPERTS + 1], body, 0)

        each_pad(lambda e: pad_copy(e).start())
        each_tail(lambda t: tail_copy(t).start())
        each_pad(lambda e: pad_copy(e).wait())
        each_tail(lambda t: tail_copy(t).wait())


def _dispatch(tables, fill, hf, pos, n_rows):
    n = hf.shape[0]
    tm = TOK_BLOCK * SORT_BLOCKS
    grid_spec = pltpu.PrefetchScalarGridSpec(
        num_scalar_prefetch=4,
        grid=(n // tm,),
        in_specs=[
            pl.BlockSpec((tm, D_MODEL), lambda i, *_: (i, 0)),
            pl.BlockSpec((tm, LANES), lambda i, *_: (i, 0)),
        ],
        out_specs=pl.BlockSpec(memory_space=pl.ANY),
        scratch_shapes=[pltpu.VMEM((2 * SORT_BLOCKS, SORT_ROWS, D_MODEL), F32),
                        pltpu.SemaphoreType.DMA((2,)), pltpu.SemaphoreType.DMA],
    )
    return pl.pallas_call(
        _dispatch_kernel,
        grid_spec=grid_spec,
        out_shape=jax.ShapeDtypeStruct((n_rows, D_MODEL), F32),
        compiler_params=_params("arbitrary"),
        name="dispatch",
    )(*tables, fill, hf, pos)


def _expert_kernel(blk_e_ref, n_used_ref, xs_ref, w1_ref, b1_ref, w2_ref, b2_ref,
                   ys_ref, w1b_ref, w2b_ref):
    i = pl.program_id(0)
    used = i < n_used_ref[0]

    @pl.when(jnp.logical_not(used))
    def _():
        ys_ref[...] = jnp.zeros_like(ys_ref)

    @pl.when(used)
    def _():
        @pl.when(jnp.logical_or(i == 0, blk_e_ref[i] != blk_e_ref[jnp.maximum(i - 1, 0)]))
        def _():
            w1b_ref[...] = w1_ref[...].astype(BF16)
            w2b_ref[...] = w2_ref[...].astype(BF16)

        gu = jnp.dot(xs_ref[...].astype(BF16), w1b_ref[...], preferred_element_type=F32) + b1_ref[...]
        gate = jnp.minimum(gu[:, :D_FF], SWIGLU_LIMIT)
        up = jnp.clip(gu[:, D_FF:], -SWIGLU_LIMIT, SWIGLU_LIMIT)
        act = (up + 1.0) * gate * jax.nn.sigmoid(SWIGLU_ALPHA * gate)
        ys_ref[...] = jnp.dot(act.astype(BF16), w2b_ref[...], preferred_element_type=F32) + b2_ref[...]


def _experts(blk_e, n_used, xs, w1, b1, w2, b2):
    n_rows = xs.shape[0]
    bm = ROW_BLOCK
    last_used = lambda i, nu: jnp.maximum(jnp.minimum(i, nu[0] - 1), 0)
    row = lambda i, be, nu: (last_used(i, nu), 0)
    exp3 = lambda i, be, nu: (be[last_used(i, nu)], 0, 0)
    grid_spec = pltpu.PrefetchScalarGridSpec(
        num_scalar_prefetch=2,
        grid=(n_rows // bm,),
        in_specs=[
            pl.BlockSpec((bm, D_MODEL), row),
            pl.BlockSpec((None, D_MODEL, 2 * D_FF), exp3),
            pl.BlockSpec((None, 1, 2 * D_FF), exp3),
            pl.BlockSpec((None, D_FF, D_MODEL), exp3),
            pl.BlockSpec((None, 1, D_MODEL), exp3),
        ],
        out_specs=pl.BlockSpec((bm, D_MODEL), lambda i, be, nu: (i, 0)),
        scratch_shapes=[pltpu.VMEM((D_MODEL, 2 * D_FF), BF16), pltpu.VMEM((D_FF, D_MODEL), BF16)],
    )
    return pl.pallas_call(
        _expert_kernel,
        grid_spec=grid_spec,
        out_shape=jax.ShapeDtypeStruct((n_rows, D_MODEL), F32),
        compiler_params=_params("arbitrary"),
        name="experts",
    )(blk_e, n_used, xs, w1, b1, w2, b2)


def _combine_kernel(src_ref, dst_ref, cnt_ref, x_ref, gate_ref, pos_ref, g_ref, ys_ref, o_ref,
                    stage_ref, sem):
    tables = (src_ref, dst_ref, cnt_ref)
    step = pl.program_id(0)
    last = pl.num_programs(0) - 1
    parity = step % 2

    def copy(at_step, at_parity, r, e):
        return _segment_copy(tables, at_step * SORT_BLOCKS + r, e,
                             stage_ref.at[at_parity * SORT_BLOCKS + r], ys_ref, sem.at[at_parity], False)

    @pl.when(step == 0)
    def _():
        stage_ref[...] = jnp.zeros_like(stage_ref)
        _each_segment(lambda r, e: copy(step, parity, r, e).start())

    @pl.when(step < last)
    def _():
        _each_segment(lambda r, e: copy(step + 1, 1 - parity, r, e).start())

    _each_segment(lambda r, e: copy(step, parity, r, e).wait())

    where = lax.broadcasted_iota(I32, (TOK_BLOCK, SORT_ROWS), 1)
    for r in range(SORT_BLOCKS):
        rows = slice(r * TOK_BLOCK, (r + 1) * TOK_BLOCK)
        pos = pos_ref[rows, :]
        gates = gate_ref[rows, :]
        weights = jnp.zeros((TOK_BLOCK, SORT_ROWS), F32)
        for k in range(TOP_K):
            weights = jnp.where(where == pos[:, k:k + 1], gates[:, k:k + 1], weights)
        y = x_ref[rows, :] + jnp.dot(weights.astype(BF16),
                                     stage_ref[parity * SORT_BLOCKS + r].astype(BF16),
                                     preferred_element_type=F32)
        o_ref[rows, :] = _rms(y, g_ref[...])


def _combine(tables, x2, gates, pos, g_final, ys):
    n = x2.shape[0]
    tc = TOK_BLOCK * SORT_BLOCKS
    grid_spec = pltpu.PrefetchScalarGridSpec(
        num_scalar_prefetch=3,
        grid=(n // tc,),
        in_specs=[
            pl.BlockSpec((tc, D_MODEL), lambda i, *_: (i, 0)),
            pl.BlockSpec((tc, LANES), lambda i, *_: (i, 0)),
            pl.BlockSpec((tc, LANES), lambda i, *_: (i, 0)),
            pl.BlockSpec((1, D_MODEL), lambda i, *_: (0, 0)),
            pl.BlockSpec(memory_space=pl.ANY),
        ],
        out_specs=pl.BlockSpec((tc, D_MODEL), lambda i, *_: (i, 0)),
        scratch_shapes=[pltpu.VMEM((2 * SORT_BLOCKS, SORT_ROWS, D_MODEL), F32),
                        pltpu.SemaphoreType.DMA((2,))],
    )
    return pl.pallas_call(
        _combine_kernel,
        grid_spec=grid_spec,
        out_shape=jax.ShapeDtypeStruct((n, D_MODEL), F32),
        compiler_params=_params("arbitrary"),
        name="combine",
    )(*tables, x2, gates, pos, g_final, ys)


def _strict_lower(n):
    r = lax.broadcasted_iota(I32, (n, n), 0)
    c = lax.broadcasted_iota(I32, (n, n), 1)
    return (r > c).astype(BF16)


def _round_up(v, m):
    return (v + m - 1) // m * m


def _routing_tables(meta, cnt, n):
    bm = ROW_BLOCK
    n_tok_blk = n // TOK_BLOCK
    n_rows = _round_up(n * TOP_K + n_tok_blk * N_EXPERTS * SUBLANES, bm) + N_EXPERTS * bm
    n_blk = n_rows // bm
    seg = _round_up(jnp.maximum(cnt[:, 0, :N_EXPERTS], 1), SUBLANES)
    local_start = jnp.cumsum(seg, axis=1) - seg
    total = jnp.sum(seg, axis=0)
    padded = _round_up(total, bm)
    pend = jnp.cumsum(padded)
    pstart = pend - padded
    global_start = pstart[None, :] + jnp.cumsum(seg, axis=0) - seg
    tables = (local_start.reshape(-1), global_start.reshape(-1), seg.reshape(-1))

    e_idx = meta[:, :TOP_K]
    rank = meta[:, TOP_K:2 * TOP_K]
    tok_start = jnp.repeat(local_start, TOK_BLOCK, axis=0)
    onehot = e_idx[:, :, None] == jnp.arange(N_EXPERTS, dtype=I32)[None, None, :]
    pos = jnp.sum(jnp.where(onehot, tok_start[:, None, :], 0), axis=-1) + rank
    pos = jnp.pad(pos, ((0, 0), (0, LANES - TOP_K)), constant_values=-1)

    n_used = (pend[-1:] // bm).astype(I32)
    blk_row = jnp.arange(n_blk, dtype=I32) * bm
    blk_e = jnp.minimum(jnp.sum(blk_row[:, None] >= pend[None, :], axis=1), N_EXPERTS - 1).astype(I32)
    fill = jnp.concatenate([pstart + total, padded - total, pend[-1:], n_blk - pend[-1:] // bm])
    return tables, fill.astype(I32), pos.astype(I32), blk_e, n_used, n_rows


def _layer(x2d, mem, b, s, g_mix, w_in, g_sb_out, g_pool_out, w_pool, pool_scale, w_out,
           g_mem_q, g_mem_kv, w_mem_q, w_mem_kv, w_mem_o, g_ffn, w_router, b_router,
           w_e_in, b_e_in, w_e_out, b_e_out, g_out):
    n = b * s
    row = lambda v: v.reshape(1, -1).astype(F32)

    proj = _in_proj(x2d, row(g_mix), w_in.astype(BF16))
    sb = _sb_attention(proj.reshape(b, s, IN_PROJ_COLS), -_strict_lower(SB_BLOCK))
    x1 = _mix_out(x2d, sb.reshape(n, SB_WIDTH), proj, w_pool.astype(BF16), row(pool_scale),
                  row(g_sb_out), row(g_pool_out), w_out.astype(BF16), s)

    kv = _mem_kv(mem, row(g_mem_kv), w_mem_kv.astype(BF16))
    wr = jnp.zeros((D_MODEL, LANES), F32).at[:, :N_EXPERTS].set(w_router)
    wr_hi = wr.astype(BF16)
    wr_lo = (wr - wr_hi.astype(F32)).astype(BF16)
    br = jnp.full((1, LANES), NEG_BIG, F32).at[0, :N_EXPERTS].set(b_router)
    x2, hf, meta, gates, cnt = _xattn_router(
        x1, kv, row(g_mem_q), w_mem_q.astype(BF16), w_mem_o.astype(BF16), row(g_ffn),
        jnp.concatenate([wr_hi, wr_lo], axis=1), br, _strict_lower(TOK_BLOCK), s)

    tables, fill, pos, blk_e, n_used, n_rows = _routing_tables(meta, cnt, n)
    xs = _dispatch(tables, fill, hf, pos, n_rows)
    ys = _experts(blk_e, n_used, xs, w_e_in, b_e_in.reshape(N_EXPERTS, 1, -1),
                  w_e_out, b_e_out.reshape(N_EXPERTS, 1, -1))
    return _combine(tables, x2, gates, pos, row(g_out), ys)


def kernel(x, mem, g_mix, w_in, g_sb_out, g_pool_out, w_pool, pool_scale, w_out, g_mem_q, g_mem_kv,
           w_mem_q, w_mem_kv, w_mem_o, g_ffn, w_router, b_router, w_expert_in, b_expert_in,
           w_expert_out, b_expert_out, g_final):
    b, s, d = x.shape
    depth = g_mix.shape[0]
    assert d == D_MODEL and depth == 1, "the final RMSNorm is fused into the single layer's combine"
    assert s % PROJ_BLOCK == 0 and s % SB_BLOCK == 0 and s % (TOK_BLOCK * XATTN_BLOCKS) == 0
    assert (b * s) % (TOK_BLOCK * SORT_BLOCKS) == 0
    out = _layer(x.reshape(b * s, d), mem, b, s, g_mix[0], w_in[0], g_sb_out[0], g_pool_out[0],
                 w_pool[0], pool_scale[0], w_out[0], g_mem_q[0], g_mem_kv[0], w_mem_q[0],
                 w_mem_kv[0], w_mem_o[0], g_ffn[0], w_router[0], b_router[0], w_expert_in[0],
                 b_expert_in[0], w_expert_out[0], b_expert_out[0], g_final)
    return out.reshape(b, s, d)
```

```python
import functools
import math

import jax
import jax.numpy as jnp
from jax import lax
from jax.experimental import pallas as pl
from jax.experimental.pallas import tpu as pltpu

F32 = jnp.float32
BF16 = jnp.bfloat16
I32 = jnp.int32

D_MODEL = 1024
SB_HEADS = 8
SB_HEAD_DIM = 64
SB_WIDTH = SB_HEADS * SB_HEAD_DIM
POOL_WINDOWS = (2, 4, 8, 16)
POOL_GROUP_DIM = 128
POOL_WIDTH = len(POOL_WINDOWS) * POOL_GROUP_DIM
IN_PROJ_COLS = 3 * SB_WIDTH + POOL_WIDTH
N_MEM = 256
MEM_HEADS = 4
MEM_HEAD_DIM = D_MODEL // MEM_HEADS
N_EXPERTS = 32
TOP_K = 4
D_FF = D_MODEL
SWIGLU_LIMIT = 7.0
SWIGLU_ALPHA = 1.702
RMS_EPS = 1e-5

LANES = 128
SUBLANES = 8
POOL_HALO = 16
SB_BLOCK = 256
SB_QBLOCKS = 2
SB_DEAD_LOG = -110.0
ROW_BLOCK = 512
TOK_BLOCK = 256
XATTN_BLOCKS = 4
XATTN_CHAIN = 2
SORT_BLOCKS = 2
PROJ_BLOCK = 512
SORT_ROWS = -(-(TOK_BLOCK * TOP_K + N_EXPERTS * SUBLANES) // LANES) * LANES
VMEM_LIMIT = 48 * 1024 * 1024
NEG_BIG = -1e30


def _rms(x, g):
    ms = jnp.mean(x * x, axis=-1, keepdims=True)
    return x * lax.rsqrt(ms + RMS_EPS) * g


def _params(*sem):
    return pltpu.CompilerParams(dimension_semantics=sem, vmem_limit_bytes=VMEM_LIMIT)


def _in_proj_kernel(x_ref, g_ref, w_ref, o_ref):
    h = _rms(x_ref[...], g_ref[...]).astype(BF16)
    acc = jnp.dot(h, w_ref[...], preferred_element_type=F32)
    scale = 1.0 / math.sqrt(SB_HEAD_DIM)
    o_ref[:, :SB_WIDTH] = (acc[:, :SB_WIDTH] * scale).astype(BF16)
    o_ref[:, SB_WIDTH:] = acc[:, SB_WIDTH:].astype(BF16)


def _in_proj(x2d, g, w):
    n = x2d.shape[0]
    tm = PROJ_BLOCK
    return pl.pallas_call(
        _in_proj_kernel,
        grid=(n // tm,),
        in_specs=[
            pl.BlockSpec((tm, D_MODEL), lambda i: (i, 0)),
            pl.BlockSpec((1, D_MODEL), lambda i: (0, 0)),
            pl.BlockSpec((D_MODEL, IN_PROJ_COLS), lambda i: (0, 0)),
        ],
        out_specs=pl.BlockSpec((tm, IN_PROJ_COLS), lambda i: (i, 0)),
        out_shape=jax.ShapeDtypeStruct((n, IN_PROJ_COLS), BF16),
        compiler_params=_params("arbitrary"),
        name="in_proj",
    )(x2d, g, w)


def _sb_chain(qh, kb, vb, neg_tri, c, causal):
    z = lax.dot_general(qh, kb, (((1,), (1,)), ((), ())), preferred_element_type=F32)
    sp = jnp.maximum(z, 0.0) + jnp.log(1.0 + jnp.exp(-jnp.abs(z)))
    if causal is not None:
        sp = jnp.where(causal, sp, 0.0)
    after = jnp.dot(sp.astype(BF16), neg_tri, preferred_element_type=F32)
    w = jnp.exp((z - sp) + after)
    if causal is not None:
        w = jnp.where(causal, w, 0.0)
    pv = jnp.dot(w.astype(BF16), vb, preferred_element_type=F32) * jnp.exp(c)
    return pv, c + (after[:, :1] - sp[:, :1])


def _sb_kernel(q_ref, k_ref, v_ref, tri_ref, o_ref, acc_ref, c_ref):
    tq = SB_BLOCK
    lane = lax.broadcasted_iota(I32, (tq, LANES), 1)
    neg_tri = tri_ref[...]
    row = lax.broadcasted_iota(I32, (2 * tq, tq), 0)
    col = lax.broadcasted_iota(I32, (2 * tq, tq), 1)
    causal = col < jnp.where(row >= tq, row - tq, row)

    def stacked(u):
        q = q_ref[u * tq:(u + 1) * tq, :]
        zero = jnp.zeros_like(q)
        return jnp.concatenate(
            [jnp.where(lane < SB_HEAD_DIM, q, zero), jnp.where(lane >= SB_HEAD_DIM, q, zero)], axis=0)

    def step(qs, j, c, diagonal):
        off = pl.multiple_of(j * tq, tq)
        return _sb_chain(qs, k_ref[pl.ds(off, tq), :], v_ref[pl.ds(off, tq), :], neg_tri, c,
                         causal if diagonal else None)

    blocks = []
    for u in range(SB_QBLOCKS):
        i = pl.program_id(2) * SB_QBLOCKS + u
        qs = stacked(u)
        pv, c = step(qs, i, jnp.zeros((2 * tq, LANES), F32), True)
        pv_left, c_left = step(qs, jnp.maximum(i - 1, 0), c, False)
        has_left = i > 0
        acc_ref[u] = pv + jnp.where(has_left, pv_left, 0.0)
        c_ref[u] = jnp.where(has_left, c_left, c)
        blocks.append((i, qs))

    for u, (i, qs) in enumerate(blocks):
        def alive():
            return jnp.max(c_ref[u]) > SB_DEAD_LOG

        def cond(carry):
            j, live = carry
            return jnp.logical_and(j >= 0, live)

        def body(carry):
            j, _ = carry
            pv, c = step(qs, j, c_ref[u], False)
            acc_ref[u] += pv
            c_ref[u] = c
            return j - 1, alive()

        lax.while_loop(cond, body, (i - 2, alive()))
        o_ref[u * tq:(u + 1) * tq, :] = jnp.where(
            lane < SB_HEAD_DIM, acc_ref[u, :tq, :], acc_ref[u, tq:, :]).astype(o_ref.dtype)


def _sb_attention(proj3, neg_tri):
    b, s, _ = proj3.shape
    tq = SB_BLOCK * SB_QBLOCKS
    pairs = SB_WIDTH // LANES
    return pl.pallas_call(
        _sb_kernel,
        grid=(b, pairs, s // tq),
        in_specs=[
            pl.BlockSpec((None, tq, LANES), lambda bi, p, i: (bi, i, p)),
            pl.BlockSpec((None, s, LANES), lambda bi, p, i: (bi, 0, pairs + p)),
            pl.BlockSpec((None, s, LANES), lambda bi, p, i: (bi, 0, 2 * pairs + p)),
            pl.BlockSpec((SB_BLOCK, SB_BLOCK), lambda bi, p, i: (0, 0)),
        ],
        out_specs=pl.BlockSpec((None, tq, LANES), lambda bi, p, i: (bi, i, p)),
        out_shape=jax.ShapeDtypeStruct((b, s, SB_WIDTH), BF16),
        scratch_shapes=[pltpu.VMEM((SB_QBLOCKS, 2 * SB_BLOCK, LANES), F32),
                        pltpu.VMEM((SB_QBLOCKS, 2 * SB_BLOCK, LANES), F32)],
        compiler_params=_params("arbitrary", "arbitrary", "arbitrary"),
        name="sb_attn",
    )(proj3, proj3, proj3, neg_tri)


def _mix_kernel(x_ref, sb_ref, u_ref, halo_ref, wp_ref, ps_ref, gsb_ref, gpool_ref, wout_ref,
                o_ref, *, seq):
    tm = x_ref.shape[0]
    t0 = (pl.program_id(0) * tm) % seq
    u = u_ref[...].astype(F32)
    halo = jnp.where(t0 == 0, 0.0, halo_ref[...].astype(F32))
    a = jnp.concatenate([halo, u], axis=0)
    pos = t0 + lax.broadcasted_iota(I32, (tm, POOL_GROUP_DIM), 0)
    outs = []
    for g, win in enumerate(POOL_WINDOWS):
        lo, hi = g * POOL_GROUP_DIM, (g + 1) * POOL_GROUP_DIM
        s = a[:, lo:hi]
        sh = 1
        while sh < win:
            s = s + pltpu.roll(s, sh, axis=0)
            sh *= 2
        cnt = jnp.minimum(pos + 1, win).astype(F32)
        outs.append(s[POOL_HALO:, :] / cnt - u[:, lo:hi])
    pooled = jnp.concatenate(outs, axis=1).astype(BF16)
    pooled = jnp.dot(pooled, wp_ref[...], preferred_element_type=F32) * ps_ref[...]
    pool_n = _rms(pooled, gpool_ref[...]).astype(BF16)
    sb_n = _rms(sb_ref[...].astype(F32), gsb_ref[...]).astype(BF16)
    y = jnp.dot(sb_n, wout_ref[:SB_WIDTH, :], preferred_element_type=F32)
    y = y + jnp.dot(pool_n, wout_ref[SB_WIDTH:, :], preferred_element_type=F32)
    o_ref[...] = x_ref[...] + y


def _mix_out(x2d, sb2d, proj2d, wp, ps, gsb, gpool, wout, seq):
    n = x2d.shape[0]
    tm = PROJ_BLOCK
    ucol = 3 * SB_WIDTH // POOL_WIDTH
    hb = tm // POOL_HALO
    return pl.pallas_call(
        functools.partial(_mix_kernel, seq=seq),
        grid=(n // tm,),
        in_specs=[
            pl.BlockSpec((tm, D_MODEL), lambda i: (i, 0)),
            pl.BlockSpec((tm, SB_WIDTH), lambda i: (i, 0)),
            pl.BlockSpec((tm, POOL_WIDTH), lambda i: (i, ucol)),
            pl.BlockSpec((POOL_HALO, POOL_WIDTH), lambda i: (jnp.maximum(i * hb - 1, 0), ucol)),
            pl.BlockSpec((POOL_WIDTH, POOL_WIDTH), lambda i: (0, 0)),
            pl.BlockSpec((1, POOL_WIDTH), lambda i: (0, 0)),
            pl.BlockSpec((1, SB_WIDTH), lambda i: (0, 0)),
            pl.BlockSpec((1, POOL_WIDTH), lambda i: (0, 0)),
            pl.BlockSpec((D_MODEL, D_MODEL), lambda i: (0, 0)),
        ],
        out_specs=pl.BlockSpec((tm, D_MODEL), lambda i: (i, 0)),
        out_shape=jax.ShapeDtypeStruct((n, D_MODEL), F32),
        compiler_params=_params("arbitrary"),
        name="mix_out",
    )(x2d, sb2d, proj2d, proj2d, wp, ps, gsb, gpool, wout)


def _mem_kv_kernel(m_ref, g_ref, w_ref, o_ref):
    h = _rms(m_ref[...], g_ref[...]).astype(BF16)
    o_ref[...] = jnp.dot(h, w_ref[...], preferred_element_type=F32).astype(BF16)


def _mem_kv(mem, g, w):
    b = mem.shape[0]
    return pl.pallas_call(
        _mem_kv_kernel,
        grid=(b,),
        in_specs=[
            pl.BlockSpec((None, N_MEM, D_MODEL), lambda i: (i, 0, 0)),
            pl.BlockSpec((1, D_MODEL), lambda i: (0, 0)),
            pl.BlockSpec((D_MODEL, 2 * D_MODEL), lambda i: (0, 0)),
        ],
        out_specs=pl.BlockSpec((None, N_MEM, 2 * D_MODEL), lambda i: (i, 0, 0)),
        out_shape=jax.ShapeDtypeStruct((b, N_MEM, 2 * D_MODEL), BF16),
        compiler_params=_params("arbitrary"),
        name="mem_kv",
    )(mem, g, w)


def _xattn_block(x, kv_ref, gq, wq_ref, wo_ref, gf, wr_ref, br, tri):
    tm = x.shape[0]
    hq = _rms(x, gq).astype(BF16)
    q = jnp.dot(hq, wq_ref[...], preferred_element_type=F32) * (1.0 / math.sqrt(MEM_HEAD_DIM))
    q = q.astype(BF16)
    outs = []
    for h in range(MEM_HEADS):
        lo, hi = h * MEM_HEAD_DIM, (h + 1) * MEM_HEAD_DIM
        s = lax.dot_general(q[:, lo:hi], kv_ref[:, lo:hi], (((1,), (1,)), ((), ())),
                            preferred_element_type=F32)
        p = jnp.exp(s - jnp.max(s, axis=-1, keepdims=True))
        denom = jnp.sum(p, axis=-1, keepdims=True)
        o = jnp.dot(p.astype(BF16), kv_ref[:, D_MODEL + lo:D_MODEL + hi], preferred_element_type=F32)
        outs.append(o / denom)
    o = jnp.concatenate(outs, axis=1).astype(BF16)
    x2 = x + jnp.dot(o, wo_ref[...], preferred_element_type=F32)
    hf = _rms(x2, gf)

    h_hi = hf.astype(BF16)
    h_lo = (hf - h_hi.astype(F32)).astype(BF16)
    hw = jnp.dot(h_hi, wr_ref[...], preferred_element_type=F32)
    lw = jnp.dot(h_lo, wr_ref[:, :LANES], preferred_element_type=F32)
    logits = (hw[:, :LANES] + (hw[:, LANES:] + lw)) + br

    lane = lax.broadcasted_iota(I32, (tm, LANES), 1).astype(F32)
    work = logits
    vals, idxs, sels = [], [], []
    for _ in range(TOP_K):
        m = jnp.max(work, axis=-1, keepdims=True)
        idx = jnp.min(jnp.where(work == m, lane, float(LANES)), axis=-1, keepdims=True)
        sel = lane == idx
        vals.append(m)
        idxs.append(idx)
        sels.append(sel)
        work = jnp.where(sel, -3e38, work)
    exps = [jnp.exp(v - vals[0]) for v in vals]
    den = exps[0] + exps[1] + exps[2] + exps[3]

    onehot = jnp.zeros((tm, LANES), F32)
    for sel in sels:
        onehot = onehot + sel.astype(F32)
    blocks = [onehot[r:r + TOK_BLOCK, :] for r in range(0, tm, TOK_BLOCK)]
    before = jnp.concatenate(
        [jnp.dot(tri, blk.astype(BF16), preferred_element_type=F32) for blk in blocks], axis=0)
    meta = jnp.zeros((tm, LANES), F32)
    gates = jnp.zeros((tm, LANES), F32)
    for k in range(TOP_K):
        rank = jnp.sum(jnp.where(sels[k], before, 0.0), axis=-1, keepdims=True)
        meta = jnp.where(lane == float(k), idxs[k], meta)
        meta = jnp.where(lane == float(TOP_K + k), rank, meta)
        gates = jnp.where(lane == float(k), exps[k] / den, gates)
    counts = [jnp.sum(blk, axis=0, keepdims=True).astype(I32) for blk in blocks]
    return x2, hf.astype(BF16), meta.astype(I32), gates, counts


def _xattn_kernel(x_ref, kv_ref, gq_ref, wq_ref, wo_ref, gf_ref, wr_ref, br_ref, tri_ref,
                  x2_ref, h_ref, meta_ref, gate_ref, cnt_ref):
    tb = TOK_BLOCK * XATTN_CHAIN
    for chain in range(XATTN_BLOCKS // XATTN_CHAIN):
        rows = slice(chain * tb, (chain + 1) * tb)
        x2, hf, meta, gates, counts = _xattn_block(
            x_ref[rows, :], kv_ref, gq_ref[...], wq_ref, wo_ref, gf_ref[...], wr_ref, br_ref[...],
            tri_ref[...])
        x2_ref[rows, :] = x2
        h_ref[rows, :] = hf
        meta_ref[rows, :] = meta
        gate_ref[rows, :] = gates
        for r, c in enumerate(counts):
            cnt_ref[chain * XATTN_CHAIN + r] = c


def _xattn_router(x1, kv, gq, wq, wo, gf, wr, br, tri, seq):
    n = x1.shape[0]
    tm = TOK_BLOCK * XATTN_BLOCKS
    per_seq = seq // tm
    const = lambda i: (0, 0)
    return pl.pallas_call(
        _xattn_kernel,
        grid=(n // tm,),
        in_specs=[
            pl.BlockSpec((tm, D_MODEL), lambda i: (i, 0)),
            pl.BlockSpec((None, N_MEM, 2 * D_MODEL), lambda i: (i // per_seq, 0, 0)),
            pl.BlockSpec((1, D_MODEL), const),
            pl.BlockSpec((D_MODEL, D_MODEL), const),
            pl.BlockSpec((D_MODEL, D_MODEL), const),
            pl.BlockSpec((1, D_MODEL), const),
            pl.BlockSpec((D_MODEL, 2 * LANES), const),
            pl.BlockSpec((1, LANES), const),
            pl.BlockSpec((TOK_BLOCK, TOK_BLOCK), const),
        ],
        out_specs=[
            pl.BlockSpec((tm, D_MODEL), lambda i: (i, 0)),
            pl.BlockSpec((tm, D_MODEL), lambda i: (i, 0)),
            pl.BlockSpec((tm, LANES), lambda i: (i, 0)),
            pl.BlockSpec((tm, LANES), lambda i: (i, 0)),
            pl.BlockSpec((XATTN_BLOCKS, 1, LANES), lambda i: (i, 0, 0)),
        ],
        out_shape=[
            jax.ShapeDtypeStruct((n, D_MODEL), F32),
            jax.ShapeDtypeStruct((n, D_MODEL), BF16),
            jax.ShapeDtypeStruct((n, LANES), I32),
            jax.ShapeDtypeStruct((n, LANES), F32),
            jax.ShapeDtypeStruct((n // TOK_BLOCK, 1, LANES), I32),
        ],
        compiler_params=_params("arbitrary"),
        name="xattn_router",
    )(x1, kv, gq, wq, wo, gf, wr, br, tri)


def _segment_copy(tables, step, e, vmem_ref, hbm_ref, sem, to_hbm):
    src_ref, dst_ref, cnt_ref = tables
    at = step * N_EXPERTS + e
    n = pl.multiple_of(cnt_ref[at], SUBLANES)
    local = vmem_ref.at[pl.ds(pl.multiple_of(src_ref[at], SUBLANES), n), :]
    remote = hbm_ref.at[pl.ds(pl.multiple_of(dst_ref[at], SUBLANES), n), :]
    return pltpu.make_async_copy(local, remote, sem) if to_hbm else pltpu.make_async_copy(remote, local, sem)


def _each_segment(fn):
    for r in range(SORT_BLOCKS):
        for e in range(N_EXPERTS):
            fn(r, e)


def _dispatch_kernel(src_ref, dst_ref, cnt_ref, fill_ref, h_ref, pos_ref, xs_ref, stage_ref, sem,
                     fill_sem):
    tables = (src_ref, dst_ref, cnt_ref)
    step = pl.program_id(0)
    last = pl.num_programs(0) - 1
    parity = step % 2
    where = lax.broadcasted_iota(I32, (TOK_BLOCK, SORT_ROWS), 1)
    for r in range(SORT_BLOCKS):
        rows = slice(r * TOK_BLOCK, (r + 1) * TOK_BLOCK)
        pos = pos_ref[rows, :]
        hit = where == pos[:, 0:1]
        for k in range(1, TOP_K):
            hit = jnp.logical_or(hit, where == pos[:, k:k + 1])
        onehot = jnp.where(hit, 1.0, 0.0).astype(BF16)
        stage_ref[parity * SORT_BLOCKS + r] = lax.dot_general(
            onehot, h_ref[rows, :], (((0,), (0,)), ((), ())), preferred_element_type=F32)

    def copy(at_step, at_parity, r, e):
        return _segment_copy(tables, at_step * SORT_BLOCKS + r, e,
                             stage_ref.at[at_parity * SORT_BLOCKS + r], xs_ref, sem.at[at_parity], True)

    _each_segment(lambda r, e: copy(step, parity, r, e).start())

    @pl.when(step > 0)
    def _():
        _each_segment(lambda r, e: copy(step - 1, 1 - parity, r, e).wait())

    @pl.when(step == last)
    def _():
        _each_segment(lambda r, e: copy(step, parity, r, e).wait())

    @pl.when(step == 0)
    def _():
        stage_ref[SORT_BLOCKS] = jnp.zeros((SORT_ROWS, D_MODEL), F32)
        zeros = stage_ref.at[SORT_BLOCKS]

        def pad_copy(e):
            n = pl.multiple_of(fill_ref[N_EXPERTS + e], SUBLANES)
            at = pl.multiple_of(fill_ref[e], SUBLANES)
            return pltpu.make_async_copy(zeros.at[pl.ds(0, n), :], xs_ref.at[pl.ds(at, n), :], fill_sem)

        def tail_copy(t):
            at = pl.multiple_of(fill_ref[2 * N_EXPERTS] + t * ROW_BLOCK, ROW_BLOCK)
            return pltpu.make_async_copy(zeros.at[pl.ds(0, ROW_BLOCK), :],
                                         xs_ref.at[pl.ds(at, ROW_BLOCK), :], fill_sem)

        def each_pad(fn):
            def body(e, carry):
                @pl.when(fill_ref[N_EXPERTS + e] > 0)
                def _():
                    fn(e)
                return carry
            lax.fori_loop(0, N_EXPERTS, body, 0)

        def each_tail(fn):
            def body(t, carry):
                fn(t)
                return carry
            lax.fori_loop(0, fill_ref[2 * N_EXPERTS + 1], body, 0)

        each_pad(lambda e: pad_copy(e).start())
        each_tail(lambda t: tail_copy(t).start())
        each_pad(lambda e: pad_copy(e).wait())
        each_tail(lambda t: tail_copy(t).wait())


def _dispatch(tables, fill, hf, pos, n_rows):
    n = hf.shape[0]
    tm = TOK_BLOCK * SORT_BLOCKS
    grid_spec = pltpu.PrefetchScalarGridSpec(
        num_scalar_prefetch=4,
        grid=(n // tm,),
        in_specs=[
            pl.BlockSpec((tm, D_MODEL), lambda i, *_: (i, 0)),
            pl.BlockSpec((tm, LANES), lambda i, *_: (i, 0)),
        ],
        out_specs=pl.BlockSpec(memory_space=pl.ANY),
        scratch_shapes=[pltpu.VMEM((2 * SORT_BLOCKS, SORT_ROWS, D_MODEL), F32),
                        pltpu.SemaphoreType.DMA((2,)), pltpu.SemaphoreType.DMA],
    )
    return pl.pallas_call(
        _dispatch_kernel,
        grid_spec=grid_spec,
        out_shape=jax.ShapeDtypeStruct((n_rows, D_MODEL), F32),
        compiler_params=_params("arbitrary"),
        name="dispatch",
    )(*tables, fill, hf, pos)


def _expert_kernel(blk_e_ref, next_e_ref, n_used_ref, xs_ref, w1_hbm, b1_ref, w2_hbm, b2_ref,
                   ys_ref, w1f_ref, w2f_ref, w1b_ref, w2b_ref, sem):
    i = pl.program_id(0)
    used = i < n_used_ref[0]

    def fetch(e):
        return (pltpu.make_async_copy(w1_hbm.at[e], w1f_ref, sem.at[0]),
                pltpu.make_async_copy(w2_hbm.at[e], w2f_ref, sem.at[1]))

    @pl.when(jnp.logical_not(used))
    def _():
        ys_ref[...] = jnp.zeros_like(ys_ref)

    @pl.when(used)
    def _():
        e = blk_e_ref[i]

        @pl.when(i == 0)
        def _():
            for copy in fetch(e):
                copy.start()

        @pl.when(jnp.logical_or(i == 0, e != blk_e_ref[jnp.maximum(i - 1, 0)]))
        def _():
            for copy in fetch(e):
                copy.wait()
            w1b_ref[...] = w1f_ref[...].astype(BF16)
            w2b_ref[...] = w2f_ref[...].astype(BF16)

            @pl.when(next_e_ref[i] >= 0)
            def _():
                for copy in fetch(next_e_ref[i]):
                    copy.start()

        gu = jnp.dot(xs_ref[...].astype(BF16), w1b_ref[...], preferred_element_type=F32) + b1_ref[...]
        gate = jnp.minimum(gu[:, :D_FF], SWIGLU_LIMIT)
        up = jnp.clip(gu[:, D_FF:], -SWIGLU_LIMIT, SWIGLU_LIMIT)
        act = (up + 1.0) * gate * jax.nn.sigmoid(SWIGLU_ALPHA * gate)
        ys_ref[...] = jnp.dot(act.astype(BF16), w2b_ref[...], preferred_element_type=F32) + b2_ref[...]


def _experts(blk_e, next_e, n_used, xs, w1, b1, w2, b2):
    n_rows = xs.shape[0]
    bm = ROW_BLOCK
    last_used = lambda i, nu: jnp.maximum(jnp.minimum(i, nu[0] - 1), 0)
    row = lambda i, be, ne, nu: (last_used(i, nu), 0)
    exp3 = lambda i, be, ne, nu: (be[last_used(i, nu)], 0, 0)
    grid_spec = pltpu.PrefetchScalarGridSpec(
        num_scalar_prefetch=3,
        grid=(n_rows // bm,),
        in_specs=[
            pl.BlockSpec((bm, D_MODEL), row),
            pl.BlockSpec(memory_space=pl.ANY),
            pl.BlockSpec((None, 1, 2 * D_FF), exp3),
            pl.BlockSpec(memory_space=pl.ANY),
            pl.BlockSpec((None, 1, D_MODEL), exp3),
        ],
        out_specs=pl.BlockSpec((bm, D_MODEL), lambda i, be, ne, nu: (i, 0)),
        scratch_shapes=[pltpu.VMEM((D_MODEL, 2 * D_FF), F32), pltpu.VMEM((D_FF, D_MODEL), F32),
                        pltpu.VMEM((D_MODEL, 2 * D_FF), BF16), pltpu.VMEM((D_FF, D_MODEL), BF16),
                        pltpu.SemaphoreType.DMA((2,))],
    )
    return pl.pallas_call(
        _expert_kernel,
        grid_spec=grid_spec,
        out_shape=jax.ShapeDtypeStruct((n_rows, D_MODEL), F32),
        compiler_params=_params("arbitrary"),
        name="experts",
    )(blk_e, next_e, n_used, xs, w1, b1, w2, b2)


def _combine_kernel(src_ref, dst_ref, cnt_ref, x_ref, gate_ref, pos_ref, g_ref, ys_ref, o_ref,
                    stage_ref, sem):
    tables = (src_ref, dst_ref, cnt_ref)
    step = pl.program_id(0)
    last = pl.num_programs(0) - 1
    parity = step % 2

    def copy(at_step, at_parity, r, e):
        return _segment_copy(tables, at_step * SORT_BLOCKS + r, e,
                             stage_ref.at[at_parity * SORT_BLOCKS + r], ys_ref, sem.at[at_parity], False)

    @pl.when(step == 0)
    def _():
        stage_ref[...] = jnp.zeros_like(stage_ref)
        _each_segment(lambda r, e: copy(step, parity, r, e).start())

    @pl.when(step < last)
    def _():
        _each_segment(lambda r, e: copy(step + 1, 1 - parity, r, e).start())

    _each_segment(lambda r, e: copy(step, parity, r, e).wait())

    where = lax.broadcasted_iota(I32, (TOK_BLOCK, SORT_ROWS), 1)
    for r in range(SORT_BLOCKS):
        rows = slice(r * TOK_BLOCK, (r + 1) * TOK_BLOCK)
        pos = pos_ref[rows, :]
        gates = gate_ref[rows, :]
        weights = jnp.zeros((TOK_BLOCK, SORT_ROWS), F32)
        for k in range(TOP_K):
            weights = jnp.where(where == pos[:, k:k + 1], gates[:, k:k + 1], weights)
        y = x_ref[rows, :] + jnp.dot(weights.astype(BF16),
                                     stage_ref[parity * SORT_BLOCKS + r].astype(BF16),
                                     preferred_element_type=F32)
        o_ref[rows, :] = _rms(y, g_ref[...])


def _combine(tables, x2, gates, pos, g_final, ys):
    n = x2.shape[0]
    tc = TOK_BLOCK * SORT_BLOCKS
    grid_spec = pltpu.PrefetchScalarGridSpec(
        num_scalar_prefetch=3,
        grid=(n // tc,),
        in_specs=[
            pl.BlockSpec((tc, D_MODEL), lambda i, *_: (i, 0)),
            pl.BlockSpec((tc, LANES), lambda i, *_: (i, 0)),
            pl.BlockSpec((tc, LANES), lambda i, *_: (i, 0)),
            pl.BlockSpec((1, D_MODEL), lambda i, *_: (0, 0)),
            pl.BlockSpec(memory_space=pl.ANY),
        ],
        out_specs=pl.BlockSpec((tc, D_MODEL), lambda i, *_: (i, 0)),
        scratch_shapes=[pltpu.VMEM((2 * SORT_BLOCKS, SORT_ROWS, D_MODEL), F32),
                        pltpu.SemaphoreType.DMA((2,))],
    )
    return pl.pallas_call(
        _combine_kernel,
        grid_spec=grid_spec,
        out_shape=jax.ShapeDtypeStruct((n, D_MODEL), F32),
        compiler_params=_params("arbitrary"),
        name="combine",
    )(*tables, x2, gates, pos, g_final, ys)


def _strict_lower(n):
    r = lax.broadcasted_iota(I32, (n, n), 0)
    c = lax.broadcasted_iota(I32, (n, n), 1)
    return (r > c).astype(BF16)


def _round_up(v, m):
    return (v + m - 1) // m * m


def _routing_tables(meta, cnt, n):
    bm = ROW_BLOCK
    n_tok_blk = n // TOK_BLOCK
    n_rows = _round_up(n * TOP_K + n_tok_blk * N_EXPERTS * SUBLANES, bm) + N_EXPERTS * bm
    n_blk = n_rows // bm
    seg = _round_up(jnp.maximum(cnt[:, 0, :N_EXPERTS], 1), SUBLANES)
    local_start = jnp.cumsum(seg, axis=1) - seg
    total = jnp.sum(seg, axis=0)
    padded = _round_up(total, bm)
    pend = jnp.cumsum(padded)
    pstart = pend - padded
    global_start = pstart[None, :] + jnp.cumsum(seg, axis=0) - seg
    tables = (local_start.reshape(-1), global_start.reshape(-1), seg.reshape(-1))

    e_idx = meta[:, :TOP_K]
    rank = meta[:, TOP_K:2 * TOP_K]
    tok_start = jnp.repeat(local_start, TOK_BLOCK, axis=0)
    onehot = e_idx[:, :, None] == jnp.arange(N_EXPERTS, dtype=I32)[None, None, :]
    pos = jnp.sum(jnp.where(onehot, tok_start[:, None, :], 0), axis=-1) + rank
    pos = jnp.pad(pos, ((0, 0), (0, LANES - TOP_K)), constant_values=-1)

    n_used = (pend[-1:] // bm).astype(I32)
    blk_row = jnp.arange(n_blk, dtype=I32) * bm
    blk_e = jnp.minimum(jnp.sum(blk_row[:, None] >= pend[None, :], axis=1), N_EXPERTS - 1).astype(I32)
    ids = jnp.arange(N_EXPERTS, dtype=I32)
    later = jnp.where((ids[None, :] > ids[:, None]) & (padded[None, :] > 0), ids[None, :], N_EXPERTS)
    follow = jnp.min(later, axis=1)
    next_e = jnp.where(follow < N_EXPERTS, follow, -1)[blk_e].astype(I32)
    fill = jnp.concatenate([pstart + total, padded - total, pend[-1:], n_blk - pend[-1:] // bm])
    return tables, fill.astype(I32), pos.astype(I32), blk_e, next_e, n_used, n_rows


def _layer(x2d, mem, b, s, g_mix, w_in, g_sb_out, g_pool_out, w_pool, pool_scale, w_out,
           g_mem_q, g_mem_kv, w_mem_q, w_mem_kv, w_mem_o, g_ffn, w_router, b_router,
           w_e_in, b_e_in, w_e_out, b_e_out, g_out):
    n = b * s
    row = lambda v: v.reshape(1, -1).astype(F32)

    proj = _in_proj(x2d, row(g_mix), w_in.astype(BF16))
    sb = _sb_attention(proj.reshape(b, s, IN_PROJ_COLS), -_strict_lower(SB_BLOCK))
    wp = jnp.zeros((POOL_WIDTH, POOL_WIDTH), BF16)
    for g in range(len(POOL_WINDOWS)):
        at = g * POOL_GROUP_DIM
        wp = wp.at[at:at + POOL_GROUP_DIM, at:at + POOL_GROUP_DIM].set(w_pool[g].astype(BF16))
    x1 = _mix_out(x2d, sb.reshape(n, SB_WIDTH), proj, wp, row(pool_scale),
                  row(g_sb_out), row(g_pool_out), w_out.astype(BF16), s)

    kv = _mem_kv(mem, row(g_mem_kv), w_mem_kv.astype(BF16))
    wr = jnp.zeros((D_MODEL, LANES), F32).at[:, :N_EXPERTS].set(w_router)
    wr_hi = wr.astype(BF16)
    wr_lo = (wr - wr_hi.astype(F32)).astype(BF16)
    br = jnp.full((1, LANES), NEG_BIG, F32).at[0, :N_EXPERTS].set(b_router)
    x2, hf, meta, gates, cnt = _xattn_router(
        x1, kv, row(g_mem_q), w_mem_q.astype(BF16), w_mem_o.astype(BF16), row(g_ffn),
        jnp.concatenate([wr_hi, wr_lo], axis=1), br, _strict_lower(TOK_BLOCK), s)

    tables, fill, pos, blk_e, next_e, n_used, n_rows = _routing_tables(meta, cnt, n)
    xs = _dispatch(tables, fill, hf, pos, n_rows)
    ys = _experts(blk_e, next_e, n_used, xs, w_e_in, b_e_in.reshape(N_EXPERTS, 1, -1),
                  w_e_out, b_e_out.reshape(N_EXPERTS, 1, -1))
    return _combine(tables, x2, gates, pos, row(g_out), ys)


def kernel(x, mem, g_mix, w_in, g_sb_out, g_pool_out, w_pool, pool_scale, w_out, g_mem_q, g_mem_kv,
           w_mem_q, w_mem_kv, w_mem_o, g_ffn, w_router, b_router, w_expert_in, b_expert_in,
           w_expert_out, b_expert_out, g_final):
    b, s, d = x.shape
    depth = g_mix.shape[0]
    assert d == D_MODEL and depth == 1, "the final RMSNorm is fused into the single layer's combine"
    assert s % PROJ_BLOCK == 0 and s % (SB_BLOCK * SB_QBLOCKS) == 0 and s % (TOK_BLOCK * XATTN_BLOCKS) == 0
    assert (b * s) % (TOK_BLOCK * SORT_BLOCKS) == 0
    out = _layer(x.reshape(b * s, d), mem, b, s, g_mix[0], w_in[0], g_sb_out[0], g_pool_out[0],
                 w_pool[0], pool_scale[0], w_out[0], g_mem_q[0], g_mem_kv[0], w_mem_q[0],
                 w_mem_kv[0], w_mem_o[0], g_ffn[0], w_router[0], b_router[0], w_expert_in[0],
                 b_expert_in[0], w_expert_out[0], b_expert_out[0], g_final)
    return out.reshape(b, s, d)
```

```python
import functools
import math

import jax
import jax.numpy as jnp
from jax import lax
from jax.experimental import pallas as pl
from jax.experimental.pallas import tpu as pltpu

F32 = jnp.float32
BF16 = jnp.bfloat16
I32 = jnp.int32

D_MODEL = 1024
SB_HEADS = 8
SB_HEAD_DIM = 64
SB_WIDTH = SB_HEADS * SB_HEAD_DIM
POOL_WINDOWS = (2, 4, 8, 16)
POOL_GROUP_DIM = 128
POOL_WIDTH = len(POOL_WINDOWS) * POOL_GROUP_DIM
IN_PROJ_COLS = 3 * SB_WIDTH + POOL_WIDTH
N_MEM = 256
MEM_HEADS = 4
MEM_HEAD_DIM = D_MODEL // MEM_HEADS
N_EXPERTS = 32
TOP_K = 4
D_FF = D_MODEL
SWIGLU_LIMIT = 7.0
SWIGLU_ALPHA = 1.702
RMS_EPS = 1e-5

LANES = 128
SUBLANES = 8
POOL_HALO = 16
SB_BLOCK = 256
SB_QBLOCKS = 4
SB_DEAD_LOG = -110.0
ROW_BLOCK = 512
TOK_BLOCK = 256
XATTN_BLOCKS = 4
XATTN_CHAIN = 2
SORT_BLOCKS = 2
PROJ_BLOCK = 1024
SORT_ROWS = -(-(TOK_BLOCK * TOP_K + N_EXPERTS * SUBLANES) // LANES) * LANES
VMEM_LIMIT = 48 * 1024 * 1024
NEG_BIG = -1e30


def _rms(x, g):
    ms = jnp.mean(x * x, axis=-1, keepdims=True)
    return x * lax.rsqrt(ms + RMS_EPS) * g


def _segment_rows(count):
    return jnp.ceil(jnp.maximum(count, 1) / SUBLANES) * SUBLANES


def _params(*sem):
    return pltpu.CompilerParams(dimension_semantics=sem, vmem_limit_bytes=VMEM_LIMIT)


def _in_proj_kernel(x_ref, g_ref, w_ref, o_ref):
    h = _rms(x_ref[...], g_ref[...]).astype(BF16)
    acc = jnp.dot(h, w_ref[...], preferred_element_type=F32)
    scale = 1.0 / math.sqrt(SB_HEAD_DIM)
    o_ref[:, :SB_WIDTH] = (acc[:, :SB_WIDTH] * scale).astype(BF16)
    o_ref[:, SB_WIDTH:] = acc[:, SB_WIDTH:].astype(BF16)


def _in_proj(x2d, g, w):
    n = x2d.shape[0]
    tm = PROJ_BLOCK
    return pl.pallas_call(
        _in_proj_kernel,
        grid=(n // tm,),
        in_specs=[
            pl.BlockSpec((tm, D_MODEL), lambda i: (i, 0)),
            pl.BlockSpec((1, D_MODEL), lambda i: (0, 0)),
            pl.BlockSpec((D_MODEL, IN_PROJ_COLS), lambda i: (0, 0)),
        ],
        out_specs=pl.BlockSpec((tm, IN_PROJ_COLS), lambda i: (i, 0)),
        out_shape=jax.ShapeDtypeStruct((n, IN_PROJ_COLS), BF16),
        compiler_params=_params("arbitrary"),
        name="in_proj",
    )(x2d, g, w)


def _sb_chain(qh, kb, vb, neg_tri, c, causal):
    z = lax.dot_general(qh, kb, (((1,), (1,)), ((), ())), preferred_element_type=F32)
    sp = jnp.maximum(z, 0.0) + jnp.log(1.0 + jnp.exp(-jnp.abs(z)))
    if causal is not None:
        sp = jnp.where(causal, sp, 0.0)
    after = jnp.dot(sp.astype(BF16), neg_tri, preferred_element_type=F32)
    w = jnp.exp((z - sp) + after)
    if causal is not None:
        w = jnp.where(causal, w, 0.0)
    pv = jnp.dot(w.astype(BF16), vb, preferred_element_type=F32) * jnp.exp(c)
    return pv, c + (after[:, :1] - sp[:, :1])


def _sb_kernel(q_ref, k_ref, v_ref, tri_ref, o_ref, acc_ref, c_ref):
    tq = SB_BLOCK
    lane = lax.broadcasted_iota(I32, (tq, LANES), 1)
    neg_tri = tri_ref[...]
    row = lax.broadcasted_iota(I32, (2 * tq, tq), 0)
    col = lax.broadcasted_iota(I32, (2 * tq, tq), 1)
    causal = col < jnp.where(row >= tq, row - tq, row)

    def stacked(u):
        q = q_ref[u * tq:(u + 1) * tq, :]
        zero = jnp.zeros_like(q)
        return jnp.concatenate(
            [jnp.where(lane < SB_HEAD_DIM, q, zero), jnp.where(lane >= SB_HEAD_DIM, q, zero)], axis=0)

    def step(qs, j, c, diagonal):
        off = pl.multiple_of(j * tq, tq)
        return _sb_chain(qs, k_ref[pl.ds(off, tq), :], v_ref[pl.ds(off, tq), :], neg_tri, c,
                         causal if diagonal else None)

    blocks = []
    for u in range(SB_QBLOCKS):
        i = pl.program_id(2) * SB_QBLOCKS + u
        qs = stacked(u)
        pv, c = step(qs, i, jnp.zeros((2 * tq, LANES), F32), True)
        pv_left, c_left = step(qs, jnp.maximum(i - 1, 0), c, False)
        has_left = i > 0
        acc_ref[u] = pv + jnp.where(has_left, pv_left, 0.0)
        c_ref[u] = jnp.where(has_left, c_left, c)
        blocks.append((i, qs))

    for u, (i, qs) in enumerate(blocks):
        def alive():
            return jnp.max(c_ref[u]) > SB_DEAD_LOG

        def cond(carry):
            j, live = carry
            return jnp.logical_and(j >= 0, live)

        def body(carry):
            j, _ = carry
            pv, c = step(qs, j, c_ref[u], False)
            acc_ref[u] += pv
            c_ref[u] = c
            return j - 1, alive()

        lax.while_loop(cond, body, (i - 2, alive()))
        o_ref[u * tq:(u + 1) * tq, :] = jnp.where(
            lane < SB_HEAD_DIM, acc_ref[u, :tq, :], acc_ref[u, tq:, :]).astype(o_ref.dtype)


def _sb_attention(proj3, neg_tri):
    b, s, _ = proj3.shape
    tq = SB_BLOCK * SB_QBLOCKS
    pairs = SB_WIDTH // LANES
    return pl.pallas_call(
        _sb_kernel,
        grid=(b, pairs, s // tq),
        in_specs=[
            pl.BlockSpec((None, tq, LANES), lambda bi, p, i: (bi, i, p)),
            pl.BlockSpec((None, s, LANES), lambda bi, p, i: (bi, 0, pairs + p)),
            pl.BlockSpec((None, s, LANES), lambda bi, p, i: (bi, 0, 2 * pairs + p)),
            pl.BlockSpec((SB_BLOCK, SB_BLOCK), lambda bi, p, i: (0, 0)),
        ],
        out_specs=pl.BlockSpec((None, tq, LANES), lambda bi, p, i: (bi, i, p)),
        out_shape=jax.ShapeDtypeStruct((b, s, SB_WIDTH), BF16),
        scratch_shapes=[pltpu.VMEM((SB_QBLOCKS, 2 * SB_BLOCK, LANES), F32),
                        pltpu.VMEM((SB_QBLOCKS, 2 * SB_BLOCK, LANES), F32)],
        compiler_params=_params("arbitrary", "arbitrary", "arbitrary"),
        name="sb_attn",
    )(proj3, proj3, proj3, neg_tri)


def _mix_kernel(x_ref, sb_ref, u_ref, halo_ref, wp_ref, ps_ref, gsb_ref, gpool_ref, wout_ref,
                o_ref, *, seq):
    tm = x_ref.shape[0]
    t0 = (pl.program_id(0) * tm) % seq
    u = u_ref[...].astype(F32)
    halo = jnp.where(t0 == 0, 0.0, halo_ref[...].astype(F32))
    a = jnp.concatenate([halo, u], axis=0)
    pos = t0 + lax.broadcasted_iota(I32, (tm, POOL_GROUP_DIM), 0)
    outs = []
    for g, win in enumerate(POOL_WINDOWS):
        lo, hi = g * POOL_GROUP_DIM, (g + 1) * POOL_GROUP_DIM
        s = a[:, lo:hi]
        sh = 1
        while sh < win:
            s = s + pltpu.roll(s, sh, axis=0)
            sh *= 2
        cnt = jnp.minimum(pos + 1, win).astype(F32)
        outs.append(s[POOL_HALO:, :] / cnt - u[:, lo:hi])
    pooled = jnp.concatenate(outs, axis=1).astype(BF16)
    pooled = jnp.dot(pooled, wp_ref[...], preferred_element_type=F32) * ps_ref[...]
    pool_n = _rms(pooled, gpool_ref[...]).astype(BF16)
    sb_n = _rms(sb_ref[...].astype(F32), gsb_ref[...]).astype(BF16)
    y = jnp.dot(sb_n, wout_ref[:SB_WIDTH, :], preferred_element_type=F32)
    y = y + jnp.dot(pool_n, wout_ref[SB_WIDTH:, :], preferred_element_type=F32)
    o_ref[...] = x_ref[...] + y


def _mix_out(x2d, sb2d, proj2d, wp, ps, gsb, gpool, wout, seq):
    n = x2d.shape[0]
    tm = PROJ_BLOCK
    ucol = 3 * SB_WIDTH // POOL_WIDTH
    hb = tm // POOL_HALO
    return pl.pallas_call(
        functools.partial(_mix_kernel, seq=seq),
        grid=(n // tm,),
        in_specs=[
            pl.BlockSpec((tm, D_MODEL), lambda i: (i, 0)),
            pl.BlockSpec((tm, SB_WIDTH), lambda i: (i, 0)),
            pl.BlockSpec((tm, POOL_WIDTH), lambda i: (i, ucol)),
            pl.BlockSpec((POOL_HALO, POOL_WIDTH), lambda i: (jnp.maximum(i * hb - 1, 0), ucol)),
            pl.BlockSpec((POOL_WIDTH, POOL_WIDTH), lambda i: (0, 0)),
            pl.BlockSpec((1, POOL_WIDTH), lambda i: (0, 0)),
            pl.BlockSpec((1, SB_WIDTH), lambda i: (0, 0)),
            pl.BlockSpec((1, POOL_WIDTH), lambda i: (0, 0)),
            pl.BlockSpec((D_MODEL, D_MODEL), lambda i: (0, 0)),
        ],
        out_specs=pl.BlockSpec((tm, D_MODEL), lambda i: (i, 0)),
        out_shape=jax.ShapeDtypeStruct((n, D_MODEL), F32),
        compiler_params=_params("arbitrary"),
        name="mix_out",
    )(x2d, sb2d, proj2d, proj2d, wp, ps, gsb, gpool, wout)


def _mem_kv_kernel(m_ref, g_ref, w_ref, o_ref):
    h = _rms(m_ref[...], g_ref[...]).astype(BF16)
    o_ref[...] = jnp.dot(h, w_ref[...], preferred_element_type=F32).astype(BF16)


def _mem_kv(mem, g, w):
    b = mem.shape[0]
    return pl.pallas_call(
        _mem_kv_kernel,
        grid=(b,),
        in_specs=[
            pl.BlockSpec((None, N_MEM, D_MODEL), lambda i: (i, 0, 0)),
            pl.BlockSpec((1, D_MODEL), lambda i: (0, 0)),
            pl.BlockSpec((D_MODEL, 2 * D_MODEL), lambda i: (0, 0)),
        ],
        out_specs=pl.BlockSpec((None, N_MEM, 2 * D_MODEL), lambda i: (i, 0, 0)),
        out_shape=jax.ShapeDtypeStruct((b, N_MEM, 2 * D_MODEL), BF16),
        compiler_params=_params("arbitrary"),
        name="mem_kv",
    )(mem, g, w)


def _xattn_block(x, kv_ref, gq, wq_ref, wo_ref, gf, wr_ref, br, tri):
    tm = x.shape[0]
    hq = _rms(x, gq).astype(BF16)
    q = jnp.dot(hq, wq_ref[...], preferred_element_type=F32) * (1.0 / math.sqrt(MEM_HEAD_DIM))
    q = q.astype(BF16)
    outs = []
    for h in range(MEM_HEADS):
        lo, hi = h * MEM_HEAD_DIM, (h + 1) * MEM_HEAD_DIM
        s = lax.dot_general(q[:, lo:hi], kv_ref[:, lo:hi], (((1,), (1,)), ((), ())),
                            preferred_element_type=F32)
        p = jnp.exp(s - jnp.max(s, axis=-1, keepdims=True))
        denom = jnp.sum(p, axis=-1, keepdims=True)
        o = jnp.dot(p.astype(BF16), kv_ref[:, D_MODEL + lo:D_MODEL + hi], preferred_element_type=F32)
        outs.append(o / denom)
    o = jnp.concatenate(outs, axis=1).astype(BF16)
    x2 = x + jnp.dot(o, wo_ref[...], preferred_element_type=F32)
    hf = _rms(x2, gf)

    h_hi = hf.astype(BF16)
    h_lo = (hf - h_hi.astype(F32)).astype(BF16)
    hw = jnp.dot(h_hi, wr_ref[...], preferred_element_type=F32)
    lw = jnp.dot(h_lo, wr_ref[:, :LANES], preferred_element_type=F32)
    logits = (hw[:, :LANES] + (hw[:, LANES:] + lw)) + br

    lane = lax.broadcasted_iota(I32, (tm, LANES), 1).astype(F32)
    work = logits
    vals, idxs, sels = [], [], []
    for _ in range(TOP_K):
        m = jnp.max(work, axis=-1, keepdims=True)
        idx = jnp.min(jnp.where(work == m, lane, float(LANES)), axis=-1, keepdims=True)
        sel = lane == idx
        vals.append(m)
        idxs.append(idx)
        sels.append(sel)
        work = jnp.where(sel, -3e38, work)
    exps = [jnp.exp(v - vals[0]) for v in vals]
    den = exps[0] + exps[1] + exps[2] + exps[3]

    onehot = jnp.zeros((tm, LANES), F32)
    for sel in sels:
        onehot = onehot + sel.astype(F32)
    blocks = [onehot[r:r + TOK_BLOCK, :] for r in range(0, tm, TOK_BLOCK)]
    before = jnp.concatenate(
        [jnp.dot(tri, blk.astype(BF16), preferred_element_type=F32) for blk in blocks], axis=0)
    counts = [jnp.sum(blk, axis=0, keepdims=True) for blk in blocks]
    upper = (lax.broadcasted_iota(I32, (LANES, LANES), 0)
             < lax.broadcasted_iota(I32, (LANES, LANES), 1)).astype(BF16)
    starts = []
    for cnt in counts:
        seg = jnp.where(lane[:SUBLANES, :] < float(N_EXPERTS), _segment_rows(cnt), 0.0)
        seg = jnp.broadcast_to(seg, (SUBLANES, LANES)).astype(BF16)
        start = jnp.dot(seg, upper, preferred_element_type=F32)[:1, :]
        starts.append(jnp.broadcast_to(start, (TOK_BLOCK, LANES)))
    first_row = before + jnp.concatenate(starts, axis=0)
    meta = jnp.zeros((tm, LANES), F32)
    gates = jnp.zeros((tm, LANES), F32)
    for k in range(TOP_K):
        pos = jnp.sum(jnp.where(sels[k], first_row, 0.0), axis=-1, keepdims=True)
        meta = jnp.where(lane == float(k), idxs[k], meta)
        meta = jnp.where(lane == float(TOP_K + k), pos, meta)
        gates = jnp.where(lane == float(k), exps[k] / den, gates)
    return x2, hf.astype(BF16), meta.astype(I32), gates, [c.astype(I32) for c in counts]


def _xattn_kernel(x_ref, kv_ref, gq_ref, wq_ref, wo_ref, gf_ref, wr_ref, br_ref, tri_ref,
                  x2_ref, h_ref, meta_ref, meta_t_ref, gate_ref, cnt_ref):
    tb = TOK_BLOCK * XATTN_CHAIN
    for chain in range(XATTN_BLOCKS // XATTN_CHAIN):
        rows = slice(chain * tb, (chain + 1) * tb)
        x2, hf, meta, gates, counts = _xattn_block(
            x_ref[rows, :], kv_ref, gq_ref[...], wq_ref, wo_ref, gf_ref[...], wr_ref, br_ref[...],
            tri_ref[...])
        x2_ref[rows, :] = x2
        h_ref[rows, :] = hf
        meta_ref[rows, :] = meta
        meta_t_ref[:, rows] = meta.T[:SUBLANES, :]
        gate_ref[rows, :] = gates
        for r, c in enumerate(counts):
            cnt_ref[chain * XATTN_CHAIN + r] = c


def _xattn_router(x1, kv, gq, wq, wo, gf, wr, br, tri, seq):
    n = x1.shape[0]
    tm = TOK_BLOCK * XATTN_BLOCKS
    per_seq = seq // tm
    const = lambda i: (0, 0)
    return pl.pallas_call(
        _xattn_kernel,
        grid=(n // tm,),
        in_specs=[
            pl.BlockSpec((tm, D_MODEL), lambda i: (i, 0)),
            pl.BlockSpec((None, N_MEM, 2 * D_MODEL), lambda i: (i // per_seq, 0, 0)),
            pl.BlockSpec((1, D_MODEL), const),
            pl.BlockSpec((D_MODEL, D_MODEL), const),
            pl.BlockSpec((D_MODEL, D_MODEL), const),
            pl.BlockSpec((1, D_MODEL), const),
            pl.BlockSpec((D_MODEL, 2 * LANES), const),
            pl.BlockSpec((1, LANES), const),
            pl.BlockSpec((TOK_BLOCK, TOK_BLOCK), const),
        ],
        out_specs=[
            pl.BlockSpec((tm, D_MODEL), lambda i: (i, 0)),
            pl.BlockSpec((tm, D_MODEL), lambda i: (i, 0)),
            pl.BlockSpec((tm, LANES), lambda i: (i, 0)),
            pl.BlockSpec((SUBLANES, tm), lambda i: (0, i)),
            pl.BlockSpec((tm, LANES), lambda i: (i, 0)),
            pl.BlockSpec((XATTN_BLOCKS, 1, LANES), lambda i: (i, 0, 0)),
        ],
        out_shape=[
            jax.ShapeDtypeStruct((n, D_MODEL), F32),
            jax.ShapeDtypeStruct((n, D_MODEL), BF16),
            jax.ShapeDtypeStruct((n, LANES), I32),
            jax.ShapeDtypeStruct((SUBLANES, n), I32),
            jax.ShapeDtypeStruct((n, LANES), F32),
            jax.ShapeDtypeStruct((n // TOK_BLOCK, 1, LANES), I32),
        ],
        compiler_params=_params("arbitrary"),
        name="xattn_router",
    )(x1, kv, gq, wq, wo, gf, wr, br, tri)


def _segment_copy(tables, step, e, vmem_ref, hbm_ref, sem, to_hbm):
    src_ref, dst_ref, cnt_ref = tables
    at = step * N_EXPERTS + e
    n = pl.multiple_of(cnt_ref[at], SUBLANES)
    local = vmem_ref.at[pl.ds(pl.multiple_of(src_ref[at], SUBLANES), n), :]
    remote = hbm_ref.at[pl.ds(pl.multiple_of(dst_ref[at], SUBLANES), n), :]
    return pltpu.make_async_copy(local, remote, sem) if to_hbm else pltpu.make_async_copy(remote, local, sem)


def _each_segment(fn):
    for r in range(SORT_BLOCKS):
        for e in range(N_EXPERTS):
            fn(r, e)


def _dispatch_kernel(src_ref, dst_ref, cnt_ref, fill_ref, h_ref, pos_ref, xs_ref, stage_ref, sem,
                     fill_sem):
    tables = (src_ref, dst_ref, cnt_ref)
    step = pl.program_id(0)
    last = pl.num_programs(0) - 1
    parity = step % 2
    where = lax.broadcasted_iota(I32, (SORT_ROWS, TOK_BLOCK), 0)
    for r in range(SORT_BLOCKS):
        rows = slice(r * TOK_BLOCK, (r + 1) * TOK_BLOCK)
        pos = pos_ref[:, rows]
        hit = where == pos[TOP_K:TOP_K + 1, :]
        for k in range(1, TOP_K):
            hit = jnp.logical_or(hit, where == pos[TOP_K + k:TOP_K + k + 1, :])
        onehot = jnp.where(hit, 1.0, 0.0).astype(BF16)
        stage_ref[parity * SORT_BLOCKS + r] = jnp.dot(onehot, h_ref[rows, :],
                                                      preferred_element_type=F32)

    def copy(at_step, at_parity, r, e):
        return _segment_copy(tables, at_step * SORT_BLOCKS + r, e,
                             stage_ref.at[at_parity * SORT_BLOCKS + r], xs_ref, sem.at[at_parity], True)

    _each_segment(lambda r, e: copy(step, parity, r, e).start())

    @pl.when(step > 0)
    def _():
        _each_segment(lambda r, e: copy(step - 1, 1 - parity, r, e).wait())

    @pl.when(step == last)
    def _():
        _each_segment(lambda r, e: copy(step, parity, r, e).wait())

    @pl.when(step == 0)
    def _():
        stage_ref[SORT_BLOCKS] = jnp.zeros((SORT_ROWS, D_MODEL), F32)
        zeros = stage_ref.at[SORT_BLOCKS]

        def pad_copy(e):
            n = pl.multiple_of(fill_ref[N_EXPERTS + e], SUBLANES)
            at = pl.multiple_of(fill_ref[e], SUBLANES)
            return pltpu.make_async_copy(zeros.at[pl.ds(0, n), :], xs_ref.at[pl.ds(at, n), :], fill_sem)

        def tail_copy(t):
            at = pl.multiple_of(fill_ref[2 * N_EXPERTS] + t * ROW_BLOCK, ROW_BLOCK)
            return pltpu.make_async_copy(zeros.at[pl.ds(0, ROW_BLOCK), :],
                                         xs_ref.at[pl.ds(at, ROW_BLOCK), :], fill_sem)

        def each_pad(fn):
            def body(e, carry):
                @pl.when(fill_ref[N_EXPERTS + e] > 0)
                def _():
                    fn(e)
                return carry
            lax.fori_loop(0, N_EXPERTS, body, 0)

        def each_tail(fn):
            def body(t, carry):
                fn(t)
                return carry
            lax.fori_loop(0, fill_ref[2 * N_EXPERTS + 1], body, 0)

        each_pad(lambda e: pad_copy(e).start())
        each_tail(lambda t: tail_copy(t).start())
        each_pad(lambda e: pad_copy(e).wait())
        each_tail(lambda t: tail_copy(t).wait())


def _dispatch(tables, fill, hf, pos, n_rows):
    n = hf.shape[0]
    tm = TOK_BLOCK * SORT_BLOCKS
    grid_spec = pltpu.PrefetchScalarGridSpec(
        num_scalar_prefetch=4,
        grid=(n // tm,),
        in_specs=[
            pl.BlockSpec((tm, D_MODEL), lambda i, *_: (i, 0)),
            pl.BlockSpec((SUBLANES, tm), lambda i, *_: (0, i)),
        ],
        out_specs=pl.BlockSpec(memory_space=pl.ANY),
        scratch_shapes=[pltpu.VMEM((2 * SORT_BLOCKS, SORT_ROWS, D_MODEL), F32),
                        pltpu.SemaphoreType.DMA((2,)), pltpu.SemaphoreType.DMA],
    )
    return pl.pallas_call(
        _dispatch_kernel,
        grid_spec=grid_spec,
        out_shape=jax.ShapeDtypeStruct((n_rows, D_MODEL), F32),
        compiler_params=_params("arbitrary"),
        name="dispatch",
    )(*tables, fill, hf, pos)


def _expert_kernel(blk_e_ref, next_e_ref, n_used_ref, xs_ref, w1_hbm, b1_ref, w2_hbm, b2_ref,
                   ys_ref, w1f_ref, w2f_ref, w1b_ref, w2b_ref, sem):
    i = pl.program_id(0)
    used = i < n_used_ref[0]

    def fetch(e):
        return (pltpu.make_async_copy(w1_hbm.at[e], w1f_ref, sem.at[0]),
                pltpu.make_async_copy(w2_hbm.at[e], w2f_ref, sem.at[1]))

    @pl.when(jnp.logical_not(used))
    def _():
        ys_ref[...] = jnp.zeros_like(ys_ref)

    @pl.when(used)
    def _():
        e = blk_e_ref[i]

        @pl.when(i == 0)
        def _():
            for copy in fetch(e):
                copy.start()

        @pl.when(jnp.logical_or(i == 0, e != blk_e_ref[jnp.maximum(i - 1, 0)]))
        def _():
            for copy in fetch(e):
                copy.wait()
            w1b_ref[...] = w1f_ref[...].astype(BF16)
            w2b_ref[...] = w2f_ref[...].astype(BF16)

            @pl.when(next_e_ref[i] >= 0)
            def _():
                for copy in fetch(next_e_ref[i]):
                    copy.start()

        gu = jnp.dot(xs_ref[...].astype(BF16), w1b_ref[...], preferred_element_type=F32) + b1_ref[...]
        gate = jnp.minimum(gu[:, :D_FF], SWIGLU_LIMIT)
        up = jnp.clip(gu[:, D_FF:], -SWIGLU_LIMIT, SWIGLU_LIMIT)
        act = (up + 1.0) * gate * jax.nn.sigmoid(SWIGLU_ALPHA * gate)
        ys_ref[...] = jnp.dot(act.astype(BF16), w2b_ref[...], preferred_element_type=F32) + b2_ref[...]


def _experts(blk_e, next_e, n_used, xs, w1, b1, w2, b2):
    n_rows = xs.shape[0]
    bm = ROW_BLOCK
    last_used = lambda i, nu: jnp.maximum(jnp.minimum(i, nu[0] - 1), 0)
    row = lambda i, be, ne, nu: (last_used(i, nu), 0)
    exp3 = lambda i, be, ne, nu: (be[last_used(i, nu)], 0, 0)
    grid_spec = pltpu.PrefetchScalarGridSpec(
        num_scalar_prefetch=3,
        grid=(n_rows // bm,),
        in_specs=[
            pl.BlockSpec((bm, D_MODEL), row),
            pl.BlockSpec(memory_space=pl.ANY),
            pl.BlockSpec((None, 1, 2 * D_FF), exp3),
            pl.BlockSpec(memory_space=pl.ANY),
            pl.BlockSpec((None, 1, D_MODEL), exp3),
        ],
        out_specs=pl.BlockSpec((bm, D_MODEL), lambda i, be, ne, nu: (i, 0)),
        scratch_shapes=[pltpu.VMEM((D_MODEL, 2 * D_FF), F32), pltpu.VMEM((D_FF, D_MODEL), F32),
                        pltpu.VMEM((D_MODEL, 2 * D_FF), BF16), pltpu.VMEM((D_FF, D_MODEL), BF16),
                        pltpu.SemaphoreType.DMA((2,))],
    )
    return pl.pallas_call(
        _expert_kernel,
        grid_spec=grid_spec,
        out_shape=jax.ShapeDtypeStruct((n_rows, D_MODEL), F32),
        compiler_params=_params("arbitrary"),
        name="experts",
    )(blk_e, next_e, n_used, xs, w1, b1, w2, b2)


def _combine_kernel(src_ref, dst_ref, cnt_ref, x_ref, gate_ref, pos_ref, g_ref, ys_ref, o_ref,
                    stage_ref, sem):
    tables = (src_ref, dst_ref, cnt_ref)
    step = pl.program_id(0)
    last = pl.num_programs(0) - 1
    parity = step % 2

    def copy(at_step, at_parity, r, e):
        return _segment_copy(tables, at_step * SORT_BLOCKS + r, e,
                             stage_ref.at[at_parity * SORT_BLOCKS + r], ys_ref, sem.at[at_parity], False)

    @pl.when(step == 0)
    def _():
        stage_ref[...] = jnp.zeros_like(stage_ref)
        _each_segment(lambda r, e: copy(step, parity, r, e).start())

    @pl.when(step < last)
    def _():
        _each_segment(lambda r, e: copy(step + 1, 1 - parity, r, e).start())

    _each_segment(lambda r, e: copy(step, parity, r, e).wait())

    where = lax.broadcasted_iota(I32, (TOK_BLOCK, SORT_ROWS), 1)
    for r in range(SORT_BLOCKS):
        rows = slice(r * TOK_BLOCK, (r + 1) * TOK_BLOCK)
        pos = pos_ref[rows, :]
        gates = gate_ref[rows, :]
        weights = jnp.zeros((TOK_BLOCK, SORT_ROWS), F32)
        for k in range(TOP_K):
            weights = jnp.where(where == pos[:, TOP_K + k:TOP_K + k + 1], gates[:, k:k + 1], weights)
        y = x_ref[rows, :] + jnp.dot(weights.astype(BF16),
                                     stage_ref[parity * SORT_BLOCKS + r].astype(BF16),
                                     preferred_element_type=F32)
        o_ref[rows, :] = _rms(y, g_ref[...])


def _combine(tables, x2, gates, pos, g_final, ys):
    n = x2.shape[0]
    tc = TOK_BLOCK * SORT_BLOCKS
    grid_spec = pltpu.PrefetchScalarGridSpec(
        num_scalar_prefetch=3,
        grid=(n // tc,),
        in_specs=[
            pl.BlockSpec((tc, D_MODEL), lambda i, *_: (i, 0)),
            pl.BlockSpec((tc, LANES), lambda i, *_: (i, 0)),
            pl.BlockSpec((tc, LANES), lambda i, *_: (i, 0)),
            pl.BlockSpec((1, D_MODEL), lambda i, *_: (0, 0)),
            pl.BlockSpec(memory_space=pl.ANY),
        ],
        out_specs=pl.BlockSpec((tc, D_MODEL), lambda i, *_: (i, 0)),
        scratch_shapes=[pltpu.VMEM((2 * SORT_BLOCKS, SORT_ROWS, D_MODEL), F32),
                        pltpu.SemaphoreType.DMA((2,))],
    )
    return pl.pallas_call(
        _combine_kernel,
        grid_spec=grid_spec,
        out_shape=jax.ShapeDtypeStruct((n, D_MODEL), F32),
        compiler_params=_params("arbitrary"),
        name="combine",
    )(*tables, x2, gates, pos, g_final, ys)


def _strict_lower(n):
    r = lax.broadcasted_iota(I32, (n, n), 0)
    c = lax.broadcasted_iota(I32, (n, n), 1)
    return (r > c).astype(BF16)


def _round_up(v, m):
    return (v + m - 1) // m * m


def _routing_tables(cnt, n):
    bm = ROW_BLOCK
    n_tok_blk = n // TOK_BLOCK
    n_rows = _round_up(n * TOP_K + n_tok_blk * N_EXPERTS * SUBLANES, bm) + N_EXPERTS * bm
    n_blk = n_rows // bm
    seg = _segment_rows(cnt[:, 0, :N_EXPERTS]).astype(I32)
    local_start = jnp.cumsum(seg, axis=1) - seg
    total = jnp.sum(seg, axis=0)
    padded = _round_up(total, bm)
    pend = jnp.cumsum(padded)
    pstart = pend - padded
    global_start = pstart[None, :] + jnp.cumsum(seg, axis=0) - seg
    tables = (local_start.reshape(-1), global_start.reshape(-1), seg.reshape(-1))

    n_used = (pend[-1:] // bm).astype(I32)
    blk_row = jnp.arange(n_blk, dtype=I32) * bm
    blk_e = jnp.minimum(jnp.sum(blk_row[:, None] >= pend[None, :], axis=1), N_EXPERTS - 1).astype(I32)
    ids = jnp.arange(N_EXPERTS, dtype=I32)
    later = jnp.where((ids[None, :] > ids[:, None]) & (padded[None, :] > 0), ids[None, :], N_EXPERTS)
    follow = jnp.min(later, axis=1)
    next_e = jnp.where(follow < N_EXPERTS, follow, -1)[blk_e].astype(I32)
    fill = jnp.concatenate([pstart + total, padded - total, pend[-1:], n_blk - pend[-1:] // bm])
    return tables, fill.astype(I32), blk_e, next_e, n_used, n_rows


def _layer(x2d, mem, b, s, g_mix, w_in, g_sb_out, g_pool_out, w_pool, pool_scale, w_out,
           g_mem_q, g_mem_kv, w_mem_q, w_mem_kv, w_mem_o, g_ffn, w_router, b_router,
           w_e_in, b_e_in, w_e_out, b_e_out, g_out):
    n = b * s
    row = lambda v: v.reshape(1, -1).astype(F32)

    proj = _in_proj(x2d, row(g_mix), w_in.astype(BF16))
    sb = _sb_attention(proj.reshape(b, s, IN_PROJ_COLS), -_strict_lower(SB_BLOCK))
    wp = jnp.zeros((POOL_WIDTH, POOL_WIDTH), BF16)
    for g in range(len(POOL_WINDOWS)):
        at = g * POOL_GROUP_DIM
        wp = wp.at[at:at + POOL_GROUP_DIM, at:at + POOL_GROUP_DIM].set(w_pool[g].astype(BF16))
    x1 = _mix_out(x2d, sb.reshape(n, SB_WIDTH), proj, wp, row(pool_scale),
                  row(g_sb_out), row(g_pool_out), w_out.astype(BF16), s)

    kv = _mem_kv(mem, row(g_mem_kv), w_mem_kv.astype(BF16))
    wr = jnp.zeros((D_MODEL, LANES), F32).at[:, :N_EXPERTS].set(w_router)
    wr_hi = wr.astype(BF16)
    wr_lo = (wr - wr_hi.astype(F32)).astype(BF16)
    br = jnp.full((1, LANES), NEG_BIG, F32).at[0, :N_EXPERTS].set(b_router)
    x2, hf, meta, meta_t, gates, cnt = _xattn_router(
        x1, kv, row(g_mem_q), w_mem_q.astype(BF16), w_mem_o.astype(BF16), row(g_ffn),
        jnp.concatenate([wr_hi, wr_lo], axis=1), br, _strict_lower(TOK_BLOCK), s)

    tables, fill, blk_e, next_e, n_used, n_rows = _routing_tables(cnt, n)
    xs = _dispatch(tables, fill, hf, meta_t, n_rows)
    ys = _experts(blk_e, next_e, n_used, xs, w_e_in, b_e_in.reshape(N_EXPERTS, 1, -1),
                  w_e_out, b_e_out.reshape(N_EXPERTS, 1, -1))
    return _combine(tables, x2, gates, meta, row(g_out), ys)


def kernel(x, mem, g_mix, w_in, g_sb_out, g_pool_out, w_pool, pool_scale, w_out, g_mem_q, g_mem_kv,
           w_mem_q, w_mem_kv, w_mem_o, g_ffn, w_router, b_router, w_expert_in, b_expert_in,
           w_expert_out, b_expert_out, g_final):
    b, s, d = x.shape
    depth = g_mix.shape[0]
    assert d == D_MODEL and depth == 1, "the final RMSNorm is fused into the single layer's combine"
    assert s % PROJ_BLOCK == 0 and s % (SB_BLOCK * SB_QBLOCKS) == 0 and s % (TOK_BLOCK * XATTN_BLOCKS) == 0
    assert (b * s) % (TOK_BLOCK * SORT_BLOCKS) == 0
    out = _layer(x.reshape(b * s, d), mem, b, s, g_mix[0], w_in[0], g_sb_out[0], g_pool_out[0],
                 w_pool[0], pool_scale[0], w_out[0], g_mem_q[0], g_mem_kv[0], w_mem_q[0],
                 w_mem_kv[0], w_mem_o[0], g_ffn[0], w_router[0], b_router[0], w_expert_in[0],
                 b_expert_in[0], w_expert_out[0], b_expert_out[0], g_final)
    return out.reshape(b, s, d)
```

```python
import functools
import math

import jax
import jax.numpy as jnp
from jax import lax
from jax.experimental import pallas as pl
from jax.experimental.pallas import tpu as pltpu

F32 = jnp.float32
BF16 = jnp.bfloat16
I32 = jnp.int32

D_MODEL = 1024
SB_HEADS = 8
SB_HEAD_DIM = 64
SB_WIDTH = SB_HEADS * SB_HEAD_DIM
POOL_WINDOWS = (2, 4, 8, 16)
POOL_GROUP_DIM = 128
POOL_WIDTH = len(POOL_WINDOWS) * POOL_GROUP_DIM
IN_PROJ_COLS = 3 * SB_WIDTH + POOL_WIDTH
N_MEM = 256
MEM_HEADS = 4
MEM_HEAD_DIM = D_MODEL // MEM_HEADS
N_EXPERTS = 32
TOP_K = 4
D_FF = D_MODEL
SWIGLU_LIMIT = 7.0
SWIGLU_ALPHA = 1.702
RMS_EPS = 1e-5

LANES = 128
SUBLANES = 8
POOL_HALO = 16
SB_BLOCK = 256
SB_QBLOCKS = 4
SB_DEAD_LOG = -110.0
ROW_BLOCK = 512
TOK_BLOCK = 256
XATTN_BLOCKS = 4
XATTN_CHAIN = 2
SORT_BLOCKS = 2
PROJ_BLOCK = 1024
SORT_ROWS = -(-(TOK_BLOCK * TOP_K + N_EXPERTS * SUBLANES) // LANES) * LANES
VMEM_LIMIT = 48 * 1024 * 1024
NEG_BIG = -1e30


def _rms(x, g):
    ms = jnp.mean(x * x, axis=-1, keepdims=True)
    return x * lax.rsqrt(ms + RMS_EPS) * g


def _segment_rows(count):
    return jnp.ceil(jnp.maximum(count, 1) / SUBLANES) * SUBLANES


def _params(*sem):
    return pltpu.CompilerParams(dimension_semantics=sem, vmem_limit_bytes=VMEM_LIMIT)


def _in_proj_kernel(x_ref, g_ref, w_ref, o_ref):
    h = _rms(x_ref[...], g_ref[...]).astype(BF16)
    acc = jnp.dot(h, w_ref[...], preferred_element_type=F32)
    scale = 1.0 / math.sqrt(SB_HEAD_DIM)
    o_ref[:, :SB_WIDTH] = (acc[:, :SB_WIDTH] * scale).astype(BF16)
    o_ref[:, SB_WIDTH:] = acc[:, SB_WIDTH:].astype(BF16)


def _in_proj(x2d, g, w):
    n = x2d.shape[0]
    tm = PROJ_BLOCK
    return pl.pallas_call(
        _in_proj_kernel,
        grid=(n // tm,),
        in_specs=[
            pl.BlockSpec((tm, D_MODEL), lambda i: (i, 0)),
            pl.BlockSpec((1, D_MODEL), lambda i: (0, 0)),
            pl.BlockSpec((D_MODEL, IN_PROJ_COLS), lambda i: (0, 0)),
        ],
        out_specs=pl.BlockSpec((tm, IN_PROJ_COLS), lambda i: (i, 0)),
        out_shape=jax.ShapeDtypeStruct((n, IN_PROJ_COLS), BF16),
        compiler_params=_params("arbitrary"),
        name="in_proj",
    )(x2d, g, w)


def _sb_chain(qh, kb, vb, neg_tri, c, causal):
    z = lax.dot_general(qh, kb, (((1,), (1,)), ((), ())), preferred_element_type=F32)
    sp = jnp.maximum(z, 0.0) + jnp.log(1.0 + jnp.exp(-jnp.abs(z)))
    if causal is not None:
        sp = jnp.where(causal, sp, 0.0)
    after = jnp.dot(sp.astype(BF16), neg_tri, preferred_element_type=F32)
    w = jnp.exp((z - sp) + after)
    if causal is not None:
        w = jnp.where(causal, w, 0.0)
    pv = jnp.dot(w.astype(BF16), vb, preferred_element_type=F32) * jnp.exp(c)
    return pv, c + (after[:, :1] - sp[:, :1])


def _sb_kernel(q_ref, k_ref, v_ref, tri_ref, o_ref, acc_ref, c_ref):
    tq = SB_BLOCK
    lane = lax.broadcasted_iota(I32, (tq, LANES), 1)
    neg_tri = tri_ref[...]
    row = lax.broadcasted_iota(I32, (2 * tq, tq), 0)
    col = lax.broadcasted_iota(I32, (2 * tq, tq), 1)
    causal = col < jnp.where(row >= tq, row - tq, row)

    def stacked(u):
        q = q_ref[u * tq:(u + 1) * tq, :]
        zero = jnp.zeros_like(q)
        return jnp.concatenate(
            [jnp.where(lane < SB_HEAD_DIM, q, zero), jnp.where(lane >= SB_HEAD_DIM, q, zero)], axis=0)

    def step(qs, j, c, diagonal):
        off = pl.multiple_of(j * tq, tq)
        return _sb_chain(qs, k_ref[pl.ds(off, tq), :], v_ref[pl.ds(off, tq), :], neg_tri, c,
                         causal if diagonal else None)

    blocks = []
    for u in range(SB_QBLOCKS):
        i = pl.program_id(2) * SB_QBLOCKS + u
        qs = stacked(u)
        pv, c = step(qs, i, jnp.zeros((2 * tq, LANES), F32), True)
        pv_left, c_left = step(qs, jnp.maximum(i - 1, 0), c, False)
        has_left = i > 0
        acc_ref[u] = pv + jnp.where(has_left, pv_left, 0.0)
        c_ref[u] = jnp.where(has_left, c_left, c)
        blocks.append((i, qs))

    for u, (i, qs) in enumerate(blocks):
        def alive():
            return jnp.max(c_ref[u]) > SB_DEAD_LOG

        def cond(carry):
            j, live = carry
            return jnp.logical_and(j >= 0, live)

        def body(carry):
            j, _ = carry
            pv, c = step(qs, j, c_ref[u], False)
            acc_ref[u] += pv
            c_ref[u] = c
            return j - 1, alive()

        lax.while_loop(cond, body, (i - 2, alive()))
        o_ref[u * tq:(u + 1) * tq, :] = jnp.where(
            lane < SB_HEAD_DIM, acc_ref[u, :tq, :], acc_ref[u, tq:, :]).astype(o_ref.dtype)


def _sb_attention(proj3, neg_tri):
    b, s, _ = proj3.shape
    tq = SB_BLOCK * SB_QBLOCKS
    pairs = SB_WIDTH // LANES
    return pl.pallas_call(
        _sb_kernel,
        grid=(b, pairs, s // tq),
        in_specs=[
            pl.BlockSpec((None, tq, LANES), lambda bi, p, i: (bi, i, p)),
            pl.BlockSpec((None, s, LANES), lambda bi, p, i: (bi, 0, pairs + p)),
            pl.BlockSpec((None, s, LANES), lambda bi, p, i: (bi, 0, 2 * pairs + p)),
            pl.BlockSpec((SB_BLOCK, SB_BLOCK), lambda bi, p, i: (0, 0)),
        ],
        out_specs=pl.BlockSpec((None, tq, LANES), lambda bi, p, i: (bi, i, p)),
        out_shape=jax.ShapeDtypeStruct((b, s, SB_WIDTH), BF16),
        scratch_shapes=[pltpu.VMEM((SB_QBLOCKS, 2 * SB_BLOCK, LANES), F32),
                        pltpu.VMEM((SB_QBLOCKS, 2 * SB_BLOCK, LANES), F32)],
        compiler_params=_params("arbitrary", "arbitrary", "arbitrary"),
        name="sb_attn",
    )(proj3, proj3, proj3, neg_tri)


def _mix_kernel(x_ref, sb_ref, u_ref, halo_ref, wp_ref, ps_ref, gsb_ref, gpool_ref, wout_ref,
                o_ref, *, seq):
    tm = x_ref.shape[0]
    t0 = (pl.program_id(0) * tm) % seq
    u = u_ref[...].astype(F32)
    halo = jnp.where(t0 == 0, 0.0, halo_ref[...].astype(F32))
    a = jnp.concatenate([halo, u], axis=0)
    pos = t0 + lax.broadcasted_iota(I32, (tm, POOL_GROUP_DIM), 0)
    outs = []
    for g, win in enumerate(POOL_WINDOWS):
        lo, hi = g * POOL_GROUP_DIM, (g + 1) * POOL_GROUP_DIM
        s = a[:, lo:hi]
        sh = 1
        while sh < win:
            s = s + pltpu.roll(s, sh, axis=0)
            sh *= 2
        cnt = jnp.minimum(pos + 1, win).astype(F32)
        outs.append(s[POOL_HALO:, :] / cnt - u[:, lo:hi])
    pooled = jnp.concatenate(outs, axis=1).astype(BF16)
    pooled = jnp.dot(pooled, wp_ref[...], preferred_element_type=F32) * ps_ref[...]
    pool_n = _rms(pooled, gpool_ref[...]).astype(BF16)
    sb_n = _rms(sb_ref[...].astype(F32), gsb_ref[...]).astype(BF16)
    y = jnp.dot(sb_n, wout_ref[:SB_WIDTH, :], preferred_element_type=F32)
    y = y + jnp.dot(pool_n, wout_ref[SB_WIDTH:, :], preferred_element_type=F32)
    o_ref[...] = x_ref[...] + y


def _mix_out(x2d, sb2d, proj2d, wp, ps, gsb, gpool, wout, seq):
    n = x2d.shape[0]
    tm = PROJ_BLOCK
    ucol = 3 * SB_WIDTH // POOL_WIDTH
    hb = tm // POOL_HALO
    return pl.pallas_call(
        functools.partial(_mix_kernel, seq=seq),
        grid=(n // tm,),
        in_specs=[
            pl.BlockSpec((tm, D_MODEL), lambda i: (i, 0)),
            pl.BlockSpec((tm, SB_WIDTH), lambda i: (i, 0)),
            pl.BlockSpec((tm, POOL_WIDTH), lambda i: (i, ucol)),
            pl.BlockSpec((POOL_HALO, POOL_WIDTH), lambda i: (jnp.maximum(i * hb - 1, 0), ucol)),
            pl.BlockSpec((POOL_WIDTH, POOL_WIDTH), lambda i: (0, 0)),
            pl.BlockSpec((1, POOL_WIDTH), lambda i: (0, 0)),
            pl.BlockSpec((1, SB_WIDTH), lambda i: (0, 0)),
            pl.BlockSpec((1, POOL_WIDTH), lambda i: (0, 0)),
            pl.BlockSpec((D_MODEL, D_MODEL), lambda i: (0, 0)),
        ],
        out_specs=pl.BlockSpec((tm, D_MODEL), lambda i: (i, 0)),
        out_shape=jax.ShapeDtypeStruct((n, D_MODEL), F32),
        compiler_params=_params("arbitrary"),
        name="mix_out",
    )(x2d, sb2d, proj2d, proj2d, wp, ps, gsb, gpool, wout)


def _mem_kv_kernel(m_ref, g_ref, w_ref, o_ref):
    h = _rms(m_ref[...], g_ref[...]).astype(BF16)
    o_ref[...] = jnp.dot(h, w_ref[...], preferred_element_type=F32).astype(BF16)


def _mem_kv(mem, g, w):
    b = mem.shape[0]
    return pl.pallas_call(
        _mem_kv_kernel,
        grid=(b,),
        in_specs=[
            pl.BlockSpec((None, N_MEM, D_MODEL), lambda i: (i, 0, 0)),
            pl.BlockSpec((1, D_MODEL), lambda i: (0, 0)),
            pl.BlockSpec((D_MODEL, 2 * D_MODEL), lambda i: (0, 0)),
        ],
        out_specs=pl.BlockSpec((None, N_MEM, 2 * D_MODEL), lambda i: (i, 0, 0)),
        out_shape=jax.ShapeDtypeStruct((b, N_MEM, 2 * D_MODEL), BF16),
        compiler_params=_params("arbitrary"),
        name="mem_kv",
    )(mem, g, w)


def _xattn_block(x, kv_ref, gq, wq_ref, wo_ref, gf, wr_ref, br, tri):
    tm = x.shape[0]
    hq = _rms(x, gq).astype(BF16)
    q = jnp.dot(hq, wq_ref[...], preferred_element_type=F32) * (1.0 / math.sqrt(MEM_HEAD_DIM))
    q = q.astype(BF16)
    outs = []
    for h in range(MEM_HEADS):
        lo, hi = h * MEM_HEAD_DIM, (h + 1) * MEM_HEAD_DIM
        s = lax.dot_general(q[:, lo:hi], kv_ref[:, lo:hi], (((1,), (1,)), ((), ())),
                            preferred_element_type=F32)
        p = jnp.exp(s - jnp.max(s, axis=-1, keepdims=True))
        denom = jnp.sum(p, axis=-1, keepdims=True)
        o = jnp.dot(p.astype(BF16), kv_ref[:, D_MODEL + lo:D_MODEL + hi], preferred_element_type=F32)
        outs.append(o / denom)
    o = jnp.concatenate(outs, axis=1).astype(BF16)
    x2 = x + jnp.dot(o, wo_ref[...], preferred_element_type=F32)
    hf = _rms(x2, gf)

    h_hi = hf.astype(BF16)
    h_lo = (hf - h_hi.astype(F32)).astype(BF16)
    hw = jnp.dot(h_hi, wr_ref[...], preferred_element_type=F32)
    lw = jnp.dot(h_lo, wr_ref[:, :LANES], preferred_element_type=F32)
    logits = (hw[:, :LANES] + (hw[:, LANES:] + lw)) + br

    lane = lax.broadcasted_iota(I32, (tm, LANES), 1).astype(F32)
    work = logits
    vals, idxs, sels = [], [], []
    for _ in range(TOP_K):
        m = jnp.max(work, axis=-1, keepdims=True)
        idx = jnp.min(jnp.where(work == m, lane, float(LANES)), axis=-1, keepdims=True)
        sel = lane == idx
        vals.append(m)
        idxs.append(idx)
        sels.append(sel)
        work = jnp.where(sel, -3e38, work)
    exps = [jnp.exp(v - vals[0]) for v in vals]
    den = exps[0] + exps[1] + exps[2] + exps[3]

    onehot = jnp.zeros((tm, LANES), F32)
    for sel in sels:
        onehot = onehot + sel.astype(F32)
    blocks = [onehot[r:r + TOK_BLOCK, :] for r in range(0, tm, TOK_BLOCK)]
    before = jnp.concatenate(
        [jnp.dot(tri, blk.astype(BF16), preferred_element_type=F32) for blk in blocks], axis=0)
    counts = [jnp.sum(blk, axis=0, keepdims=True) for blk in blocks]
    upper = (lax.broadcasted_iota(I32, (LANES, LANES), 0)
             < lax.broadcasted_iota(I32, (LANES, LANES), 1)).astype(BF16)
    starts = []
    for cnt in counts:
        seg = jnp.where(lane[:SUBLANES, :] < float(N_EXPERTS), _segment_rows(cnt), 0.0)
        seg = jnp.broadcast_to(seg, (SUBLANES, LANES)).astype(BF16)
        start = jnp.dot(seg, upper, preferred_element_type=F32)[:1, :]
        starts.append(jnp.broadcast_to(start, (TOK_BLOCK, LANES)))
    first_row = before + jnp.concatenate(starts, axis=0)
    meta = jnp.zeros((tm, LANES), F32)
    gates = jnp.zeros((tm, LANES), F32)
    for k in range(TOP_K):
        pos = jnp.sum(jnp.where(sels[k], first_row, 0.0), axis=-1, keepdims=True)
        meta = jnp.where(lane == float(k), idxs[k], meta)
        meta = jnp.where(lane == float(TOP_K + k), pos, meta)
        gates = jnp.where(lane == float(k), exps[k] / den, gates)
    return x2, hf.astype(BF16), meta.astype(I32), gates, [c.astype(I32) for c in counts]


def _xattn_kernel(x_ref, kv_ref, gq_ref, wq_ref, wo_ref, gf_ref, wr_ref, br_ref, tri_ref,
                  x2_ref, h_ref, meta_ref, meta_t_ref, gate_ref, cnt_ref):
    tb = TOK_BLOCK * XATTN_CHAIN
    for chain in range(XATTN_BLOCKS // XATTN_CHAIN):
        rows = slice(chain * tb, (chain + 1) * tb)
        x2, hf, meta, gates, counts = _xattn_block(
            x_ref[rows, :], kv_ref, gq_ref[...], wq_ref, wo_ref, gf_ref[...], wr_ref, br_ref[...],
            tri_ref[...])
        x2_ref[rows, :] = x2
        h_ref[rows, :] = hf
        meta_ref[rows, :] = meta
        meta_t_ref[:, rows] = meta.T[:SUBLANES, :]
        gate_ref[rows, :] = gates
        for r, c in enumerate(counts):
            cnt_ref[chain * XATTN_CHAIN + r] = c


def _xattn_router(x1, kv, gq, wq, wo, gf, wr, br, tri, seq):
    n = x1.shape[0]
    tm = TOK_BLOCK * XATTN_BLOCKS
    per_seq = seq // tm
    const = lambda i: (0, 0)
    return pl.pallas_call(
        _xattn_kernel,
        grid=(n // tm,),
        in_specs=[
            pl.BlockSpec((tm, D_MODEL), lambda i: (i, 0)),
            pl.BlockSpec((None, N_MEM, 2 * D_MODEL), lambda i: (i // per_seq, 0, 0)),
            pl.BlockSpec((1, D_MODEL), const),
            pl.BlockSpec((D_MODEL, D_MODEL), const),
            pl.BlockSpec((D_MODEL, D_MODEL), const),
            pl.BlockSpec((1, D_MODEL), const),
            pl.BlockSpec((D_MODEL, 2 * LANES), const),
            pl.BlockSpec((1, LANES), const),
            pl.BlockSpec((TOK_BLOCK, TOK_BLOCK), const),
        ],
        out_specs=[
            pl.BlockSpec((tm, D_MODEL), lambda i: (i, 0)),
            pl.BlockSpec((tm, D_MODEL), lambda i: (i, 0)),
            pl.BlockSpec((tm, LANES), lambda i: (i, 0)),
            pl.BlockSpec((SUBLANES, tm), lambda i: (0, i)),
            pl.BlockSpec((tm, LANES), lambda i: (i, 0)),
            pl.BlockSpec((XATTN_BLOCKS, 1, LANES), lambda i: (i, 0, 0)),
        ],
        out_shape=[
            jax.ShapeDtypeStruct((n, D_MODEL), F32),
            jax.ShapeDtypeStruct((n, D_MODEL), BF16),
            jax.ShapeDtypeStruct((n, LANES), I32),
            jax.ShapeDtypeStruct((SUBLANES, n), I32),
            jax.ShapeDtypeStruct((n, LANES), F32),
            jax.ShapeDtypeStruct((n // TOK_BLOCK, 1, LANES), I32),
        ],
        compiler_params=_params("arbitrary"),
        name="xattn_router",
    )(x1, kv, gq, wq, wo, gf, wr, br, tri)


def _segment_copy(tables, step, e, vmem_ref, hbm_ref, sem, to_hbm):
    src_ref, dst_ref, cnt_ref = tables
    at = step * N_EXPERTS + e
    n = pl.multiple_of(cnt_ref[at], SUBLANES)
    local = vmem_ref.at[pl.ds(pl.multiple_of(src_ref[at], SUBLANES), n), :]
    remote = hbm_ref.at[pl.ds(pl.multiple_of(dst_ref[at], SUBLANES), n), :]
    return pltpu.make_async_copy(local, remote, sem) if to_hbm else pltpu.make_async_copy(remote, local, sem)


def _each_segment(fn):
    for r in range(SORT_BLOCKS):
        for e in range(N_EXPERTS):
            fn(r, e)


def _dispatch_kernel(src_ref, dst_ref, cnt_ref, fill_ref, h_ref, pos_ref, xs_ref, stage_ref,
                     zero_ref, sem, fill_sem):
    tables = (src_ref, dst_ref, cnt_ref)
    step = pl.program_id(0)
    last = pl.num_programs(0) - 1
    parity = step % 2
    where = lax.broadcasted_iota(I32, (SORT_ROWS, TOK_BLOCK), 0)
    for r in range(SORT_BLOCKS):
        rows = slice(r * TOK_BLOCK, (r + 1) * TOK_BLOCK)
        pos = pos_ref[:, rows]
        hit = where == pos[TOP_K:TOP_K + 1, :]
        for k in range(1, TOP_K):
            hit = jnp.logical_or(hit, where == pos[TOP_K + k:TOP_K + k + 1, :])
        onehot = jnp.where(hit, 1.0, 0.0).astype(BF16)
        stage_ref[parity * SORT_BLOCKS + r] = jnp.dot(onehot, h_ref[rows, :],
                                                      preferred_element_type=F32)

    def copy(at_step, at_parity, r, e):
        return _segment_copy(tables, at_step * SORT_BLOCKS + r, e,
                             stage_ref.at[at_parity * SORT_BLOCKS + r], xs_ref, sem.at[at_parity], True)

    _each_segment(lambda r, e: copy(step, parity, r, e).start())

    @pl.when(step > 0)
    def _():
        _each_segment(lambda r, e: copy(step - 1, 1 - parity, r, e).wait())

    @pl.when(step == last)
    def _():
        _each_segment(lambda r, e: copy(step, parity, r, e).wait())

    def pad_copy(e):
        n = pl.multiple_of(fill_ref[N_EXPERTS + e], SUBLANES)
        at = pl.multiple_of(fill_ref[e], SUBLANES)
        return pltpu.make_async_copy(zero_ref.at[pl.ds(0, n), :], xs_ref.at[pl.ds(at, n), :], fill_sem)

    def tail_copy(t):
        at = pl.multiple_of(fill_ref[2 * N_EXPERTS] + t * ROW_BLOCK, ROW_BLOCK)
        return pltpu.make_async_copy(zero_ref, xs_ref.at[pl.ds(at, ROW_BLOCK), :], fill_sem)

    def each_pad(fn):
        def body(e, carry):
            @pl.when(fill_ref[N_EXPERTS + e] > 0)
            def _():
                fn(e)
            return carry
        lax.fori_loop(0, N_EXPERTS, body, 0)

    def each_tail(fn):
        def body(t, carry):
            fn(t)
            return carry
        lax.fori_loop(0, fill_ref[2 * N_EXPERTS + 1], body, 0)

    @pl.when(step == 0)
    def _():
        zero_ref[...] = jnp.zeros_like(zero_ref)
        each_pad(lambda e: pad_copy(e).start())
        each_tail(lambda t: tail_copy(t).start())

    @pl.when(step == last)
    def _():
        each_pad(lambda e: pad_copy(e).wait())
        each_tail(lambda t: tail_copy(t).wait())


def _dispatch(tables, fill, hf, pos, n_rows):
    n = hf.shape[0]
    tm = TOK_BLOCK * SORT_BLOCKS
    grid_spec = pltpu.PrefetchScalarGridSpec(
        num_scalar_prefetch=4,
        grid=(n // tm,),
        in_specs=[
            pl.BlockSpec((tm, D_MODEL), lambda i, *_: (i, 0)),
            pl.BlockSpec((SUBLANES, tm), lambda i, *_: (0, i)),
        ],
        out_specs=pl.BlockSpec(memory_space=pl.ANY),
        scratch_shapes=[pltpu.VMEM((2 * SORT_BLOCKS, SORT_ROWS, D_MODEL), F32),
                        pltpu.VMEM((ROW_BLOCK, D_MODEL), F32),
                        pltpu.SemaphoreType.DMA((2,)), pltpu.SemaphoreType.DMA],
    )
    return pl.pallas_call(
        _dispatch_kernel,
        grid_spec=grid_spec,
        out_shape=jax.ShapeDtypeStruct((n_rows, D_MODEL), F32),
        compiler_params=_params("arbitrary"),
        name="dispatch",
    )(*tables, fill, hf, pos)


def _expert_kernel(blk_e_ref, next_e_ref, n_used_ref, xs_ref, w1_hbm, b1_ref, w2_hbm, b2_ref,
                   ys_ref, w1f_ref, w2f_ref, w1b_ref, w2b_ref, sem):
    i = pl.program_id(0)
    used = i < n_used_ref[0]

    def fetch(e):
        return (pltpu.make_async_copy(w1_hbm.at[e], w1f_ref, sem.at[0]),
                pltpu.make_async_copy(w2_hbm.at[e], w2f_ref, sem.at[1]))

    @pl.when(used)
    def _():
        e = blk_e_ref[i]

        @pl.when(i == 0)
        def _():
            for copy in fetch(e):
                copy.start()

        @pl.when(jnp.logical_or(i == 0, e != blk_e_ref[jnp.maximum(i - 1, 0)]))
        def _():
            for copy in fetch(e):
                copy.wait()
            w1b_ref[...] = w1f_ref[...].astype(BF16)
            w2b_ref[...] = w2f_ref[...].astype(BF16)

            @pl.when(next_e_ref[i] >= 0)
            def _():
                for copy in fetch(next_e_ref[i]):
                    copy.start()

        gu = jnp.dot(xs_ref[...].astype(BF16), w1b_ref[...], preferred_element_type=F32) + b1_ref[...]
        gate = jnp.minimum(gu[:, :D_FF], SWIGLU_LIMIT)
        up = jnp.clip(gu[:, D_FF:], -SWIGLU_LIMIT, SWIGLU_LIMIT)
        act = (up + 1.0) * gate * jax.nn.sigmoid(SWIGLU_ALPHA * gate)
        ys_ref[...] = jnp.dot(act.astype(BF16), w2b_ref[...], preferred_element_type=F32) + b2_ref[...]


def _experts(blk_e, next_e, n_used, xs, w1, b1, w2, b2):
    n_rows = xs.shape[0]
    bm = ROW_BLOCK
    last_used = lambda i, nu: jnp.maximum(jnp.minimum(i, nu[0] - 1), 0)
    row = lambda i, be, ne, nu: (last_used(i, nu), 0)
    exp3 = lambda i, be, ne, nu: (be[last_used(i, nu)], 0, 0)
    grid_spec = pltpu.PrefetchScalarGridSpec(
        num_scalar_prefetch=3,
        grid=(n_rows // bm,),
        in_specs=[
            pl.BlockSpec((bm, D_MODEL), row),
            pl.BlockSpec(memory_space=pl.ANY),
            pl.BlockSpec((None, 1, 2 * D_FF), exp3),
            pl.BlockSpec(memory_space=pl.ANY),
            pl.BlockSpec((None, 1, D_MODEL), exp3),
        ],
        out_specs=pl.BlockSpec((bm, D_MODEL), row),
        scratch_shapes=[pltpu.VMEM((D_MODEL, 2 * D_FF), F32), pltpu.VMEM((D_FF, D_MODEL), F32),
                        pltpu.VMEM((D_MODEL, 2 * D_FF), BF16), pltpu.VMEM((D_FF, D_MODEL), BF16),
                        pltpu.SemaphoreType.DMA((2,))],
    )
    return pl.pallas_call(
        _expert_kernel,
        grid_spec=grid_spec,
        out_shape=jax.ShapeDtypeStruct((n_rows, D_MODEL), F32),
        input_output_aliases={3: 0},
        compiler_params=_params("arbitrary"),
        name="experts",
    )(blk_e, next_e, n_used, xs, w1, b1, w2, b2)


def _combine_kernel(src_ref, dst_ref, cnt_ref, x_ref, gate_ref, pos_ref, g_ref, ys_ref, o_ref,
                    stage_ref, sem):
    tables = (src_ref, dst_ref, cnt_ref)
    step = pl.program_id(0)
    last = pl.num_programs(0) - 1
    parity = step % 2

    def copy(at_step, at_parity, r, e):
        return _segment_copy(tables, at_step * SORT_BLOCKS + r, e,
                             stage_ref.at[at_parity * SORT_BLOCKS + r], ys_ref, sem.at[at_parity], False)

    @pl.when(step == 0)
    def _():
        stage_ref[...] = jnp.zeros_like(stage_ref)
        _each_segment(lambda r, e: copy(step, parity, r, e).start())

    @pl.when(step < last)
    def _():
        _each_segment(lambda r, e: copy(step + 1, 1 - parity, r, e).start())

    _each_segment(lambda r, e: copy(step, parity, r, e).wait())

    where = lax.broadcasted_iota(I32, (TOK_BLOCK, SORT_ROWS), 1)
    for r in range(SORT_BLOCKS):
        rows = slice(r * TOK_BLOCK, (r + 1) * TOK_BLOCK)
        pos = pos_ref[rows, :]
        gates = gate_ref[rows, :]
        weights = jnp.zeros((TOK_BLOCK, SORT_ROWS), F32)
        for k in range(TOP_K):
            weights = jnp.where(where == pos[:, TOP_K + k:TOP_K + k + 1], gates[:, k:k + 1], weights)
        y = x_ref[rows, :] + jnp.dot(weights.astype(BF16),
                                     stage_ref[parity * SORT_BLOCKS + r].astype(BF16),
                                     preferred_element_type=F32)
        o_ref[rows, :] = _rms(y, g_ref[...])


def _combine(tables, x2, gates, pos, g_final, ys):
    n = x2.shape[0]
    tc = TOK_BLOCK * SORT_BLOCKS
    grid_spec = pltpu.PrefetchScalarGridSpec(
        num_scalar_prefetch=3,
        grid=(n // tc,),
        in_specs=[
            pl.BlockSpec((tc, D_MODEL), lambda i, *_: (i, 0)),
            pl.BlockSpec((tc, LANES), lambda i, *_: (i, 0)),
            pl.BlockSpec((tc, LANES), lambda i, *_: (i, 0)),
            pl.BlockSpec((1, D_MODEL), lambda i, *_: (0, 0)),
            pl.BlockSpec(memory_space=pl.ANY),
        ],
        out_specs=pl.BlockSpec((tc, D_MODEL), lambda i, *_: (i, 0)),
        scratch_shapes=[pltpu.VMEM((2 * SORT_BLOCKS, SORT_ROWS, D_MODEL), F32),
                        pltpu.SemaphoreType.DMA((2,))],
    )
    return pl.pallas_call(
        _combine_kernel,
        grid_spec=grid_spec,
        out_shape=jax.ShapeDtypeStruct((n, D_MODEL), F32),
        compiler_params=_params("arbitrary"),
        name="combine",
    )(*tables, x2, gates, pos, g_final, ys)


def _strict_lower(n):
    r = lax.broadcasted_iota(I32, (n, n), 0)
    c = lax.broadcasted_iota(I32, (n, n), 1)
    return (r > c).astype(BF16)


def _round_up(v, m):
    return (v + m - 1) // m * m


def _routing_tables(cnt, n):
    bm = ROW_BLOCK
    n_tok_blk = n // TOK_BLOCK
    n_rows = _round_up(n * TOP_K + n_tok_blk * N_EXPERTS * SUBLANES, bm) + N_EXPERTS * bm
    n_blk = n_rows // bm
    seg = _segment_rows(cnt[:, 0, :N_EXPERTS]).astype(I32)
    local_start = jnp.cumsum(seg, axis=1) - seg
    total = jnp.sum(seg, axis=0)
    padded = _round_up(total, bm)
    pend = jnp.cumsum(padded)
    pstart = pend - padded
    global_start = pstart[None, :] + jnp.cumsum(seg, axis=0) - seg
    tables = (local_start.reshape(-1), global_start.reshape(-1), seg.reshape(-1))

    n_used = (pend[-1:] // bm).astype(I32)
    blk_row = jnp.arange(n_blk, dtype=I32) * bm
    blk_e = jnp.minimum(jnp.sum(blk_row[:, None] >= pend[None, :], axis=1), N_EXPERTS - 1).astype(I32)
    ids = jnp.arange(N_EXPERTS, dtype=I32)
    later = jnp.where((ids[None, :] > ids[:, None]) & (padded[None, :] > 0), ids[None, :], N_EXPERTS)
    follow = jnp.min(later, axis=1)
    next_e = jnp.where(follow < N_EXPERTS, follow, -1)[blk_e].astype(I32)
    fill = jnp.concatenate([pstart + total, padded - total, pend[-1:], n_blk - pend[-1:] // bm])
    return tables, fill.astype(I32), blk_e, next_e, n_used, n_rows


def _layer(x2d, mem, b, s, g_mix, w_in, g_sb_out, g_pool_out, w_pool, pool_scale, w_out,
           g_mem_q, g_mem_kv, w_mem_q, w_mem_kv, w_mem_o, g_ffn, w_router, b_router,
           w_e_in, b_e_in, w_e_out, b_e_out, g_out):
    n = b * s
    row = lambda v: v.reshape(1, -1).astype(F32)

    proj = _in_proj(x2d, row(g_mix), w_in.astype(BF16))
    sb = _sb_attention(proj.reshape(b, s, IN_PROJ_COLS), -_strict_lower(SB_BLOCK))
    wp = jnp.zeros((POOL_WIDTH, POOL_WIDTH), BF16)
    for g in range(len(POOL_WINDOWS)):
        at = g * POOL_GROUP_DIM
        wp = wp.at[at:at + POOL_GROUP_DIM, at:at + POOL_GROUP_DIM].set(w_pool[g].astype(BF16))
    x1 = _mix_out(x2d, sb.reshape(n, SB_WIDTH), proj, wp, row(pool_scale),
                  row(g_sb_out), row(g_pool_out), w_out.astype(BF16), s)

    kv = _mem_kv(mem, row(g_mem_kv), w_mem_kv.astype(BF16))
    wr = jnp.zeros((D_MODEL, LANES), F32).at[:, :N_EXPERTS].set(w_router)
    wr_hi = wr.astype(BF16)
    wr_lo = (wr - wr_hi.astype(F32)).astype(BF16)
    br = jnp.full((1, LANES), NEG_BIG, F32).at[0, :N_EXPERTS].set(b_router)
    x2, hf, meta, meta_t, gates, cnt = _xattn_router(
        x1, kv, row(g_mem_q), w_mem_q.astype(BF16), w_mem_o.astype(BF16), row(g_ffn),
        jnp.concatenate([wr_hi, wr_lo], axis=1), br, _strict_lower(TOK_BLOCK), s)

    tables, fill, blk_e, next_e, n_used, n_rows = _routing_tables(cnt, n)
    xs = _dispatch(tables, fill, hf, meta_t, n_rows)
    ys = _experts(blk_e, next_e, n_used, xs, w_e_in, b_e_in.reshape(N_EXPERTS, 1, -1),
                  w_e_out, b_e_out.reshape(N_EXPERTS, 1, -1))
    return _combine(tables, x2, gates, meta, row(g_out), ys)


def kernel(x, mem, g_mix, w_in, g_sb_out, g_pool_out, w_pool, pool_scale, w_out, g_mem_q, g_mem_kv,
           w_mem_q, w_mem_kv, w_mem_o, g_ffn, w_router, b_router, w_expert_in, b_expert_in,
           w_expert_out, b_expert_out, g_final):
    b, s, d = x.shape
    depth = g_mix.shape[0]
    assert d == D_MODEL and depth == 1, "the final RMSNorm is fused into the single layer's combine"
    assert s % PROJ_BLOCK == 0 and s % (SB_BLOCK * SB_QBLOCKS) == 0 and s % (TOK_BLOCK * XATTN_BLOCKS) == 0
    assert (b * s) % (TOK_BLOCK * SORT_BLOCKS) == 0
    out = _layer(x.reshape(b * s, d), mem, b, s, g_mix[0], w_in[0], g_sb_out[0], g_pool_out[0],
                 w_pool[0], pool_scale[0], w_out[0], g_mem_q[0], g_mem_kv[0], w_mem_q[0],
                 w_mem_kv[0], w_mem_o[0], g_ffn[0], w_router[0], b_router[0], w_expert_in[0],
                 b_expert_in[0], w_expert_out[0], b_expert_out[0], g_final)
    return out.reshape(b, s, d)
```

```python
import functools
import math

import jax
import jax.numpy as jnp
from jax import lax
from jax.experimental import pallas as pl
from jax.experimental.pallas import tpu as pltpu

F32 = jnp.float32
BF16 = jnp.bfloat16
I32 = jnp.int32

D_MODEL = 1024
SB_HEADS = 8
SB_HEAD_DIM = 64
SB_WIDTH = SB_HEADS * SB_HEAD_DIM
POOL_WINDOWS = (2, 4, 8, 16)
POOL_GROUP_DIM = 128
POOL_WIDTH = len(POOL_WINDOWS) * POOL_GROUP_DIM
IN_PROJ_COLS = 3 * SB_WIDTH + POOL_WIDTH
N_MEM = 256
MEM_HEADS = 4
MEM_HEAD_DIM = D_MODEL // MEM_HEADS
N_EXPERTS = 32
TOP_K = 4
D_FF = D_MODEL
SWIGLU_LIMIT = 7.0
SWIGLU_ALPHA = 1.702
RMS_EPS = 1e-5

LANES = 128
SUBLANES = 8
POOL_HALO = 16
SB_BLOCK = 256
SB_QBLOCKS = 4
SB_DEAD_LOG = -110.0
ROW_BLOCK = 512
TOK_BLOCK = 256
XATTN_BLOCKS = 4
XATTN_CHAIN = 2
SORT_BLOCKS = 2
PROJ_BLOCK = 1024
SORT_ROWS = -(-(TOK_BLOCK * TOP_K + N_EXPERTS * 2 * SUBLANES) // LANES) * LANES
VMEM_LIMIT = 48 * 1024 * 1024
NEG_BIG = -1e30


def _rms(x, g):
    ms = jnp.mean(x * x, axis=-1, keepdims=True)
    return x * lax.rsqrt(ms + RMS_EPS) * g


def _segment_plan(before, count):
    aligned = jnp.floor(before / SUBLANES) * SUBLANES
    carried = before - aligned
    rows = jnp.ceil(jnp.maximum(carried + count, 1) / SUBLANES) * SUBLANES
    return aligned, carried, rows


def _params(*sem):
    return pltpu.CompilerParams(dimension_semantics=sem, vmem_limit_bytes=VMEM_LIMIT)


def _in_proj_kernel(x_ref, g_ref, w_ref, o_ref):
    h = _rms(x_ref[...], g_ref[...]).astype(BF16)
    acc = jnp.dot(h, w_ref[...], preferred_element_type=F32)
    scale = 1.0 / math.sqrt(SB_HEAD_DIM)
    o_ref[:, :SB_WIDTH] = (acc[:, :SB_WIDTH] * scale).astype(BF16)
    o_ref[:, SB_WIDTH:] = acc[:, SB_WIDTH:].astype(BF16)


def _in_proj(x2d, g, w):
    n = x2d.shape[0]
    tm = PROJ_BLOCK
    return pl.pallas_call(
        _in_proj_kernel,
        grid=(n // tm,),
        in_specs=[
            pl.BlockSpec((tm, D_MODEL), lambda i: (i, 0)),
            pl.BlockSpec((1, D_MODEL), lambda i: (0, 0)),
            pl.BlockSpec((D_MODEL, IN_PROJ_COLS), lambda i: (0, 0)),
        ],
        out_specs=pl.BlockSpec((tm, IN_PROJ_COLS), lambda i: (i, 0)),
        out_shape=jax.ShapeDtypeStruct((n, IN_PROJ_COLS), BF16),
        compiler_params=_params("arbitrary"),
        name="in_proj",
    )(x2d, g, w)


def _sb_chain(qh, kb, vb, neg_tri, c, causal):
    z = lax.dot_general(qh, kb, (((1,), (1,)), ((), ())), preferred_element_type=F32)
    sp = jnp.maximum(z, 0.0) + jnp.log(1.0 + jnp.exp(-jnp.abs(z)))
    if causal is not None:
        sp = jnp.where(causal, sp, 0.0)
    after = jnp.dot(sp.astype(BF16), neg_tri, preferred_element_type=F32)
    w = jnp.exp((z - sp) + after)
    if causal is not None:
        w = jnp.where(causal, w, 0.0)
    pv = jnp.dot(w.astype(BF16), vb, preferred_element_type=F32) * jnp.exp(c)
    return pv, c + (after[:, :1] - sp[:, :1])


def _sb_kernel(q_ref, k_ref, v_ref, tri_ref, o_ref, acc_ref, c_ref):
    tq = SB_BLOCK
    lane = lax.broadcasted_iota(I32, (tq, LANES), 1)
    neg_tri = tri_ref[...]
    row = lax.broadcasted_iota(I32, (2 * tq, tq), 0)
    col = lax.broadcasted_iota(I32, (2 * tq, tq), 1)
    causal = col < jnp.where(row >= tq, row - tq, row)

    def stacked(u):
        q = q_ref[u * tq:(u + 1) * tq, :]
        zero = jnp.zeros_like(q)
        return jnp.concatenate(
            [jnp.where(lane < SB_HEAD_DIM, q, zero), jnp.where(lane >= SB_HEAD_DIM, q, zero)], axis=0)

    def step(qs, j, c, diagonal):
        off = pl.multiple_of(j * tq, tq)
        return _sb_chain(qs, k_ref[pl.ds(off, tq), :], v_ref[pl.ds(off, tq), :], neg_tri, c,
                         causal if diagonal else None)

    blocks = []
    for u in range(SB_QBLOCKS):
        i = pl.program_id(2) * SB_QBLOCKS + u
        qs = stacked(u)
        pv, c = step(qs, i, jnp.zeros((2 * tq, LANES), F32), True)
        pv_left, c_left = step(qs, jnp.maximum(i - 1, 0), c, False)
        has_left = i > 0
        acc_ref[u] = pv + jnp.where(has_left, pv_left, 0.0)
        c_ref[u] = jnp.where(has_left, c_left, c)
        blocks.append((i, qs))

    for u, (i, qs) in enumerate(blocks):
        def alive():
            return jnp.max(c_ref[u]) > SB_DEAD_LOG

        def cond(carry):
            j, live = carry
            return jnp.logical_and(j >= 0, live)

        def body(carry):
            j, _ = carry
            pv, c = step(qs, j, c_ref[u], False)
            acc_ref[u] += pv
            c_ref[u] = c
            return j - 1, alive()

        lax.while_loop(cond, body, (i - 2, alive()))
        o_ref[u * tq:(u + 1) * tq, :] = jnp.where(
            lane < SB_HEAD_DIM, acc_ref[u, :tq, :], acc_ref[u, tq:, :]).astype(o_ref.dtype)


def _sb_attention(proj3, neg_tri):
    b, s, _ = proj3.shape
    tq = SB_BLOCK * SB_QBLOCKS
    pairs = SB_WIDTH // LANES
    return pl.pallas_call(
        _sb_kernel,
        grid=(b, pairs, s // tq),
        in_specs=[
            pl.BlockSpec((None, tq, LANES), lambda bi, p, i: (bi, i, p)),
            pl.BlockSpec((None, s, LANES), lambda bi, p, i: (bi, 0, pairs + p)),
            pl.BlockSpec((None, s, LANES), lambda bi, p, i: (bi, 0, 2 * pairs + p)),
            pl.BlockSpec((SB_BLOCK, SB_BLOCK), lambda bi, p, i: (0, 0)),
        ],
        out_specs=pl.BlockSpec((None, tq, LANES), lambda bi, p, i: (bi, i, p)),
        out_shape=jax.ShapeDtypeStruct((b, s, SB_WIDTH), BF16),
        scratch_shapes=[pltpu.VMEM((SB_QBLOCKS, 2 * SB_BLOCK, LANES), F32),
                        pltpu.VMEM((SB_QBLOCKS, 2 * SB_BLOCK, LANES), F32)],
        compiler_params=_params("arbitrary", "arbitrary", "arbitrary"),
        name="sb_attn",
    )(proj3, proj3, proj3, neg_tri)


def _mix_kernel(x_ref, sb_ref, u_ref, halo_ref, wp_ref, ps_ref, gsb_ref, gpool_ref, wout_ref,
                o_ref, *, seq):
    tm = x_ref.shape[0]
    t0 = (pl.program_id(0) * tm) % seq
    u = u_ref[...].astype(F32)
    halo = jnp.where(t0 == 0, 0.0, halo_ref[...].astype(F32))
    a = jnp.concatenate([halo, u], axis=0)
    pos = t0 + lax.broadcasted_iota(I32, (tm, POOL_GROUP_DIM), 0)
    outs = []
    for g, win in enumerate(POOL_WINDOWS):
        lo, hi = g * POOL_GROUP_DIM, (g + 1) * POOL_GROUP_DIM
        s = a[:, lo:hi]
        sh = 1
        while sh < win:
            s = s + pltpu.roll(s, sh, axis=0)
            sh *= 2
        cnt = jnp.minimum(pos + 1, win).astype(F32)
        outs.append(s[POOL_HALO:, :] / cnt - u[:, lo:hi])
    pooled = jnp.concatenate(outs, axis=1).astype(BF16)
    pooled = jnp.dot(pooled, wp_ref[...], preferred_element_type=F32) * ps_ref[...]
    pool_n = _rms(pooled, gpool_ref[...]).astype(BF16)
    sb_n = _rms(sb_ref[...].astype(F32), gsb_ref[...]).astype(BF16)
    y = jnp.dot(sb_n, wout_ref[:SB_WIDTH, :], preferred_element_type=F32)
    y = y + jnp.dot(pool_n, wout_ref[SB_WIDTH:, :], preferred_element_type=F32)
    o_ref[...] = x_ref[...] + y


def _mix_out(x2d, sb2d, proj2d, wp, ps, gsb, gpool, wout, seq):
    n = x2d.shape[0]
    tm = PROJ_BLOCK
    ucol = 3 * SB_WIDTH // POOL_WIDTH
    hb = tm // POOL_HALO
    return pl.pallas_call(
        functools.partial(_mix_kernel, seq=seq),
        grid=(n // tm,),
        in_specs=[
            pl.BlockSpec((tm, D_MODEL), lambda i: (i, 0)),
            pl.BlockSpec((tm, SB_WIDTH), lambda i: (i, 0)),
            pl.BlockSpec((tm, POOL_WIDTH), lambda i: (i, ucol)),
            pl.BlockSpec((POOL_HALO, POOL_WIDTH), lambda i: (jnp.maximum(i * hb - 1, 0), ucol)),
            pl.BlockSpec((POOL_WIDTH, POOL_WIDTH), lambda i: (0, 0)),
            pl.BlockSpec((1, POOL_WIDTH), lambda i: (0, 0)),
            pl.BlockSpec((1, SB_WIDTH), lambda i: (0, 0)),
            pl.BlockSpec((1, POOL_WIDTH), lambda i: (0, 0)),
            pl.BlockSpec((D_MODEL, D_MODEL), lambda i: (0, 0)),
        ],
        out_specs=pl.BlockSpec((tm, D_MODEL), lambda i: (i, 0)),
        out_shape=jax.ShapeDtypeStruct((n, D_MODEL), F32),
        compiler_params=_params("arbitrary"),
        name="mix_out",
    )(x2d, sb2d, proj2d, proj2d, wp, ps, gsb, gpool, wout)


def _mem_kv_kernel(m_ref, g_ref, w_ref, o_ref):
    h = _rms(m_ref[...], g_ref[...]).astype(BF16)
    o_ref[...] = jnp.dot(h, w_ref[...], preferred_element_type=F32).astype(BF16)


def _mem_kv(mem, g, w):
    b = mem.shape[0]
    return pl.pallas_call(
        _mem_kv_kernel,
        grid=(b,),
        in_specs=[
            pl.BlockSpec((None, N_MEM, D_MODEL), lambda i: (i, 0, 0)),
            pl.BlockSpec((1, D_MODEL), lambda i: (0, 0)),
            pl.BlockSpec((D_MODEL, 2 * D_MODEL), lambda i: (0, 0)),
        ],
        out_specs=pl.BlockSpec((None, N_MEM, 2 * D_MODEL), lambda i: (i, 0, 0)),
        out_shape=jax.ShapeDtypeStruct((b, N_MEM, 2 * D_MODEL), BF16),
        compiler_params=_params("arbitrary"),
        name="mem_kv",
    )(mem, g, w)


def _xattn_block(x, kv_ref, gq, wq_ref, wo_ref, gf, wr_ref, br, tri):
    tm = x.shape[0]
    hq = _rms(x, gq).astype(BF16)
    q = jnp.dot(hq, wq_ref[...], preferred_element_type=F32) * (1.0 / math.sqrt(MEM_HEAD_DIM))
    q = q.astype(BF16)
    outs = []
    for h in range(MEM_HEADS):
        lo, hi = h * MEM_HEAD_DIM, (h + 1) * MEM_HEAD_DIM
        s = lax.dot_general(q[:, lo:hi], kv_ref[:, lo:hi], (((1,), (1,)), ((), ())),
                            preferred_element_type=F32)
        p = jnp.exp(s - jnp.max(s, axis=-1, keepdims=True))
        denom = jnp.sum(p, axis=-1, keepdims=True)
        o = jnp.dot(p.astype(BF16), kv_ref[:, D_MODEL + lo:D_MODEL + hi], preferred_element_type=F32)
        outs.append(o / denom)
    o = jnp.concatenate(outs, axis=1).astype(BF16)
    x2 = x + jnp.dot(o, wo_ref[...], preferred_element_type=F32)
    hf = _rms(x2, gf)

    h_hi = hf.astype(BF16)
    h_lo = (hf - h_hi.astype(F32)).astype(BF16)
    hw = jnp.dot(h_hi, wr_ref[...], preferred_element_type=F32)
    lw = jnp.dot(h_lo, wr_ref[:, :LANES], preferred_element_type=F32)
    logits = (hw[:, :LANES] + (hw[:, LANES:] + lw)) + br

    lane = lax.broadcasted_iota(I32, (tm, LANES), 1).astype(F32)
    work = logits
    vals, idxs, sels = [], [], []
    for _ in range(TOP_K):
        m = jnp.max(work, axis=-1, keepdims=True)
        idx = jnp.min(jnp.where(work == m, lane, float(LANES)), axis=-1, keepdims=True)
        sel = lane == idx
        vals.append(m)
        idxs.append(idx)
        sels.append(sel)
        work = jnp.where(sel, -3e38, work)
    exps = [jnp.exp(v - vals[0]) for v in vals]
    den = exps[0] + exps[1] + exps[2] + exps[3]

    onehot = jnp.zeros((tm, LANES), F32)
    for sel in sels:
        onehot = onehot + sel.astype(F32)
    blocks = [onehot[r:r + TOK_BLOCK, :] for r in range(0, tm, TOK_BLOCK)]
    before = jnp.concatenate(
        [jnp.dot(tri, blk.astype(BF16), preferred_element_type=F32) for blk in blocks], axis=0)
    gates = jnp.zeros((tm, LANES), F32)
    for k in range(TOP_K):
        gates = jnp.where(lane == float(k), exps[k] / den, gates)
    counts = [jnp.sum(blk, axis=0, keepdims=True) for blk in blocks]
    return x2, hf.astype(BF16), gates, (idxs, sels, before, counts)


def _sorted_positions(routing, base, lane):
    idxs, sels, before, counts = routing
    upper = (lax.broadcasted_iota(I32, (LANES, LANES), 0)
             < lax.broadcasted_iota(I32, (LANES, LANES), 1)).astype(BF16)
    offsets = []
    for cnt in counts:
        _, carried, rows = _segment_plan(base, cnt)
        rows = jnp.where(lane[:1, :] < float(N_EXPERTS), rows, 0.0)
        start = jnp.dot(jnp.broadcast_to(rows, (SUBLANES, LANES)).astype(BF16), upper,
                        preferred_element_type=F32)[:1, :]
        offsets.append(jnp.broadcast_to(start + carried, (TOK_BLOCK, LANES)))
        base = base + cnt
    first_row = before + jnp.concatenate(offsets, axis=0)
    meta = jnp.zeros(before.shape, F32)
    for k in range(TOP_K):
        pos = jnp.sum(jnp.where(sels[k], first_row, 0.0), axis=-1, keepdims=True)
        meta = jnp.where(lane == float(k), idxs[k], meta)
        meta = jnp.where(lane == float(TOP_K + k), pos, meta)
    return meta.astype(I32), base


def _xattn_kernel(x_ref, kv_ref, gq_ref, wq_ref, wo_ref, gf_ref, wr_ref, br_ref, tri_ref,
                  x2_ref, h_ref, meta_ref, meta_t_ref, gate_ref, cnt_ref, base_ref):
    @pl.when(pl.program_id(0) == 0)
    def _():
        base_ref[...] = jnp.zeros_like(base_ref)

    tb = TOK_BLOCK * XATTN_CHAIN
    chains = []
    for chain in range(XATTN_BLOCKS // XATTN_CHAIN):
        rows = slice(chain * tb, (chain + 1) * tb)
        x2, hf, gates, routing = _xattn_block(
            x_ref[rows, :], kv_ref, gq_ref[...], wq_ref, wo_ref, gf_ref[...], wr_ref, br_ref[...],
            tri_ref[...])
        x2_ref[rows, :] = x2
        h_ref[rows, :] = hf
        gate_ref[rows, :] = gates
        chains.append((rows, routing))

    lane = lax.broadcasted_iota(I32, (tb, LANES), 1).astype(F32)
    base = base_ref[...]
    for chain, (rows, routing) in enumerate(chains):
        meta, base = _sorted_positions(routing, base, lane)
        meta_ref[rows, :] = meta
        meta_t_ref[:, rows] = meta.T[:SUBLANES, :]
        for r, c in enumerate(routing[3]):
            cnt_ref[chain * XATTN_CHAIN + r] = c.astype(I32)
    base_ref[...] = base


def _xattn_router(x1, kv, gq, wq, wo, gf, wr, br, tri, seq):
    n = x1.shape[0]
    tm = TOK_BLOCK * XATTN_BLOCKS
    per_seq = seq // tm
    const = lambda i: (0, 0)
    return pl.pallas_call(
        _xattn_kernel,
        grid=(n // tm,),
        in_specs=[
            pl.BlockSpec((tm, D_MODEL), lambda i: (i, 0)),
            pl.BlockSpec((None, N_MEM, 2 * D_MODEL), lambda i: (i // per_seq, 0, 0)),
            pl.BlockSpec((1, D_MODEL), const),
            pl.BlockSpec((D_MODEL, D_MODEL), const),
            pl.BlockSpec((D_MODEL, D_MODEL), const),
            pl.BlockSpec((1, D_MODEL), const),
            pl.BlockSpec((D_MODEL, 2 * LANES), const),
            pl.BlockSpec((1, LANES), const),
            pl.BlockSpec((TOK_BLOCK, TOK_BLOCK), const),
        ],
        out_specs=[
            pl.BlockSpec((tm, D_MODEL), lambda i: (i, 0)),
            pl.BlockSpec((tm, D_MODEL), lambda i: (i, 0)),
            pl.BlockSpec((tm, LANES), lambda i: (i, 0)),
            pl.BlockSpec((SUBLANES, tm), lambda i: (0, i)),
            pl.BlockSpec((tm, LANES), lambda i: (i, 0)),
            pl.BlockSpec((XATTN_BLOCKS, 1, LANES), lambda i: (i, 0, 0)),
        ],
        out_shape=[
            jax.ShapeDtypeStruct((n, D_MODEL), F32),
            jax.ShapeDtypeStruct((n, D_MODEL), BF16),
            jax.ShapeDtypeStruct((n, LANES), I32),
            jax.ShapeDtypeStruct((SUBLANES, n), I32),
            jax.ShapeDtypeStruct((n, LANES), F32),
            jax.ShapeDtypeStruct((n // TOK_BLOCK, 1, LANES), I32),
        ],
        scratch_shapes=[pltpu.VMEM((1, LANES), F32)],
        compiler_params=_params("arbitrary"),
        name="xattn_router",
    )(x1, kv, gq, wq, wo, gf, wr, br, tri)


def _segment_copy(tables, step, e, vmem_ref, hbm_ref, sem, to_hbm):
    src_ref, dst_ref, cnt_ref = tables
    at = step * N_EXPERTS + e
    n = pl.multiple_of(cnt_ref[at], SUBLANES)
    local = vmem_ref.at[pl.ds(pl.multiple_of(src_ref[at], SUBLANES), n), :]
    remote = hbm_ref.at[pl.ds(pl.multiple_of(dst_ref[at], SUBLANES), n), :]
    return pltpu.make_async_copy(local, remote, sem) if to_hbm else pltpu.make_async_copy(remote, local, sem)


def _each_segment(fn):
    for r in range(SORT_BLOCKS):
        for e in range(N_EXPERTS):
            fn(r, e)


def _dispatch_kernel(src_ref, dst_ref, cnt_ref, carry_ref, fill_ref, h_ref, pos_ref, xs_ref,
                     stage_ref, zero_ref, sem, fill_sem):
    tables = (src_ref, dst_ref, cnt_ref)
    step = pl.program_id(0)
    last = pl.num_programs(0) - 1
    parity = step % 2

    def dump_copy():
        at = pl.multiple_of(fill_ref[2 * N_EXPERTS + 2], ROW_BLOCK)
        return pltpu.make_async_copy(zero_ref, xs_ref.at[pl.ds(at, ROW_BLOCK), :], fill_sem)

    @pl.when(step == 0)
    def _():
        stage_ref[...] = jnp.zeros_like(stage_ref)
        zero_ref[...] = jnp.zeros_like(zero_ref)
        dump_copy().start()
        dump_copy().wait()

    def copy(at_step, r, e):
        slot = (at_step % 2) * SORT_BLOCKS + r
        return _segment_copy(tables, at_step * SORT_BLOCKS + r, e, stage_ref.at[slot], xs_ref,
                             sem.at[r], True)

    where = lax.broadcasted_iota(I32, (SORT_ROWS, TOK_BLOCK), 0)
    for r in range(SORT_BLOCKS):
        rows = slice(r * TOK_BLOCK, (r + 1) * TOK_BLOCK)
        pos = pos_ref[:, rows]
        hit = where == pos[TOP_K:TOP_K + 1, :]
        for k in range(1, TOP_K):
            hit = jnp.logical_or(hit, where == pos[TOP_K + k:TOP_K + k + 1, :])
        onehot = jnp.where(hit, 1.0, 0.0).astype(BF16)
        slot = parity * SORT_BLOCKS + r
        stage_ref[slot] = jnp.dot(onehot, h_ref[rows, :], preferred_element_type=F32)
        before = (1 - parity) * SORT_BLOCKS + SORT_BLOCKS - 1 if r == 0 else slot - 1
        for e in range(N_EXPERTS):
            at = (step * SORT_BLOCKS + r) * N_EXPERTS + e
            src = carry_ref[at]
            tile = stage_ref[before, pl.ds(pl.multiple_of(jnp.maximum(src, 0), SUBLANES), SUBLANES), :]
            head = pl.ds(pl.multiple_of(src_ref[at], SUBLANES), SUBLANES)
            stage_ref[slot, head, :] += jnp.where(src >= 0, tile, 0.0)

        if r > 0:
            for e in range(N_EXPERTS):
                copy(step, r - 1, e).wait()
        else:
            @pl.when(step > 0)
            def _():
                for e in range(N_EXPERTS):
                    copy(step - 1, SORT_BLOCKS - 1, e).wait()

        for e in range(N_EXPERTS):
            copy(step, r, e).start()

    @pl.when(step == last)
    def _():
        for e in range(N_EXPERTS):
            copy(step, SORT_BLOCKS - 1, e).wait()

    def pad_copy(e):
        n = pl.multiple_of(fill_ref[N_EXPERTS + e], SUBLANES)
        at = pl.multiple_of(fill_ref[e], SUBLANES)
        return pltpu.make_async_copy(zero_ref.at[pl.ds(0, n), :], xs_ref.at[pl.ds(at, n), :], fill_sem)

    def tail_copy(t):
        at = pl.multiple_of(fill_ref[2 * N_EXPERTS] + t * ROW_BLOCK, ROW_BLOCK)
        return pltpu.make_async_copy(zero_ref, xs_ref.at[pl.ds(at, ROW_BLOCK), :], fill_sem)

    def each_pad(fn):
        def body(e, carry):
            @pl.when(fill_ref[N_EXPERTS + e] > 0)
            def _():
                fn(e)
            return carry
        lax.fori_loop(0, N_EXPERTS, body, 0)

    def each_tail(fn):
        def body(t, carry):
            fn(t)
            return carry
        lax.fori_loop(0, fill_ref[2 * N_EXPERTS + 1], body, 0)

    @pl.when(step == 0)
    def _():
        each_pad(lambda e: pad_copy(e).start())
        each_tail(lambda t: tail_copy(t).start())

    @pl.when(step == last)
    def _():
        each_pad(lambda e: pad_copy(e).wait())
        each_tail(lambda t: tail_copy(t).wait())


def _dispatch(tables, carry, fill, hf, pos, n_rows):
    n = hf.shape[0]
    tm = TOK_BLOCK * SORT_BLOCKS
    grid_spec = pltpu.PrefetchScalarGridSpec(
        num_scalar_prefetch=5,
        grid=(n // tm,),
        in_specs=[
            pl.BlockSpec((tm, D_MODEL), lambda i, *_: (i, 0)),
            pl.BlockSpec((SUBLANES, tm), lambda i, *_: (0, i)),
        ],
        out_specs=pl.BlockSpec(memory_space=pl.ANY),
        scratch_shapes=[pltpu.VMEM((2 * SORT_BLOCKS, SORT_ROWS, D_MODEL), F32),
                        pltpu.VMEM((ROW_BLOCK, D_MODEL), F32),
                        pltpu.SemaphoreType.DMA((2,)), pltpu.SemaphoreType.DMA],
    )
    return pl.pallas_call(
        _dispatch_kernel,
        grid_spec=grid_spec,
        out_shape=jax.ShapeDtypeStruct((n_rows, D_MODEL), F32),
        compiler_params=_params("arbitrary"),
        name="dispatch",
    )(*tables, carry, fill, hf, pos)


def _expert_kernel(blk_e_ref, next_e_ref, n_used_ref, xs_ref, w1_hbm, b1_ref, w2_hbm, b2_ref,
                   ys_ref, w1f_ref, w2f_ref, w1b_ref, w2b_ref, sem):
    i = pl.program_id(0)
    used = i < n_used_ref[0]

    def fetch(e):
        return (pltpu.make_async_copy(w1_hbm.at[e], w1f_ref, sem.at[0]),
                pltpu.make_async_copy(w2_hbm.at[e], w2f_ref, sem.at[1]))

    @pl.when(used)
    def _():
        e = blk_e_ref[i]

        @pl.when(i == 0)
        def _():
            for copy in fetch(e):
                copy.start()

        @pl.when(jnp.logical_or(i == 0, e != blk_e_ref[jnp.maximum(i - 1, 0)]))
        def _():
            for copy in fetch(e):
                copy.wait()
            w1b_ref[...] = w1f_ref[...].astype(BF16)
            w2b_ref[...] = w2f_ref[...].astype(BF16)

            @pl.when(next_e_ref[i] >= 0)
            def _():
                for copy in fetch(next_e_ref[i]):
                    copy.start()

        gu = jnp.dot(xs_ref[...].astype(BF16), w1b_ref[...], preferred_element_type=F32) + b1_ref[...]
        gate = jnp.minimum(gu[:, :D_FF], SWIGLU_LIMIT)
        up = jnp.clip(gu[:, D_FF:], -SWIGLU_LIMIT, SWIGLU_LIMIT)
        act = (up + 1.0) * gate * jax.nn.sigmoid(SWIGLU_ALPHA * gate)
        ys_ref[...] = jnp.dot(act.astype(BF16), w2b_ref[...], preferred_element_type=F32) + b2_ref[...]


def _experts(blk_e, next_e, n_used, xs, w1, b1, w2, b2):
    n_rows = xs.shape[0]
    bm = ROW_BLOCK
    last_used = lambda i, nu: jnp.maximum(jnp.minimum(i, nu[0] - 1), 0)
    row = lambda i, be, ne, nu: (last_used(i, nu), 0)
    exp3 = lambda i, be, ne, nu: (be[last_used(i, nu)], 0, 0)
    grid_spec = pltpu.PrefetchScalarGridSpec(
        num_scalar_prefetch=3,
        grid=(n_rows // bm,),
        in_specs=[
            pl.BlockSpec((bm, D_MODEL), row),
            pl.BlockSpec(memory_space=pl.ANY),
            pl.BlockSpec((None, 1, 2 * D_FF), exp3),
            pl.BlockSpec(memory_space=pl.ANY),
            pl.BlockSpec((None, 1, D_MODEL), exp3),
        ],
        out_specs=pl.BlockSpec((bm, D_MODEL), row),
        scratch_shapes=[pltpu.VMEM((D_MODEL, 2 * D_FF), F32), pltpu.VMEM((D_FF, D_MODEL), F32),
                        pltpu.VMEM((D_MODEL, 2 * D_FF), BF16), pltpu.VMEM((D_FF, D_MODEL), BF16),
                        pltpu.SemaphoreType.DMA((2,))],
    )
    return pl.pallas_call(
        _expert_kernel,
        grid_spec=grid_spec,
        out_shape=jax.ShapeDtypeStruct((n_rows, D_MODEL), F32),
        input_output_aliases={3: 0},
        compiler_params=_params("arbitrary"),
        name="experts",
    )(blk_e, next_e, n_used, xs, w1, b1, w2, b2)


def _combine_kernel(src_ref, dst_ref, cnt_ref, x_ref, gate_ref, pos_ref, g_ref, ys_ref, o_ref,
                    stage_ref, sem):
    tables = (src_ref, dst_ref, cnt_ref)
    step = pl.program_id(0)
    last = pl.num_programs(0) - 1
    parity = step % 2

    def copy(at_step, at_parity, r, e):
        return _segment_copy(tables, at_step * SORT_BLOCKS + r, e,
                             stage_ref.at[at_parity * SORT_BLOCKS + r], ys_ref, sem.at[at_parity], False)

    @pl.when(step == 0)
    def _():
        stage_ref[...] = jnp.zeros_like(stage_ref)
        _each_segment(lambda r, e: copy(step, parity, r, e).start())

    @pl.when(step < last)
    def _():
        _each_segment(lambda r, e: copy(step + 1, 1 - parity, r, e).start())

    _each_segment(lambda r, e: copy(step, parity, r, e).wait())

    where = lax.broadcasted_iota(I32, (TOK_BLOCK, SORT_ROWS), 1)
    for r in range(SORT_BLOCKS):
        rows = slice(r * TOK_BLOCK, (r + 1) * TOK_BLOCK)
        pos = pos_ref[rows, :]
        gates = gate_ref[rows, :]
        weights = jnp.zeros((TOK_BLOCK, SORT_ROWS), F32)
        for k in range(TOP_K):
            weights = jnp.where(where == pos[:, TOP_K + k:TOP_K + k + 1], gates[:, k:k + 1], weights)
        y = x_ref[rows, :] + jnp.dot(weights.astype(BF16),
                                     stage_ref[parity * SORT_BLOCKS + r].astype(BF16),
                                     preferred_element_type=F32)
        o_ref[rows, :] = _rms(y, g_ref[...])


def _combine(tables, x2, gates, pos, g_final, ys):
    n = x2.shape[0]
    tc = TOK_BLOCK * SORT_BLOCKS
    grid_spec = pltpu.PrefetchScalarGridSpec(
        num_scalar_prefetch=3,
        grid=(n // tc,),
        in_specs=[
            pl.BlockSpec((tc, D_MODEL), lambda i, *_: (i, 0)),
            pl.BlockSpec((tc, LANES), lambda i, *_: (i, 0)),
            pl.BlockSpec((tc, LANES), lambda i, *_: (i, 0)),
            pl.BlockSpec((1, D_MODEL), lambda i, *_: (0, 0)),
            pl.BlockSpec(memory_space=pl.ANY),
        ],
        out_specs=pl.BlockSpec((tc, D_MODEL), lambda i, *_: (i, 0)),
        scratch_shapes=[pltpu.VMEM((2 * SORT_BLOCKS, SORT_ROWS, D_MODEL), F32),
                        pltpu.SemaphoreType.DMA((2,))],
    )
    return pl.pallas_call(
        _combine_kernel,
        grid_spec=grid_spec,
        out_shape=jax.ShapeDtypeStruct((n, D_MODEL), F32),
        compiler_params=_params("arbitrary"),
        name="combine",
    )(*tables, x2, gates, pos, g_final, ys)


def _strict_lower(n):
    r = lax.broadcasted_iota(I32, (n, n), 0)
    c = lax.broadcasted_iota(I32, (n, n), 1)
    return (r > c).astype(BF16)


def _round_up(v, m):
    return (v + m - 1) // m * m


def _routing_tables(cnt, n):
    bm = ROW_BLOCK
    n_blocks = n // TOK_BLOCK
    n_rows = _round_up(n * TOP_K, bm) + (N_EXPERTS + 1) * bm
    n_blk = n_rows // bm
    park = n_rows - bm
    cnt = cnt[:, 0, :N_EXPERTS]
    before = jnp.cumsum(cnt, axis=0) - cnt
    aligned, carried, seg = (v.astype(I32) for v in _segment_plan(before, cnt))
    local_start = jnp.cumsum(seg, axis=1) - seg
    total = jnp.sum(cnt, axis=0)
    padded = _round_up(jnp.maximum(total, 1), bm)
    pend = jnp.cumsum(padded)
    pstart = pend - padded
    ids = jnp.arange(N_EXPERTS, dtype=I32)
    blocks = jnp.arange(n_blocks, dtype=I32)
    parked = park + ((blocks[:, None] % SORT_BLOCKS) * N_EXPERTS + ids[None, :]) * SUBLANES
    global_start = jnp.where(carried + cnt > 0, pstart[None, :] + aligned, parked)
    tables = (local_start.reshape(-1), global_start.reshape(-1), seg.reshape(-1))
    prev_start = jnp.roll(local_start, 1, axis=0)
    prev_aligned = jnp.roll(aligned, 1, axis=0)
    carry = jnp.where(carried > 0, prev_start + aligned - prev_aligned, -1).reshape(-1)

    n_used = (pend[-1:] // bm).astype(I32)
    blk_row = jnp.arange(n_blk, dtype=I32) * bm
    blk_e = jnp.minimum(jnp.sum(blk_row[:, None] >= pend[None, :], axis=1), N_EXPERTS - 1).astype(I32)
    later = jnp.where((ids[None, :] > ids[:, None]) & (padded[None, :] > 0), ids[None, :], N_EXPERTS)
    follow = jnp.min(later, axis=1)
    next_e = jnp.where(follow < N_EXPERTS, follow, -1)[blk_e].astype(I32)
    used = _round_up(total, SUBLANES)
    fill = jnp.concatenate([pstart + used, padded - used, pend[-1:], (park - pend[-1:]) // bm,
                            jnp.full((1,), park, I32)])
    return tables, carry.astype(I32), fill.astype(I32), blk_e, next_e, n_used, n_rows


def _layer(x2d, mem, b, s, g_mix, w_in, g_sb_out, g_pool_out, w_pool, pool_scale, w_out,
           g_mem_q, g_mem_kv, w_mem_q, w_mem_kv, w_mem_o, g_ffn, w_router, b_router,
           w_e_in, b_e_in, w_e_out, b_e_out, g_out):
    n = b * s
    row = lambda v: v.reshape(1, -1).astype(F32)

    proj = _in_proj(x2d, row(g_mix), w_in.astype(BF16))
    sb = _sb_attention(proj.reshape(b, s, IN_PROJ_COLS), -_strict_lower(SB_BLOCK))
    wp = jnp.zeros((POOL_WIDTH, POOL_WIDTH), BF16)
    for g in range(len(POOL_WINDOWS)):
        at = g * POOL_GROUP_DIM
        wp = wp.at[at:at + POOL_GROUP_DIM, at:at + POOL_GROUP_DIM].set(w_pool[g].astype(BF16))
    x1 = _mix_out(x2d, sb.reshape(n, SB_WIDTH), proj, wp, row(pool_scale),
                  row(g_sb_out), row(g_pool_out), w_out.astype(BF16), s)

    kv = _mem_kv(mem, row(g_mem_kv), w_mem_kv.astype(BF16))
    wr = jnp.zeros((D_MODEL, LANES), F32).at[:, :N_EXPERTS].set(w_router)
    wr_hi = wr.astype(BF16)
    wr_lo = (wr - wr_hi.astype(F32)).astype(BF16)
    br = jnp.full((1, LANES), NEG_BIG, F32).at[0, :N_EXPERTS].set(b_router)
    x2, hf, meta, meta_t, gates, cnt = _xattn_router(
        x1, kv, row(g_mem_q), w_mem_q.astype(BF16), w_mem_o.astype(BF16), row(g_ffn),
        jnp.concatenate([wr_hi, wr_lo], axis=1), br, _strict_lower(TOK_BLOCK), s)

    tables, carry, fill, blk_e, next_e, n_used, n_rows = _routing_tables(cnt, n)
    xs = _dispatch(tables, carry, fill, hf, meta_t, n_rows)
    ys = _experts(blk_e, next_e, n_used, xs, w_e_in, b_e_in.reshape(N_EXPERTS, 1, -1),
                  w_e_out, b_e_out.reshape(N_EXPERTS, 1, -1))
    return _combine(tables, x2, gates, meta, row(g_out), ys)


def kernel(x, mem, g_mix, w_in, g_sb_out, g_pool_out, w_pool, pool_scale, w_out, g_mem_q, g_mem_kv,
           w_mem_q, w_mem_kv, w_mem_o, g_ffn, w_router, b_router, w_expert_in, b_expert_in,
           w_expert_out, b_expert_out, g_final):
    b, s, d = x.shape
    depth = g_mix.shape[0]
    assert d == D_MODEL and depth == 1, "the final RMSNorm is fused into the single layer's combine"
    assert s % PROJ_BLOCK == 0 and s % (SB_BLOCK * SB_QBLOCKS) == 0 and s % (TOK_BLOCK * XATTN_BLOCKS) == 0
    assert (b * s) % (TOK_BLOCK * SORT_BLOCKS) == 0
    out = _layer(x.reshape(b * s, d), mem, b, s, g_mix[0], w_in[0], g_sb_out[0], g_pool_out[0],
                 w_pool[0], pool_scale[0], w_out[0], g_mem_q[0], g_mem_kv[0], w_mem_q[0],
                 w_mem_kv[0], w_mem_o[0], g_ffn[0], w_router[0], b_router[0], w_expert_in[0],
                 b_expert_in[0], w_expert_out[0], b_expert_out[0], g_final)
    return out.reshape(b, s, d)
```

```python
import functools
import math

import jax
import jax.numpy as jnp
from jax import lax
from jax.experimental import pallas as pl
from jax.experimental.pallas import tpu as pltpu

F32 = jnp.float32
BF16 = jnp.bfloat16
I32 = jnp.int32

D_MODEL = 1024
SB_HEADS = 8
SB_HEAD_DIM = 64
SB_WIDTH = SB_HEADS * SB_HEAD_DIM
POOL_WINDOWS = (2, 4, 8, 16)
POOL_GROUP_DIM = 128
POOL_WIDTH = len(POOL_WINDOWS) * POOL_GROUP_DIM
IN_PROJ_COLS = 3 * SB_WIDTH + POOL_WIDTH
N_MEM = 256
MEM_HEADS = 4
MEM_HEAD_DIM = D_MODEL // MEM_HEADS
N_EXPERTS = 32
TOP_K = 4
D_FF = D_MODEL
SWIGLU_LIMIT = 7.0
SWIGLU_ALPHA = 1.702
RMS_EPS = 1e-5

LANES = 128
SUBLANES = 8
POOL_HALO = 16
SB_BLOCK = 256
SB_QBLOCKS = 4
SB_DEAD_LOG = -110.0
ROW_BLOCK = 512
TOK_BLOCK = 256
XATTN_BLOCKS = 4
XATTN_CHAIN = 2
SORT_BLOCKS = 2
PROJ_BLOCK = 1024
SORT_ROWS = -(-(TOK_BLOCK * TOP_K + N_EXPERTS * SUBLANES) // LANES) * LANES
VMEM_LIMIT = 48 * 1024 * 1024
NEG_BIG = -1e30


def _rms(x, g):
    ms = jnp.mean(x * x, axis=-1, keepdims=True)
    return x * lax.rsqrt(ms + RMS_EPS) * g


def _segment_rows(count):
    return jnp.ceil(jnp.maximum(count, 1) / SUBLANES) * SUBLANES


def _params(*sem):
    return pltpu.CompilerParams(dimension_semantics=sem, vmem_limit_bytes=VMEM_LIMIT)


def _in_proj_kernel(x_ref, g_ref, w_ref, o_ref):
    h = _rms(x_ref[...], g_ref[...]).astype(BF16)
    acc = jnp.dot(h, w_ref[...], preferred_element_type=F32)
    scale = 1.0 / math.sqrt(SB_HEAD_DIM)
    o_ref[:, :SB_WIDTH] = (acc[:, :SB_WIDTH] * scale).astype(BF16)
    o_ref[:, SB_WIDTH:] = acc[:, SB_WIDTH:].astype(BF16)


def _in_proj(x2d, g, w):
    n = x2d.shape[0]
    tm = PROJ_BLOCK
    return pl.pallas_call(
        _in_proj_kernel,
        grid=(n // tm,),
        in_specs=[
            pl.BlockSpec((tm, D_MODEL), lambda i: (i, 0)),
            pl.BlockSpec((1, D_MODEL), lambda i: (0, 0)),
            pl.BlockSpec((D_MODEL, IN_PROJ_COLS), lambda i: (0, 0)),
        ],
        out_specs=pl.BlockSpec((tm, IN_PROJ_COLS), lambda i: (i, 0)),
        out_shape=jax.ShapeDtypeStruct((n, IN_PROJ_COLS), BF16),
        compiler_params=_params("arbitrary"),
        name="in_proj",
    )(x2d, g, w)


def _sb_chain(qh, kb, vb, neg_tri, c, causal):
    z = lax.dot_general(qh, kb, (((1,), (1,)), ((), ())), preferred_element_type=F32)
    sp = jnp.maximum(z, 0.0) + jnp.log(1.0 + jnp.exp(-jnp.abs(z)))
    if causal is not None:
        sp = jnp.where(causal, sp, 0.0)
    after = jnp.dot(sp.astype(BF16), neg_tri, preferred_element_type=F32)
    w = jnp.exp((z - sp) + after)
    if causal is not None:
        w = jnp.where(causal, w, 0.0)
    pv = jnp.dot(w.astype(BF16), vb, preferred_element_type=F32) * jnp.exp(c)
    return pv, c + (after[:, :1] - sp[:, :1])


def _sb_kernel(q_ref, k_ref, v_ref, tri_ref, o_ref, acc_ref, c_ref):
    tq = SB_BLOCK
    lane = lax.broadcasted_iota(I32, (tq, LANES), 1)
    neg_tri = tri_ref[...]
    row = lax.broadcasted_iota(I32, (2 * tq, tq), 0)
    col = lax.broadcasted_iota(I32, (2 * tq, tq), 1)
    causal = col < jnp.where(row >= tq, row - tq, row)

    def stacked(u):
        q = q_ref[u * tq:(u + 1) * tq, :]
        zero = jnp.zeros_like(q)
        return jnp.concatenate(
            [jnp.where(lane < SB_HEAD_DIM, q, zero), jnp.where(lane >= SB_HEAD_DIM, q, zero)], axis=0)

    def step(qs, j, c, diagonal):
        off = pl.multiple_of(j * tq, tq)
        return _sb_chain(qs, k_ref[pl.ds(off, tq), :], v_ref[pl.ds(off, tq), :], neg_tri, c,
                         causal if diagonal else None)

    blocks = []
    for u in range(SB_QBLOCKS):
        i = pl.program_id(2) * SB_QBLOCKS + u
        qs = stacked(u)
        pv, c = step(qs, i, jnp.zeros((2 * tq, LANES), F32), True)
        pv_left, c_left = step(qs, jnp.maximum(i - 1, 0), c, False)
        has_left = i > 0
        acc_ref[u] = pv + jnp.where(has_left, pv_left, 0.0)
        c_ref[u] = jnp.where(has_left, c_left, c)
        blocks.append((i, qs))

    @pl.when(jnp.max(c_ref[...]) > SB_DEAD_LOG)
    def _():
        for u, (i, qs) in enumerate(blocks):
            def alive():
                return jnp.max(c_ref[u]) > SB_DEAD_LOG

            def cond(carry):
                j, live = carry
                return jnp.logical_and(j >= 0, live)

            def body(carry):
                j, _ = carry
                pv, c = step(qs, j, c_ref[u], False)
                acc_ref[u] += pv
                c_ref[u] = c
                return j - 1, alive()

            lax.while_loop(cond, body, (i - 2, alive()))

    for u in range(SB_QBLOCKS):
        o_ref[u * tq:(u + 1) * tq, :] = jnp.where(
            lane < SB_HEAD_DIM, acc_ref[u, :tq, :], acc_ref[u, tq:, :]).astype(o_ref.dtype)


def _sb_attention(proj3, neg_tri):
    b, s, _ = proj3.shape
    tq = SB_BLOCK * SB_QBLOCKS
    pairs = SB_WIDTH // LANES
    return pl.pallas_call(
        _sb_kernel,
        grid=(b, pairs, s // tq),
        in_specs=[
            pl.BlockSpec((None, tq, LANES), lambda bi, p, i: (bi, i, p)),
            pl.BlockSpec((None, s, LANES), lambda bi, p, i: (bi, 0, pairs + p)),
            pl.BlockSpec((None, s, LANES), lambda bi, p, i: (bi, 0, 2 * pairs + p)),
            pl.BlockSpec((SB_BLOCK, SB_BLOCK), lambda bi, p, i: (0, 0)),
        ],
        out_specs=pl.BlockSpec((None, tq, LANES), lambda bi, p, i: (bi, i, p)),
        out_shape=jax.ShapeDtypeStruct((b, s, SB_WIDTH), BF16),
        scratch_shapes=[pltpu.VMEM((SB_QBLOCKS, 2 * SB_BLOCK, LANES), F32),
                        pltpu.VMEM((SB_QBLOCKS, 2 * SB_BLOCK, LANES), F32)],
        compiler_params=_params("arbitrary", "arbitrary", "arbitrary"),
        name="sb_attn",
    )(proj3, proj3, proj3, neg_tri)


def _mix_kernel(x_ref, sb_ref, u_ref, halo_ref, wp_ref, ps_ref, gsb_ref, gpool_ref, wout_ref,
                o_ref, *, seq):
    tm = x_ref.shape[0]
    t0 = (pl.program_id(0) * tm) % seq
    u = u_ref[...].astype(F32)
    halo = jnp.where(t0 == 0, 0.0, halo_ref[...].astype(F32))
    a = jnp.concatenate([halo, u], axis=0)
    pos = t0 + lax.broadcasted_iota(I32, (tm, POOL_GROUP_DIM), 0)
    outs = []
    for g, win in enumerate(POOL_WINDOWS):
        lo, hi = g * POOL_GROUP_DIM, (g + 1) * POOL_GROUP_DIM
        s = a[:, lo:hi]
        sh = 1
        while sh < win:
            s = s + pltpu.roll(s, sh, axis=0)
            sh *= 2
        cnt = jnp.minimum(pos + 1, win).astype(F32)
        outs.append(s[POOL_HALO:, :] / cnt - u[:, lo:hi])
    pooled = jnp.concatenate(outs, axis=1).astype(BF16)
    pooled = jnp.dot(pooled, wp_ref[...], preferred_element_type=F32) * ps_ref[...]
    pool_n = _rms(pooled, gpool_ref[...]).astype(BF16)
    sb_n = _rms(sb_ref[...].astype(F32), gsb_ref[...]).astype(BF16)
    y = jnp.dot(sb_n, wout_ref[:SB_WIDTH, :], preferred_element_type=F32)
    y = y + jnp.dot(pool_n, wout_ref[SB_WIDTH:, :], preferred_element_type=F32)
    o_ref[...] = x_ref[...] + y


def _mix_out(x2d, sb2d, proj2d, wp, ps, gsb, gpool, wout, seq):
    n = x2d.shape[0]
    tm = PROJ_BLOCK
    ucol = 3 * SB_WIDTH // POOL_WIDTH
    hb = tm // POOL_HALO
    return pl.pallas_call(
        functools.partial(_mix_kernel, seq=seq),
        grid=(n // tm,),
        in_specs=[
            pl.BlockSpec((tm, D_MODEL), lambda i: (i, 0)),
            pl.BlockSpec((tm, SB_WIDTH), lambda i: (i, 0)),
            pl.BlockSpec((tm, POOL_WIDTH), lambda i: (i, ucol)),
            pl.BlockSpec((POOL_HALO, POOL_WIDTH), lambda i: (jnp.maximum(i * hb - 1, 0), ucol)),
            pl.BlockSpec((POOL_WIDTH, POOL_WIDTH), lambda i: (0, 0)),
            pl.BlockSpec((1, POOL_WIDTH), lambda i: (0, 0)),
            pl.BlockSpec((1, SB_WIDTH), lambda i: (0, 0)),
            pl.BlockSpec((1, POOL_WIDTH), lambda i: (0, 0)),
            pl.BlockSpec((D_MODEL, D_MODEL), lambda i: (0, 0)),
        ],
        out_specs=pl.BlockSpec((tm, D_MODEL), lambda i: (i, 0)),
        out_shape=jax.ShapeDtypeStruct((n, D_MODEL), F32),
        compiler_params=_params("arbitrary"),
        name="mix_out",
    )(x2d, sb2d, proj2d, proj2d, wp, ps, gsb, gpool, wout)


def _mem_kv_kernel(m_ref, g_ref, w_ref, o_ref):
    h = _rms(m_ref[...], g_ref[...]).astype(BF16)
    o_ref[...] = jnp.dot(h, w_ref[...], preferred_element_type=F32).astype(BF16)


def _mem_kv(mem, g, w):
    b = mem.shape[0]
    return pl.pallas_call(
        _mem_kv_kernel,
        grid=(b,),
        in_specs=[
            pl.BlockSpec((None, N_MEM, D_MODEL), lambda i: (i, 0, 0)),
            pl.BlockSpec((1, D_MODEL), lambda i: (0, 0)),
            pl.BlockSpec((D_MODEL, 2 * D_MODEL), lambda i: (0, 0)),
        ],
        out_specs=pl.BlockSpec((None, N_MEM, 2 * D_MODEL), lambda i: (i, 0, 0)),
        out_shape=jax.ShapeDtypeStruct((b, N_MEM, 2 * D_MODEL), BF16),
        compiler_params=_params("arbitrary"),
        name="mem_kv",
    )(mem, g, w)


def _xattn_block(x, kv_ref, gq, wq_ref, wo_ref, gf, wr_ref, br, tri):
    tm = x.shape[0]
    hq = _rms(x, gq).astype(BF16)
    q = jnp.dot(hq, wq_ref[...], preferred_element_type=F32) * (1.0 / math.sqrt(MEM_HEAD_DIM))
    q = q.astype(BF16)
    outs = []
    for h in range(MEM_HEADS):
        lo, hi = h * MEM_HEAD_DIM, (h + 1) * MEM_HEAD_DIM
        s = lax.dot_general(q[:, lo:hi], kv_ref[:, lo:hi], (((1,), (1,)), ((), ())),
                            preferred_element_type=F32)
        p = jnp.exp(s - jnp.max(s, axis=-1, keepdims=True))
        denom = jnp.sum(p, axis=-1, keepdims=True)
        o = jnp.dot(p.astype(BF16), kv_ref[:, D_MODEL + lo:D_MODEL + hi], preferred_element_type=F32)
        outs.append(o / denom)
    o = jnp.concatenate(outs, axis=1).astype(BF16)
    x2 = x + jnp.dot(o, wo_ref[...], preferred_element_type=F32)
    hf = _rms(x2, gf)

    h_hi = hf.astype(BF16)
    h_lo = (hf - h_hi.astype(F32)).astype(BF16)
    hw = jnp.dot(h_hi, wr_ref[...], preferred_element_type=F32)
    lw = jnp.dot(h_lo, wr_ref[:, :LANES], preferred_element_type=F32)
    logits = (hw[:, :LANES] + (hw[:, LANES:] + lw)) + br

    lane = lax.broadcasted_iota(I32, (tm, LANES), 1).astype(F32)
    work = logits
    vals, idxs, sels = [], [], []
    for _ in range(TOP_K):
        m = jnp.max(work, axis=-1, keepdims=True)
        idx = jnp.min(jnp.where(work == m, lane, float(LANES)), axis=-1, keepdims=True)
        sel = lane == idx
        vals.append(m)
        idxs.append(idx)
        sels.append(sel)
        work = jnp.where(sel, -3e38, work)
    exps = [jnp.exp(v - vals[0]) for v in vals]
    den = exps[0] + exps[1] + exps[2] + exps[3]

    onehot = jnp.zeros((tm, LANES), F32)
    for sel in sels:
        onehot = onehot + sel.astype(F32)
    blocks = [onehot[r:r + TOK_BLOCK, :] for r in range(0, tm, TOK_BLOCK)]
    before = jnp.concatenate(
        [jnp.dot(tri, blk.astype(BF16), preferred_element_type=F32) for blk in blocks], axis=0)
    counts = [jnp.sum(blk, axis=0, keepdims=True) for blk in blocks]
    upper = (lax.broadcasted_iota(I32, (LANES, LANES), 0)
             < lax.broadcasted_iota(I32, (LANES, LANES), 1)).astype(BF16)
    starts = []
    for cnt in counts:
        seg = jnp.where(lane[:SUBLANES, :] < float(N_EXPERTS), _segment_rows(cnt), 0.0)
        seg = jnp.broadcast_to(seg, (SUBLANES, LANES)).astype(BF16)
        start = jnp.dot(seg, upper, preferred_element_type=F32)[:1, :]
        starts.append(jnp.broadcast_to(start, (TOK_BLOCK, LANES)))
    first_row = before + jnp.concatenate(starts, axis=0)
    meta = jnp.zeros((tm, LANES), F32)
    gates = jnp.zeros((tm, LANES), F32)
    for k in range(TOP_K):
        pos = jnp.sum(jnp.where(sels[k], first_row, 0.0), axis=-1, keepdims=True)
        meta = jnp.where(lane == float(k), idxs[k], meta)
        meta = jnp.where(lane == float(TOP_K + k), pos, meta)
        gates = jnp.where(lane == float(k), exps[k] / den, gates)
    return x2, hf.astype(BF16), meta.astype(I32), gates, [c.astype(I32) for c in counts]


def _xattn_kernel(x_ref, kv_ref, gq_ref, wq_ref, wo_ref, gf_ref, wr_ref, br_ref, tri_ref,
                  x2_ref, h_ref, meta_ref, meta_t_ref, gate_ref, cnt_ref):
    tb = TOK_BLOCK * XATTN_CHAIN
    for chain in range(XATTN_BLOCKS // XATTN_CHAIN):
        rows = slice(chain * tb, (chain + 1) * tb)
        x2, hf, meta, gates, counts = _xattn_block(
            x_ref[rows, :], kv_ref, gq_ref[...], wq_ref, wo_ref, gf_ref[...], wr_ref, br_ref[...],
            tri_ref[...])
        x2_ref[rows, :] = x2
        h_ref[rows, :] = hf
        meta_ref[rows, :] = meta
        meta_t_ref[:, rows] = meta.T[:SUBLANES, :]
        gate_ref[rows, :] = gates
        for r, c in enumerate(counts):
            cnt_ref[chain * XATTN_CHAIN + r] = c


def _xattn_router(x1, kv, gq, wq, wo, gf, wr, br, tri, seq):
    n = x1.shape[0]
    tm = TOK_BLOCK * XATTN_BLOCKS
    per_seq = seq // tm
    const = lambda i: (0, 0)
    return pl.pallas_call(
        _xattn_kernel,
        grid=(n // tm,),
        in_specs=[
            pl.BlockSpec((tm, D_MODEL), lambda i: (i, 0)),
            pl.BlockSpec((None, N_MEM, 2 * D_MODEL), lambda i: (i // per_seq, 0, 0)),
            pl.BlockSpec((1, D_MODEL), const),
            pl.BlockSpec((D_MODEL, D_MODEL), const),
            pl.BlockSpec((D_MODEL, D_MODEL), const),
            pl.BlockSpec((1, D_MODEL), const),
            pl.BlockSpec((D_MODEL, 2 * LANES), const),
            pl.BlockSpec((1, LANES), const),
            pl.BlockSpec((TOK_BLOCK, TOK_BLOCK), const),
        ],
        out_specs=[
            pl.BlockSpec((tm, D_MODEL), lambda i: (i, 0)),
            pl.BlockSpec((tm, D_MODEL), lambda i: (i, 0)),
            pl.BlockSpec((tm, LANES), lambda i: (i, 0)),
            pl.BlockSpec((SUBLANES, tm), lambda i: (0, i)),
            pl.BlockSpec((tm, LANES), lambda i: (i, 0)),
            pl.BlockSpec((XATTN_BLOCKS, 1, LANES), lambda i: (i, 0, 0)),
        ],
        out_shape=[
            jax.ShapeDtypeStruct((n, D_MODEL), F32),
            jax.ShapeDtypeStruct((n, D_MODEL), BF16),
            jax.ShapeDtypeStruct((n, LANES), I32),
            jax.ShapeDtypeStruct((SUBLANES, n), I32),
            jax.ShapeDtypeStruct((n, LANES), F32),
            jax.ShapeDtypeStruct((n // TOK_BLOCK, 1, LANES), I32),
        ],
        compiler_params=_params("arbitrary"),
        name="xattn_router",
    )(x1, kv, gq, wq, wo, gf, wr, br, tri)


def _segment_copy(tables, step, e, vmem_ref, hbm_ref, sem, to_hbm):
    src_ref, dst_ref, cnt_ref = tables
    at = step * N_EXPERTS + e
    n = pl.multiple_of(cnt_ref[at], SUBLANES)
    local = vmem_ref.at[pl.ds(pl.multiple_of(src_ref[at], SUBLANES), n), :]
    remote = hbm_ref.at[pl.ds(pl.multiple_of(dst_ref[at], SUBLANES), n), :]
    return pltpu.make_async_copy(local, remote, sem) if to_hbm else pltpu.make_async_copy(remote, local, sem)


def _each_segment(fn):
    for r in range(SORT_BLOCKS):
        for e in range(N_EXPERTS):
            fn(r, e)


def _dispatch_kernel(src_ref, dst_ref, cnt_ref, fill_ref, h_ref, pos_ref, xs_ref, stage_ref,
                     zero_ref, sem, fill_sem):
    tables = (src_ref, dst_ref, cnt_ref)
    step = pl.program_id(0)
    last = pl.num_programs(0) - 1
    parity = step % 2
    where = lax.broadcasted_iota(I32, (SORT_ROWS, TOK_BLOCK), 0)
    for r in range(SORT_BLOCKS):
        rows = slice(r * TOK_BLOCK, (r + 1) * TOK_BLOCK)
        pos = pos_ref[:, rows]
        hit = where == pos[TOP_K:TOP_K + 1, :]
        for k in range(1, TOP_K):
            hit = jnp.logical_or(hit, where == pos[TOP_K + k:TOP_K + k + 1, :])
        onehot = jnp.where(hit, 1.0, 0.0).astype(BF16)
        stage_ref[parity * SORT_BLOCKS + r] = jnp.dot(onehot, h_ref[rows, :],
                                                      preferred_element_type=F32)

    def copy(at_step, at_parity, r, e):
        return _segment_copy(tables, at_step * SORT_BLOCKS + r, e,
                             stage_ref.at[at_parity * SORT_BLOCKS + r], xs_ref, sem.at[at_parity], True)

    _each_segment(lambda r, e: copy(step, parity, r, e).start())

    @pl.when(step > 0)
    def _():
        _each_segment(lambda r, e: copy(step - 1, 1 - parity, r, e).wait())

    @pl.when(step == last)
    def _():
        _each_segment(lambda r, e: copy(step, parity, r, e).wait())

    def pad_copy(e):
        n = pl.multiple_of(fill_ref[N_EXPERTS + e], SUBLANES)
        at = pl.multiple_of(fill_ref[e], SUBLANES)
        return pltpu.make_async_copy(zero_ref.at[pl.ds(0, n), :], xs_ref.at[pl.ds(at, n), :], fill_sem)

    def tail_copy(t):
        at = pl.multiple_of(fill_ref[2 * N_EXPERTS] + t * ROW_BLOCK, ROW_BLOCK)
        return pltpu.make_async_copy(zero_ref, xs_ref.at[pl.ds(at, ROW_BLOCK), :], fill_sem)

    def each_pad(fn):
        def body(e, carry):
            @pl.when(fill_ref[N_EXPERTS + e] > 0)
            def _():
                fn(e)
            return carry
        lax.fori_loop(0, N_EXPERTS, body, 0)

    def each_tail(fn):
        def body(t, carry):
            fn(t)
            return carry
        lax.fori_loop(0, fill_ref[2 * N_EXPERTS + 1], body, 0)

    @pl.when(step == 0)
    def _():
        zero_ref[...] = jnp.zeros_like(zero_ref)
        each_pad(lambda e: pad_copy(e).start())
        each_tail(lambda t: tail_copy(t).start())

    @pl.when(step == last)
    def _():
        each_pad(lambda e: pad_copy(e).wait())
        each_tail(lambda t: tail_copy(t).wait())


def _dispatch(tables, fill, hf, pos, n_rows):
    n = hf.shape[0]
    tm = TOK_BLOCK * SORT_BLOCKS
    grid_spec = pltpu.PrefetchScalarGridSpec(
        num_scalar_prefetch=4,
        grid=(n // tm,),
        in_specs=[
            pl.BlockSpec((tm, D_MODEL), lambda i, *_: (i, 0)),
            pl.BlockSpec((SUBLANES, tm), lambda i, *_: (0, i)),
        ],
        out_specs=pl.BlockSpec(memory_space=pl.ANY),
        scratch_shapes=[pltpu.VMEM((2 * SORT_BLOCKS, SORT_ROWS, D_MODEL), F32),
                        pltpu.VMEM((ROW_BLOCK, D_MODEL), F32),
                        pltpu.SemaphoreType.DMA((2,)), pltpu.SemaphoreType.DMA],
    )
    return pl.pallas_call(
        _dispatch_kernel,
        grid_spec=grid_spec,
        out_shape=jax.ShapeDtypeStruct((n_rows, D_MODEL), F32),
        compiler_params=_params("arbitrary"),
        name="dispatch",
    )(*tables, fill, hf, pos)


def _expert_kernel(blk_e_ref, next_e_ref, n_used_ref, xs_ref, w1_hbm, b1_ref, w2_hbm, b2_ref,
                   ys_ref, w1f_ref, w2f_ref, w1b_ref, w2b_ref, sem):
    i = pl.program_id(0)
    used = i < n_used_ref[0]

    def fetch(e):
        return (pltpu.make_async_copy(w1_hbm.at[e], w1f_ref, sem.at[0]),
                pltpu.make_async_copy(w2_hbm.at[e], w2f_ref, sem.at[1]))

    @pl.when(used)
    def _():
        e = blk_e_ref[i]

        @pl.when(i == 0)
        def _():
            for copy in fetch(e):
                copy.start()

        @pl.when(jnp.logical_or(i == 0, e != blk_e_ref[jnp.maximum(i - 1, 0)]))
        def _():
            for copy in fetch(e):
                copy.wait()
            w1b_ref[...] = w1f_ref[...].astype(BF16)
            w2b_ref[...] = w2f_ref[...].astype(BF16)

            @pl.when(next_e_ref[i] >= 0)
            def _():
                for copy in fetch(next_e_ref[i]):
                    copy.start()

        gu = jnp.dot(xs_ref[...].astype(BF16), w1b_ref[...], preferred_element_type=F32) + b1_ref[...]
        gate = jnp.minimum(gu[:, :D_FF], SWIGLU_LIMIT)
        up = jnp.clip(gu[:, D_FF:], -SWIGLU_LIMIT, SWIGLU_LIMIT)
        act = (up + 1.0) * gate * jax.nn.sigmoid(SWIGLU_ALPHA * gate)
        ys_ref[...] = jnp.dot(act.astype(BF16), w2b_ref[...], preferred_element_type=F32) + b2_ref[...]


def _experts(blk_e, next_e, n_used, xs, w1, b1, w2, b2):
    n_rows = xs.shape[0]
    bm = ROW_BLOCK
    last_used = lambda i, nu: jnp.maximum(jnp.minimum(i, nu[0] - 1), 0)
    row = lambda i, be, ne, nu: (last_used(i, nu), 0)
    exp3 = lambda i, be, ne, nu: (be[last_used(i, nu)], 0, 0)
    grid_spec = pltpu.PrefetchScalarGridSpec(
        num_scalar_prefetch=3,
        grid=(n_rows // bm,),
        in_specs=[
            pl.BlockSpec((bm, D_MODEL), row),
            pl.BlockSpec(memory_space=pl.ANY),
            pl.BlockSpec((None, 1, 2 * D_FF), exp3),
            pl.BlockSpec(memory_space=pl.ANY),
            pl.BlockSpec((None, 1, D_MODEL), exp3),
        ],
        out_specs=pl.BlockSpec((bm, D_MODEL), row),
        scratch_shapes=[pltpu.VMEM((D_MODEL, 2 * D_FF), F32), pltpu.VMEM((D_FF, D_MODEL), F32),
                        pltpu.VMEM((D_MODEL, 2 * D_FF), BF16), pltpu.VMEM((D_FF, D_MODEL), BF16),
                        pltpu.SemaphoreType.DMA((2,))],
    )
    return pl.pallas_call(
        _expert_kernel,
        grid_spec=grid_spec,
        out_shape=jax.ShapeDtypeStruct((n_rows, D_MODEL), F32),
        input_output_aliases={3: 0},
        compiler_params=_params("arbitrary"),
        name="experts",
    )(blk_e, next_e, n_used, xs, w1, b1, w2, b2)


def _combine_kernel(src_ref, dst_ref, cnt_ref, x_ref, gate_ref, pos_ref, g_ref, ys_ref, o_ref,
                    stage_ref, sem):
    tables = (src_ref, dst_ref, cnt_ref)
    step = pl.program_id(0)
    last = pl.num_programs(0) - 1
    parity = step % 2

    def copy(at_step, at_parity, r, e):
        return _segment_copy(tables, at_step * SORT_BLOCKS + r, e,
                             stage_ref.at[at_parity * SORT_BLOCKS + r], ys_ref, sem.at[at_parity], False)

    @pl.when(step == 0)
    def _():
        stage_ref[...] = jnp.zeros_like(stage_ref)
        _each_segment(lambda r, e: copy(step, parity, r, e).start())

    @pl.when(step < last)
    def _():
        _each_segment(lambda r, e: copy(step + 1, 1 - parity, r, e).start())

    _each_segment(lambda r, e: copy(step, parity, r, e).wait())

    where = lax.broadcasted_iota(I32, (TOK_BLOCK, SORT_ROWS), 1)
    for r in range(SORT_BLOCKS):
        rows = slice(r * TOK_BLOCK, (r + 1) * TOK_BLOCK)
        pos = pos_ref[rows, :]
        gates = gate_ref[rows, :]
        weights = jnp.zeros((TOK_BLOCK, SORT_ROWS), F32)
        for k in range(TOP_K):
            weights = jnp.where(where == pos[:, TOP_K + k:TOP_K + k + 1], gates[:, k:k + 1], weights)
        y = x_ref[rows, :] + jnp.dot(weights.astype(BF16),
                                     stage_ref[parity * SORT_BLOCKS + r].astype(BF16),
                                     preferred_element_type=F32)
        o_ref[rows, :] = _rms(y, g_ref[...])


def _combine(tables, x2, gates, pos, g_final, ys):
    n = x2.shape[0]
    tc = TOK_BLOCK * SORT_BLOCKS
    grid_spec = pltpu.PrefetchScalarGridSpec(
        num_scalar_prefetch=3,
        grid=(n // tc,),
        in_specs=[
            pl.BlockSpec((tc, D_MODEL), lambda i, *_: (i, 0)),
            pl.BlockSpec((tc, LANES), lambda i, *_: (i, 0)),
            pl.BlockSpec((tc, LANES), lambda i, *_: (i, 0)),
            pl.BlockSpec((1, D_MODEL), lambda i, *_: (0, 0)),
            pl.BlockSpec(memory_space=pl.ANY),
        ],
        out_specs=pl.BlockSpec((tc, D_MODEL), lambda i, *_: (i, 0)),
        scratch_shapes=[pltpu.VMEM((2 * SORT_BLOCKS, SORT_ROWS, D_MODEL), F32),
                        pltpu.SemaphoreType.DMA((2,))],
    )
    return pl.pallas_call(
        _combine_kernel,
        grid_spec=grid_spec,
        out_shape=jax.ShapeDtypeStruct((n, D_MODEL), F32),
        compiler_params=_params("arbitrary"),
        name="combine",
    )(*tables, x2, gates, pos, g_final, ys)


def _strict_lower(n):
    r = lax.broadcasted_iota(I32, (n, n), 0)
    c = lax.broadcasted_iota(I32, (n, n), 1)
    return (r > c).astype(BF16)


def _round_up(v, m):
    return (v + m - 1) // m * m


def _routing_tables(cnt, n):
    bm = ROW_BLOCK
    n_tok_blk = n // TOK_BLOCK
    n_rows = _round_up(n * TOP_K + n_tok_blk * N_EXPERTS * SUBLANES, bm) + N_EXPERTS * bm
    n_blk = n_rows // bm
    seg = _segment_rows(cnt[:, 0, :N_EXPERTS]).astype(I32)
    local_start = jnp.cumsum(seg, axis=1) - seg
    total = jnp.sum(seg, axis=0)
    padded = _round_up(total, bm)
    pend = jnp.cumsum(padded)
    pstart = pend - padded
    global_start = pstart[None, :] + jnp.cumsum(seg, axis=0) - seg
    tables = (local_start.reshape(-1), global_start.reshape(-1), seg.reshape(-1))

    n_used = (pend[-1:] // bm).astype(I32)
    blk_row = jnp.arange(n_blk, dtype=I32) * bm
    blk_e = jnp.minimum(jnp.sum(blk_row[:, None] >= pend[None, :], axis=1), N_EXPERTS - 1).astype(I32)
    ids = jnp.arange(N_EXPERTS, dtype=I32)
    later = jnp.where((ids[None, :] > ids[:, None]) & (padded[None, :] > 0), ids[None, :], N_EXPERTS)
    follow = jnp.min(later, axis=1)
    next_e = jnp.where(follow < N_EXPERTS, follow, -1)[blk_e].astype(I32)
    fill = jnp.concatenate([pstart + total, padded - total, pend[-1:], n_blk - pend[-1:] // bm])
    return tables, fill.astype(I32), blk_e, next_e, n_used, n_rows


def _layer(x2d, mem, b, s, g_mix, w_in, g_sb_out, g_pool_out, w_pool, pool_scale, w_out,
           g_mem_q, g_mem_kv, w_mem_q, w_mem_kv, w_mem_o, g_ffn, w_router, b_router,
           w_e_in, b_e_in, w_e_out, b_e_out, g_out):
    n = b * s
    row = lambda v: v.reshape(1, -1).astype(F32)

    proj = _in_proj(x2d, row(g_mix), w_in.astype(BF16))
    sb = _sb_attention(proj.reshape(b, s, IN_PROJ_COLS), -_strict_lower(SB_BLOCK))
    wp = jnp.zeros((POOL_WIDTH, POOL_WIDTH), BF16)
    for g in range(len(POOL_WINDOWS)):
        at = g * POOL_GROUP_DIM
        wp = wp.at[at:at + POOL_GROUP_DIM, at:at + POOL_GROUP_DIM].set(w_pool[g].astype(BF16))
    x1 = _mix_out(x2d, sb.reshape(n, SB_WIDTH), proj, wp, row(pool_scale),
                  row(g_sb_out), row(g_pool_out), w_out.astype(BF16), s)

    kv = _mem_kv(mem, row(g_mem_kv), w_mem_kv.astype(BF16))
    wr = jnp.zeros((D_MODEL, LANES), F32).at[:, :N_EXPERTS].set(w_router)
    wr_hi = wr.astype(BF16)
    wr_lo = (wr - wr_hi.astype(F32)).astype(BF16)
    br = jnp.full((1, LANES), NEG_BIG, F32).at[0, :N_EXPERTS].set(b_router)
    x2, hf, meta, meta_t, gates, cnt = _xattn_router(
        x1, kv, row(g_mem_q), w_mem_q.astype(BF16), w_mem_o.astype(BF16), row(g_ffn),
        jnp.concatenate([wr_hi, wr_lo], axis=1), br, _strict_lower(TOK_BLOCK), s)

    tables, fill, blk_e, next_e, n_used, n_rows = _routing_tables(cnt, n)
    xs = _dispatch(tables, fill, hf, meta_t, n_rows)
    ys = _experts(blk_e, next_e, n_used, xs, w_e_in, b_e_in.reshape(N_EXPERTS, 1, -1),
                  w_e_out, b_e_out.reshape(N_EXPERTS, 1, -1))
    return _combine(tables, x2, gates, meta, row(g_out), ys)


def kernel(x, mem, g_mix, w_in, g_sb_out, g_pool_out, w_pool, pool_scale, w_out, g_mem_q, g_mem_kv,
           w_mem_q, w_mem_kv, w_mem_o, g_ffn, w_router, b_router, w_expert_in, b_expert_in,
           w_expert_out, b_expert_out, g_final):
    b, s, d = x.shape
    depth = g_mix.shape[0]
    assert d == D_MODEL and depth == 1, "the final RMSNorm is fused into the single layer's combine"
    assert s % PROJ_BLOCK == 0 and s % (SB_BLOCK * SB_QBLOCKS) == 0 and s % (TOK_BLOCK * XATTN_BLOCKS) == 0
    assert (b * s) % (TOK_BLOCK * SORT_BLOCKS) == 0
    out = _layer(x.reshape(b * s, d), mem, b, s, g_mix[0], w_in[0], g_sb_out[0], g_pool_out[0],
                 w_pool[0], pool_scale[0], w_out[0], g_mem_q[0], g_mem_kv[0], w_mem_q[0],
                 w_mem_kv[0], w_mem_o[0], g_ffn[0], w_router[0], b_router[0], w_expert_in[0],
                 b_expert_in[0], w_expert_out[0], b_expert_out[0], g_final)
    return out.reshape(b, s, d)
```

```python
import functools
import math

import jax
import jax.numpy as jnp
from jax import lax
from jax.experimental import pallas as pl
from jax.experimental.pallas import tpu as pltpu

F32 = jnp.float32
BF16 = jnp.bfloat16
I32 = jnp.int32

D_MODEL = 1024
SB_HEADS = 8
SB_HEAD_DIM = 64
SB_WIDTH = SB_HEADS * SB_HEAD_DIM
POOL_WINDOWS = (2, 4, 8, 16)
POOL_GROUP_DIM = 128
POOL_WIDTH = len(POOL_WINDOWS) * POOL_GROUP_DIM
IN_PROJ_COLS = 3 * SB_WIDTH + POOL_WIDTH
N_MEM = 256
MEM_HEADS = 4
MEM_HEAD_DIM = D_MODEL // MEM_HEADS
N_EXPERTS = 32
TOP_K = 4
D_FF = D_MODEL
SWIGLU_LIMIT = 7.0
SWIGLU_ALPHA = 1.702
RMS_EPS = 1e-5

LANES = 128
SUBLANES = 8
POOL_HALO = 16
SB_BLOCK = 256
SB_QBLOCKS = 8
SB_DEAD_LOG = -110.0
ROW_BLOCK = 512
TOK_BLOCK = 256
XATTN_BLOCKS = 4
XATTN_CHAIN = 2
SORT_BLOCKS = 2
PROJ_BLOCK = 1024
SORT_ROWS = -(-(TOK_BLOCK * TOP_K + N_EXPERTS * SUBLANES) // LANES) * LANES
VMEM_LIMIT = 48 * 1024 * 1024
NEG_BIG = -1e30


def _rms(x, g):
    ms = jnp.mean(x * x, axis=-1, keepdims=True)
    return x * lax.rsqrt(ms + RMS_EPS) * g


def _segment_rows(count):
    return jnp.ceil(jnp.maximum(count, 1) / SUBLANES) * SUBLANES


def _params(*sem):
    return pltpu.CompilerParams(dimension_semantics=sem, vmem_limit_bytes=VMEM_LIMIT)


def _in_proj_kernel(x_ref, g_ref, w_ref, o_ref):
    h = _rms(x_ref[...], g_ref[...]).astype(BF16)
    acc = jnp.dot(h, w_ref[...], preferred_element_type=F32)
    scale = 1.0 / math.sqrt(SB_HEAD_DIM)
    o_ref[:, :SB_WIDTH] = (acc[:, :SB_WIDTH] * scale).astype(BF16)
    o_ref[:, SB_WIDTH:] = acc[:, SB_WIDTH:].astype(BF16)


def _in_proj(x2d, g, w):
    n = x2d.shape[0]
    tm = PROJ_BLOCK
    return pl.pallas_call(
        _in_proj_kernel,
        grid=(n // tm,),
        in_specs=[
            pl.BlockSpec((tm, D_MODEL), lambda i: (i, 0)),
            pl.BlockSpec((1, D_MODEL), lambda i: (0, 0)),
            pl.BlockSpec((D_MODEL, IN_PROJ_COLS), lambda i: (0, 0)),
        ],
        out_specs=pl.BlockSpec((tm, IN_PROJ_COLS), lambda i: (i, 0)),
        out_shape=jax.ShapeDtypeStruct((n, IN_PROJ_COLS), BF16),
        compiler_params=_params("arbitrary"),
        name="in_proj",
    )(x2d, g, w)


def _sb_chain(qh, kb, vb, neg_tri, c, causal):
    z = lax.dot_general(qh, kb, (((1,), (1,)), ((), ())), preferred_element_type=F32)
    sp = jnp.maximum(z, 0.0) + jnp.log(1.0 + jnp.exp(-jnp.abs(z)))
    if causal is not None:
        sp = jnp.where(causal, sp, 0.0)
    after = jnp.dot(sp.astype(BF16), neg_tri, preferred_element_type=F32)
    w = jnp.exp((z - sp) + after)
    if causal is not None:
        w = jnp.where(causal, w, 0.0)
    pv = jnp.dot(w.astype(BF16), vb, preferred_element_type=F32) * jnp.exp(c)
    return pv, c + (after[:, :1] - sp[:, :1])


def _sb_kernel(q_ref, k_ref, v_ref, tri_ref, o_ref, acc_ref, c_ref):
    tq = SB_BLOCK
    lane = lax.broadcasted_iota(I32, (tq, LANES), 1)
    neg_tri = tri_ref[...]
    row = lax.broadcasted_iota(I32, (2 * tq, tq), 0)
    col = lax.broadcasted_iota(I32, (2 * tq, tq), 1)
    causal = col < jnp.where(row >= tq, row - tq, row)

    def stacked(u):
        q = q_ref[u * tq:(u + 1) * tq, :]
        zero = jnp.zeros_like(q)
        return jnp.concatenate(
            [jnp.where(lane < SB_HEAD_DIM, q, zero), jnp.where(lane >= SB_HEAD_DIM, q, zero)], axis=0)

    def step(qs, j, c, diagonal):
        off = pl.multiple_of(j * tq, tq)
        return _sb_chain(qs, k_ref[pl.ds(off, tq), :], v_ref[pl.ds(off, tq), :], neg_tri, c,
                         causal if diagonal else None)

    blocks = []
    for u in range(SB_QBLOCKS):
        i = pl.program_id(2) * SB_QBLOCKS + u
        qs = stacked(u)
        pv, c = step(qs, i, jnp.zeros((2 * tq, LANES), F32), True)
        pv_left, c_left = step(qs, jnp.maximum(i - 1, 0), c, False)
        has_left = i > 0
        acc_ref[u] = pv + jnp.where(has_left, pv_left, 0.0)
        c_ref[u] = jnp.where(has_left, c_left, c)
        blocks.append((i, qs))

    @pl.when(jnp.max(c_ref[...]) > SB_DEAD_LOG)
    def _():
        for u, (i, qs) in enumerate(blocks):
            def alive():
                return jnp.max(c_ref[u]) > SB_DEAD_LOG

            def cond(carry):
                j, live = carry
                return jnp.logical_and(j >= 0, live)

            def body(carry):
                j, _ = carry
                pv, c = step(qs, j, c_ref[u], False)
                acc_ref[u] += pv
                c_ref[u] = c
                return j - 1, alive()

            lax.while_loop(cond, body, (i - 2, alive()))

    for u in range(SB_QBLOCKS):
        o_ref[u * tq:(u + 1) * tq, :] = jnp.where(
            lane < SB_HEAD_DIM, acc_ref[u, :tq, :], acc_ref[u, tq:, :]).astype(o_ref.dtype)


def _sb_attention(proj3, neg_tri):
    b, s, _ = proj3.shape
    tq = SB_BLOCK * SB_QBLOCKS
    pairs = SB_WIDTH // LANES
    return pl.pallas_call(
        _sb_kernel,
        grid=(b, pairs, s // tq),
        in_specs=[
            pl.BlockSpec((None, tq, LANES), lambda bi, p, i: (bi, i, p)),
            pl.BlockSpec((None, s, LANES), lambda bi, p, i: (bi, 0, pairs + p)),
            pl.BlockSpec((None, s, LANES), lambda bi, p, i: (bi, 0, 2 * pairs + p)),
            pl.BlockSpec((SB_BLOCK, SB_BLOCK), lambda bi, p, i: (0, 0)),
        ],
        out_specs=pl.BlockSpec((None, tq, LANES), lambda bi, p, i: (bi, i, p)),
        out_shape=jax.ShapeDtypeStruct((b, s, SB_WIDTH), BF16),
        scratch_shapes=[pltpu.VMEM((SB_QBLOCKS, 2 * SB_BLOCK, LANES), F32),
                        pltpu.VMEM((SB_QBLOCKS, 2 * SB_BLOCK, LANES), F32)],
        compiler_params=_params("arbitrary", "arbitrary", "arbitrary"),
        name="sb_attn",
    )(proj3, proj3, proj3, neg_tri)


def _mix_kernel(x_ref, sb_ref, u_ref, halo_ref, wp_ref, ps_ref, gsb_ref, gpool_ref, wout_ref,
                o_ref, *, seq):
    tm = x_ref.shape[0]
    t0 = (pl.program_id(0) * tm) % seq
    u = u_ref[...].astype(F32)
    halo = jnp.where(t0 == 0, 0.0, halo_ref[...].astype(F32))
    a = jnp.concatenate([halo, u], axis=0)
    pos = t0 + lax.broadcasted_iota(I32, (tm, POOL_GROUP_DIM), 0)
    outs = []
    for g, win in enumerate(POOL_WINDOWS):
        lo, hi = g * POOL_GROUP_DIM, (g + 1) * POOL_GROUP_DIM
        s = a[:, lo:hi]
        sh = 1
        while sh < win:
            s = s + pltpu.roll(s, sh, axis=0)
            sh *= 2
        cnt = jnp.minimum(pos + 1, win).astype(F32)
        outs.append(s[POOL_HALO:, :] / cnt - u[:, lo:hi])
    pooled = jnp.concatenate(outs, axis=1).astype(BF16)
    pooled = jnp.dot(pooled, wp_ref[...], preferred_element_type=F32) * ps_ref[...]
    pool_n = _rms(pooled, gpool_ref[...]).astype(BF16)
    sb_n = _rms(sb_ref[...].astype(F32), gsb_ref[...]).astype(BF16)
    y = jnp.dot(sb_n, wout_ref[:SB_WIDTH, :], preferred_element_type=F32)
    y = y + jnp.dot(pool_n, wout_ref[SB_WIDTH:, :], preferred_element_type=F32)
    o_ref[...] = x_ref[...] + y


def _mix_out(x2d, sb2d, proj2d, wp, ps, gsb, gpool, wout, seq):
    n = x2d.shape[0]
    tm = PROJ_BLOCK
    ucol = 3 * SB_WIDTH // POOL_WIDTH
    hb = tm // POOL_HALO
    return pl.pallas_call(
        functools.partial(_mix_kernel, seq=seq),
        grid=(n // tm,),
        in_specs=[
            pl.BlockSpec((tm, D_MODEL), lambda i: (i, 0)),
            pl.BlockSpec((tm, SB_WIDTH), lambda i: (i, 0)),
            pl.BlockSpec((tm, POOL_WIDTH), lambda i: (i, ucol)),
            pl.BlockSpec((POOL_HALO, POOL_WIDTH), lambda i: (jnp.maximum(i * hb - 1, 0), ucol)),
            pl.BlockSpec((POOL_WIDTH, POOL_WIDTH), lambda i: (0, 0)),
            pl.BlockSpec((1, POOL_WIDTH), lambda i: (0, 0)),
            pl.BlockSpec((1, SB_WIDTH), lambda i: (0, 0)),
            pl.BlockSpec((1, POOL_WIDTH), lambda i: (0, 0)),
            pl.BlockSpec((D_MODEL, D_MODEL), lambda i: (0, 0)),
        ],
        out_specs=pl.BlockSpec((tm, D_MODEL), lambda i: (i, 0)),
        out_shape=jax.ShapeDtypeStruct((n, D_MODEL), F32),
        compiler_params=_params("arbitrary"),
        name="mix_out",
    )(x2d, sb2d, proj2d, proj2d, wp, ps, gsb, gpool, wout)


def _mem_kv_kernel(m_ref, g_ref, w_ref, o_ref):
    h = _rms(m_ref[...], g_ref[...]).astype(BF16)
    o_ref[...] = jnp.dot(h, w_ref[...], preferred_element_type=F32).astype(BF16)


def _mem_kv(mem, g, w):
    b = mem.shape[0]
    return pl.pallas_call(
        _mem_kv_kernel,
        grid=(b,),
        in_specs=[
            pl.BlockSpec((None, N_MEM, D_MODEL), lambda i: (i, 0, 0)),
            pl.BlockSpec((1, D_MODEL), lambda i: (0, 0)),
            pl.BlockSpec((D_MODEL, 2 * D_MODEL), lambda i: (0, 0)),
        ],
        out_specs=pl.BlockSpec((None, N_MEM, 2 * D_MODEL), lambda i: (i, 0, 0)),
        out_shape=jax.ShapeDtypeStruct((b, N_MEM, 2 * D_MODEL), BF16),
        compiler_params=_params("arbitrary"),
        name="mem_kv",
    )(mem, g, w)


def _xattn_block(x, kv_ref, gq, wq_ref, wo_ref, gf, wr_ref, br, tri):
    tm = x.shape[0]
    hq = _rms(x, gq).astype(BF16)
    q = jnp.dot(hq, wq_ref[...], preferred_element_type=F32) * (1.0 / math.sqrt(MEM_HEAD_DIM))
    q = q.astype(BF16)
    outs = []
    for h in range(MEM_HEADS):
        lo, hi = h * MEM_HEAD_DIM, (h + 1) * MEM_HEAD_DIM
        s = lax.dot_general(q[:, lo:hi], kv_ref[:, lo:hi], (((1,), (1,)), ((), ())),
                            preferred_element_type=F32)
        p = jnp.exp(s - jnp.max(s, axis=-1, keepdims=True))
        denom = jnp.sum(p, axis=-1, keepdims=True)
        o = jnp.dot(p.astype(BF16), kv_ref[:, D_MODEL + lo:D_MODEL + hi], preferred_element_type=F32)
        outs.append(o / denom)
    o = jnp.concatenate(outs, axis=1).astype(BF16)
    x2 = x + jnp.dot(o, wo_ref[...], preferred_element_type=F32)
    hf = _rms(x2, gf)

    h_hi = hf.astype(BF16)
    h_lo = (hf - h_hi.astype(F32)).astype(BF16)
    hw = jnp.dot(h_hi, wr_ref[...], preferred_element_type=F32)
    lw = jnp.dot(h_lo, wr_ref[:, :LANES], preferred_element_type=F32)
    logits = (hw[:, :LANES] + (hw[:, LANES:] + lw)) + br

    lane = lax.broadcasted_iota(I32, (tm, LANES), 1).astype(F32)
    work = logits
    vals, idxs, sels = [], [], []
    for _ in range(TOP_K):
        m = jnp.max(work, axis=-1, keepdims=True)
        idx = jnp.min(jnp.where(work == m, lane, float(LANES)), axis=-1, keepdims=True)
        sel = lane == idx
        vals.append(m)
        idxs.append(idx)
        sels.append(sel)
        work = jnp.where(sel, -3e38, work)
    exps = [jnp.exp(v - vals[0]) for v in vals]
    den = exps[0] + exps[1] + exps[2] + exps[3]

    onehot = jnp.zeros((tm, LANES), F32)
    for sel in sels:
        onehot = onehot + sel.astype(F32)
    blocks = [onehot[r:r + TOK_BLOCK, :] for r in range(0, tm, TOK_BLOCK)]
    before = jnp.concatenate(
        [jnp.dot(tri, blk.astype(BF16), preferred_element_type=F32) for blk in blocks], axis=0)
    counts = [jnp.sum(blk, axis=0, keepdims=True) for blk in blocks]
    upper = (lax.broadcasted_iota(I32, (LANES, LANES), 0)
             < lax.broadcasted_iota(I32, (LANES, LANES), 1)).astype(BF16)
    starts = []
    for cnt in counts:
        seg = jnp.where(lane[:SUBLANES, :] < float(N_EXPERTS), _segment_rows(cnt), 0.0)
        seg = jnp.broadcast_to(seg, (SUBLANES, LANES)).astype(BF16)
        start = jnp.dot(seg, upper, preferred_element_type=F32)[:1, :]
        starts.append(jnp.broadcast_to(start, (TOK_BLOCK, LANES)))
    first_row = before + jnp.concatenate(starts, axis=0)
    meta = jnp.zeros((tm, LANES), F32)
    gates = jnp.zeros((tm, LANES), F32)
    for k in range(TOP_K):
        pos = jnp.sum(jnp.where(sels[k], first_row, 0.0), axis=-1, keepdims=True)
        meta = jnp.where(lane == float(k), idxs[k], meta)
        meta = jnp.where(lane == float(TOP_K + k), pos, meta)
        gates = jnp.where(lane == float(k), exps[k] / den, gates)
    return x2, hf.astype(BF16), meta.astype(I32), gates, [c.astype(I32) for c in counts]


def _xattn_kernel(x_ref, kv_ref, gq_ref, wq_ref, wo_ref, gf_ref, wr_ref, br_ref, tri_ref,
                  x2_ref, h_ref, meta_ref, meta_t_ref, gate_ref, cnt_ref):
    tb = TOK_BLOCK * XATTN_CHAIN
    for chain in range(XATTN_BLOCKS // XATTN_CHAIN):
        rows = slice(chain * tb, (chain + 1) * tb)
        x2, hf, meta, gates, counts = _xattn_block(
            x_ref[rows, :], kv_ref, gq_ref[...], wq_ref, wo_ref, gf_ref[...], wr_ref, br_ref[...],
            tri_ref[...])
        x2_ref[rows, :] = x2
        h_ref[rows, :] = hf
        meta_ref[rows, :] = meta
        meta_t_ref[:, rows] = meta.T[:SUBLANES, :]
        gate_ref[rows, :] = gates
        for r, c in enumerate(counts):
            cnt_ref[chain * XATTN_CHAIN + r] = c


def _xattn_router(x1, kv, gq, wq, wo, gf, wr, br, tri, seq):
    n = x1.shape[0]
    tm = TOK_BLOCK * XATTN_BLOCKS
    per_seq = seq // tm
    const = lambda i: (0, 0)
    return pl.pallas_call(
        _xattn_kernel,
        grid=(n // tm,),
        in_specs=[
            pl.BlockSpec((tm, D_MODEL), lambda i: (i, 0)),
            pl.BlockSpec((None, N_MEM, 2 * D_MODEL), lambda i: (i // per_seq, 0, 0)),
            pl.BlockSpec((1, D_MODEL), const),
            pl.BlockSpec((D_MODEL, D_MODEL), const),
            pl.BlockSpec((D_MODEL, D_MODEL), const),
            pl.BlockSpec((1, D_MODEL), const),
            pl.BlockSpec((D_MODEL, 2 * LANES), const),
            pl.BlockSpec((1, LANES), const),
            pl.BlockSpec((TOK_BLOCK, TOK_BLOCK), const),
        ],
        out_specs=[
            pl.BlockSpec((tm, D_MODEL), lambda i: (i, 0)),
            pl.BlockSpec((tm, D_MODEL), lambda i: (i, 0)),
            pl.BlockSpec((tm, LANES), lambda i: (i, 0)),
            pl.BlockSpec((SUBLANES, tm), lambda i: (0, i)),
            pl.BlockSpec((tm, LANES), lambda i: (i, 0)),
            pl.BlockSpec((XATTN_BLOCKS, 1, LANES), lambda i: (i, 0, 0)),
        ],
        out_shape=[
            jax.ShapeDtypeStruct((n, D_MODEL), F32),
            jax.ShapeDtypeStruct((n, D_MODEL), BF16),
            jax.ShapeDtypeStruct((n, LANES), I32),
            jax.ShapeDtypeStruct((SUBLANES, n), I32),
            jax.ShapeDtypeStruct((n, LANES), F32),
            jax.ShapeDtypeStruct((n // TOK_BLOCK, 1, LANES), I32),
        ],
        compiler_params=_params("arbitrary"),
        name="xattn_router",
    )(x1, kv, gq, wq, wo, gf, wr, br, tri)


def _segment_copy(tables, step, e, vmem_ref, hbm_ref, sem, to_hbm):
    src_ref, dst_ref, cnt_ref = tables
    at = step * N_EXPERTS + e
    n = pl.multiple_of(cnt_ref[at], SUBLANES)
    local = vmem_ref.at[pl.ds(pl.multiple_of(src_ref[at], SUBLANES), n), :]
    remote = hbm_ref.at[pl.ds(pl.multiple_of(dst_ref[at], SUBLANES), n), :]
    return pltpu.make_async_copy(local, remote, sem) if to_hbm else pltpu.make_async_copy(remote, local, sem)


def _each_segment(fn):
    for r in range(SORT_BLOCKS):
        for e in range(N_EXPERTS):
            fn(r, e)


def _dispatch_kernel(src_ref, dst_ref, cnt_ref, fill_ref, h_ref, pos_ref, xs_ref, stage_ref,
                     zero_ref, sem, fill_sem):
    tables = (src_ref, dst_ref, cnt_ref)
    step = pl.program_id(0)
    last = pl.num_programs(0) - 1
    parity = step % 2
    where = lax.broadcasted_iota(I32, (SORT_ROWS, TOK_BLOCK), 0)
    for r in range(SORT_BLOCKS):
        rows = slice(r * TOK_BLOCK, (r + 1) * TOK_BLOCK)
        pos = pos_ref[:, rows]
        hit = where == pos[TOP_K:TOP_K + 1, :]
        for k in range(1, TOP_K):
            hit = jnp.logical_or(hit, where == pos[TOP_K + k:TOP_K + k + 1, :])
        onehot = jnp.where(hit, 1.0, 0.0).astype(BF16)
        stage_ref[parity * SORT_BLOCKS + r] = jnp.dot(onehot, h_ref[rows, :],
                                                      preferred_element_type=F32)

    def copy(at_step, at_parity, r, e):
        return _segment_copy(tables, at_step * SORT_BLOCKS + r, e,
                             stage_ref.at[at_parity * SORT_BLOCKS + r], xs_ref, sem.at[at_parity], True)

    _each_segment(lambda r, e: copy(step, parity, r, e).start())

    @pl.when(step > 0)
    def _():
        _each_segment(lambda r, e: copy(step - 1, 1 - parity, r, e).wait())

    @pl.when(step == last)
    def _():
        _each_segment(lambda r, e: copy(step, parity, r, e).wait())

    def pad_copy(e):
        n = pl.multiple_of(fill_ref[N_EXPERTS + e], SUBLANES)
        at = pl.multiple_of(fill_ref[e], SUBLANES)
        return pltpu.make_async_copy(zero_ref.at[pl.ds(0, n), :], xs_ref.at[pl.ds(at, n), :], fill_sem)

    def tail_copy(t):
        at = pl.multiple_of(fill_ref[2 * N_EXPERTS] + t * ROW_BLOCK, ROW_BLOCK)
        return pltpu.make_async_copy(zero_ref, xs_ref.at[pl.ds(at, ROW_BLOCK), :], fill_sem)

    def each_pad(fn):
        def body(e, carry):
            @pl.when(fill_ref[N_EXPERTS + e] > 0)
            def _():
                fn(e)
            return carry
        lax.fori_loop(0, N_EXPERTS, body, 0)

    def each_tail(fn):
        def body(t, carry):
            fn(t)
            return carry
        lax.fori_loop(0, fill_ref[2 * N_EXPERTS + 1], body, 0)

    @pl.when(step == 0)
    def _():
        zero_ref[...] = jnp.zeros_like(zero_ref)
        each_pad(lambda e: pad_copy(e).start())
        each_tail(lambda t: tail_copy(t).start())

    @pl.when(step == last)
    def _():
        each_pad(lambda e: pad_copy(e).wait())
        each_tail(lambda t: tail_copy(t).wait())


def _dispatch(tables, fill, hf, pos, n_rows):
    n = hf.shape[0]
    tm = TOK_BLOCK * SORT_BLOCKS
    grid_spec = pltpu.PrefetchScalarGridSpec(
        num_scalar_prefetch=4,
        grid=(n // tm,),
        in_specs=[
            pl.BlockSpec((tm, D_MODEL), lambda i, *_: (i, 0)),
            pl.BlockSpec((SUBLANES, tm), lambda i, *_: (0, i)),
        ],
        out_specs=pl.BlockSpec(memory_space=pl.ANY),
        scratch_shapes=[pltpu.VMEM((2 * SORT_BLOCKS, SORT_ROWS, D_MODEL), F32),
                        pltpu.VMEM((ROW_BLOCK, D_MODEL), F32),
                        pltpu.SemaphoreType.DMA((2,)), pltpu.SemaphoreType.DMA],
    )
    return pl.pallas_call(
        _dispatch_kernel,
        grid_spec=grid_spec,
        out_shape=jax.ShapeDtypeStruct((n_rows, D_MODEL), F32),
        compiler_params=_params("arbitrary"),
        name="dispatch",
    )(*tables, fill, hf, pos)


def _expert_kernel(blk_e_ref, next_e_ref, n_used_ref, xs_ref, w1_hbm, b1_ref, w2_hbm, b2_ref,
                   ys_ref, w1f_ref, w2f_ref, w1b_ref, w2b_ref, sem):
    i = pl.program_id(0)
    used = i < n_used_ref[0]

    def fetch(e):
        return (pltpu.make_async_copy(w1_hbm.at[e], w1f_ref, sem.at[0]),
                pltpu.make_async_copy(w2_hbm.at[e], w2f_ref, sem.at[1]))

    @pl.when(used)
    def _():
        e = blk_e_ref[i]

        @pl.when(i == 0)
        def _():
            for copy in fetch(e):
                copy.start()

        @pl.when(jnp.logical_or(i == 0, e != blk_e_ref[jnp.maximum(i - 1, 0)]))
        def _():
            for copy in fetch(e):
                copy.wait()
            w1b_ref[...] = w1f_ref[...].astype(BF16)
            w2b_ref[...] = w2f_ref[...].astype(BF16)

            @pl.when(next_e_ref[i] >= 0)
            def _():
                for copy in fetch(next_e_ref[i]):
                    copy.start()

        gu = jnp.dot(xs_ref[...].astype(BF16), w1b_ref[...], preferred_element_type=F32) + b1_ref[...]
        gate = jnp.minimum(gu[:, :D_FF], SWIGLU_LIMIT)
        up = jnp.clip(gu[:, D_FF:], -SWIGLU_LIMIT, SWIGLU_LIMIT)
        act = (up + 1.0) * gate * jax.nn.sigmoid(SWIGLU_ALPHA * gate)
        ys_ref[...] = jnp.dot(act.astype(BF16), w2b_ref[...], preferred_element_type=F32) + b2_ref[...]


def _experts(blk_e, next_e, n_used, xs, w1, b1, w2, b2):
    n_rows = xs.shape[0]
    bm = ROW_BLOCK
    last_used = lambda i, nu: jnp.maximum(jnp.minimum(i, nu[0] - 1), 0)
    row = lambda i, be, ne, nu: (last_used(i, nu), 0)
    exp3 = lambda i, be, ne, nu: (be[last_used(i, nu)], 0, 0)
    grid_spec = pltpu.PrefetchScalarGridSpec(
        num_scalar_prefetch=3,
        grid=(n_rows // bm,),
        in_specs=[
            pl.BlockSpec((bm, D_MODEL), row),
            pl.BlockSpec(memory_space=pl.ANY),
            pl.BlockSpec((None, 1, 2 * D_FF), exp3),
            pl.BlockSpec(memory_space=pl.ANY),
            pl.BlockSpec((None, 1, D_MODEL), exp3),
        ],
        out_specs=pl.BlockSpec((bm, D_MODEL), row),
        scratch_shapes=[pltpu.VMEM((D_MODEL, 2 * D_FF), F32), pltpu.VMEM((D_FF, D_MODEL), F32),
                        pltpu.VMEM((D_MODEL, 2 * D_FF), BF16), pltpu.VMEM((D_FF, D_MODEL), BF16),
                        pltpu.SemaphoreType.DMA((2,))],
    )
    return pl.pallas_call(
        _expert_kernel,
        grid_spec=grid_spec,
        out_shape=jax.ShapeDtypeStruct((n_rows, D_MODEL), F32),
        input_output_aliases={3: 0},
        compiler_params=_params("arbitrary"),
        name="experts",
    )(blk_e, next_e, n_used, xs, w1, b1, w2, b2)


def _combine_kernel(src_ref, dst_ref, cnt_ref, x_ref, gate_ref, pos_ref, g_ref, ys_ref, o_ref,
                    stage_ref, sem):
    tables = (src_ref, dst_ref, cnt_ref)
    step = pl.program_id(0)
    last = pl.num_programs(0) - 1
    parity = step % 2

    def copy(at_step, at_parity, r, e):
        return _segment_copy(tables, at_step * SORT_BLOCKS + r, e,
                             stage_ref.at[at_parity * SORT_BLOCKS + r], ys_ref, sem.at[at_parity], False)

    @pl.when(step == 0)
    def _():
        stage_ref[...] = jnp.zeros_like(stage_ref)
        _each_segment(lambda r, e: copy(step, parity, r, e).start())

    @pl.when(step < last)
    def _():
        _each_segment(lambda r, e: copy(step + 1, 1 - parity, r, e).start())

    _each_segment(lambda r, e: copy(step, parity, r, e).wait())

    where = lax.broadcasted_iota(I32, (TOK_BLOCK, SORT_ROWS), 1)
    for r in range(SORT_BLOCKS):
        rows = slice(r * TOK_BLOCK, (r + 1) * TOK_BLOCK)
        pos = pos_ref[rows, :]
        gates = gate_ref[rows, :]
        weights = jnp.zeros((TOK_BLOCK, SORT_ROWS), F32)
        for k in range(TOP_K):
            weights = jnp.where(where == pos[:, TOP_K + k:TOP_K + k + 1], gates[:, k:k + 1], weights)
        y = x_ref[rows, :] + jnp.dot(weights.astype(BF16),
                                     stage_ref[parity * SORT_BLOCKS + r].astype(BF16),
                                     preferred_element_type=F32)
        o_ref[rows, :] = _rms(y, g_ref[...])


def _combine(tables, x2, gates, pos, g_final, ys):
    n = x2.shape[0]
    tc = TOK_BLOCK * SORT_BLOCKS
    grid_spec = pltpu.PrefetchScalarGridSpec(
        num_scalar_prefetch=3,
        grid=(n // tc,),
        in_specs=[
            pl.BlockSpec((tc, D_MODEL), lambda i, *_: (i, 0)),
            pl.BlockSpec((tc, LANES), lambda i, *_: (i, 0)),
            pl.BlockSpec((tc, LANES), lambda i, *_: (i, 0)),
            pl.BlockSpec((1, D_MODEL), lambda i, *_: (0, 0)),
            pl.BlockSpec(memory_space=pl.ANY),
        ],
        out_specs=pl.BlockSpec((tc, D_MODEL), lambda i, *_: (i, 0)),
        scratch_shapes=[pltpu.VMEM((2 * SORT_BLOCKS, SORT_ROWS, D_MODEL), F32),
                        pltpu.SemaphoreType.DMA((2,))],
    )
    return pl.pallas_call(
        _combine_kernel,
        grid_spec=grid_spec,
        out_shape=jax.ShapeDtypeStruct((n, D_MODEL), F32),
        compiler_params=_params("arbitrary"),
        name="combine",
    )(*tables, x2, gates, pos, g_final, ys)


def _strict_lower(n):
    r = lax.broadcasted_iota(I32, (n, n), 0)
    c = lax.broadcasted_iota(I32, (n, n), 1)
    return (r > c).astype(BF16)


def _round_up(v, m):
    return (v + m - 1) // m * m


def _routing_tables(cnt, n):
    bm = ROW_BLOCK
    n_tok_blk = n // TOK_BLOCK
    n_rows = _round_up(n * TOP_K + n_tok_blk * N_EXPERTS * SUBLANES, bm) + N_EXPERTS * bm
    n_blk = n_rows // bm
    seg = _segment_rows(cnt[:, 0, :N_EXPERTS]).astype(I32)
    local_start = jnp.cumsum(seg, axis=1) - seg
    total = jnp.sum(seg, axis=0)
    padded = _round_up(total, bm)
    pend = jnp.cumsum(padded)
    pstart = pend - padded
    global_start = pstart[None, :] + jnp.cumsum(seg, axis=0) - seg
    tables = (local_start.reshape(-1), global_start.reshape(-1), seg.reshape(-1))

    n_used = (pend[-1:] // bm).astype(I32)
    blk_row = jnp.arange(n_blk, dtype=I32) * bm
    blk_e = jnp.minimum(jnp.sum(blk_row[:, None] >= pend[None, :], axis=1), N_EXPERTS - 1).astype(I32)
    ids = jnp.arange(N_EXPERTS, dtype=I32)
    later = jnp.where((ids[None, :] > ids[:, None]) & (padded[None, :] > 0), ids[None, :], N_EXPERTS)
    follow = jnp.min(later, axis=1)
    next_e = jnp.where(follow < N_EXPERTS, follow, -1)[blk_e].astype(I32)
    fill = jnp.concatenate([pstart + total, padded - total, pend[-1:], n_blk - pend[-1:] // bm])
    return tables, fill.astype(I32), blk_e, next_e, n_used, n_rows


def _layer(x2d, mem, b, s, g_mix, w_in, g_sb_out, g_pool_out, w_pool, pool_scale, w_out,
           g_mem_q, g_mem_kv, w_mem_q, w_mem_kv, w_mem_o, g_ffn, w_router, b_router,
           w_e_in, b_e_in, w_e_out, b_e_out, g_out):
    n = b * s
    row = lambda v: v.reshape(1, -1).astype(F32)

    proj = _in_proj(x2d, row(g_mix), w_in.astype(BF16))
    sb = _sb_attention(proj.reshape(b, s, IN_PROJ_COLS), -_strict_lower(SB_BLOCK))
    groups = len(POOL_WINDOWS)
    wp = (w_pool.astype(BF16)[:, :, None, :] * jnp.eye(groups, dtype=BF16)[:, None, :, None])
    wp = wp.reshape(POOL_WIDTH, POOL_WIDTH)
    x1 = _mix_out(x2d, sb.reshape(n, SB_WIDTH), proj, wp, row(pool_scale),
                  row(g_sb_out), row(g_pool_out), w_out.astype(BF16), s)

    kv = _mem_kv(mem, row(g_mem_kv), w_mem_kv.astype(BF16))
    spare = LANES - N_EXPERTS
    wr = jnp.pad(w_router.astype(F32), ((0, 0), (0, spare)))
    wr_hi = wr.astype(BF16)
    wr_lo = (wr - wr_hi.astype(F32)).astype(BF16)
    br = jnp.pad(b_router.astype(F32).reshape(1, -1), ((0, 0), (0, spare)), constant_values=NEG_BIG)
    x2, hf, meta, meta_t, gates, cnt = _xattn_router(
        x1, kv, row(g_mem_q), w_mem_q.astype(BF16), w_mem_o.astype(BF16), row(g_ffn),
        jnp.concatenate([wr_hi, wr_lo], axis=1), br, _strict_lower(TOK_BLOCK), s)

    tables, fill, blk_e, next_e, n_used, n_rows = _routing_tables(cnt, n)
    xs = _dispatch(tables, fill, hf, meta_t, n_rows)
    ys = _experts(blk_e, next_e, n_used, xs, w_e_in, b_e_in.reshape(N_EXPERTS, 1, -1),
                  w_e_out, b_e_out.reshape(N_EXPERTS, 1, -1))
    return _combine(tables, x2, gates, meta, row(g_out), ys)


def kernel(x, mem, g_mix, w_in, g_sb_out, g_pool_out, w_pool, pool_scale, w_out, g_mem_q, g_mem_kv,
           w_mem_q, w_mem_kv, w_mem_o, g_ffn, w_router, b_router, w_expert_in, b_expert_in,
           w_expert_out, b_expert_out, g_final):
    b, s, d = x.shape
    depth = g_mix.shape[0]
    assert d == D_MODEL and depth == 1, "the final RMSNorm is fused into the single layer's combine"
    assert s % PROJ_BLOCK == 0 and s % (SB_BLOCK * SB_QBLOCKS) == 0 and s % (TOK_BLOCK * XATTN_BLOCKS) == 0
    assert (b * s) % (TOK_BLOCK * SORT_BLOCKS) == 0
    out = _layer(x.reshape(b * s, d), mem, b, s, g_mix[0], w_in[0], g_sb_out[0], g_pool_out[0],
                 w_pool[0], pool_scale[0], w_out[0], g_mem_q[0], g_mem_kv[0], w_mem_q[0],
                 w_mem_kv[0], w_mem_o[0], g_ffn[0], w_router[0], b_router[0], w_expert_in[0],
                 b_expert_in[0], w_expert_out[0], b_expert_out[0], g_final)
    return out.reshape(b, s, d)
```

```python
import functools
import math

import jax
import jax.numpy as jnp
from jax import lax
from jax.experimental import pallas as pl
from jax.experimental.pallas import tpu as pltpu

F32 = jnp.float32
BF16 = jnp.bfloat16
I32 = jnp.int32

D_MODEL = 1024
SB_HEADS = 8
SB_HEAD_DIM = 64
SB_WIDTH = SB_HEADS * SB_HEAD_DIM
POOL_WINDOWS = (2, 4, 8, 16)
POOL_GROUP_DIM = 128
POOL_WIDTH = len(POOL_WINDOWS) * POOL_GROUP_DIM
IN_PROJ_COLS = 3 * SB_WIDTH + POOL_WIDTH
N_MEM = 256
MEM_HEADS = 4
MEM_HEAD_DIM = D_MODEL // MEM_HEADS
N_EXPERTS = 32
TOP_K = 4
D_FF = D_MODEL
SWIGLU_LIMIT = 7.0
SWIGLU_ALPHA = 1.702
RMS_EPS = 1e-5

LANES = 128
SUBLANES = 8
POOL_HALO = 16
SB_BLOCK = 256
SB_QBLOCKS = 4
SB_DEAD_LOG = -110.0
ROW_BLOCK = 512
TOK_BLOCK = 512
XATTN_BLOCKS = 2
XATTN_CHAIN = 1
SORT_BLOCKS = 1
PROJ_BLOCK = 1024
SORT_ROWS = -(-(TOK_BLOCK * TOP_K + N_EXPERTS * SUBLANES) // LANES) * LANES
VMEM_LIMIT = 48 * 1024 * 1024
NEG_BIG = -1e30


def _rms(x, g):
    ms = jnp.mean(x * x, axis=-1, keepdims=True)
    return x * lax.rsqrt(ms + RMS_EPS) * g


def _segment_rows(count):
    return jnp.ceil(jnp.maximum(count, 1) / SUBLANES) * SUBLANES


def _params(*sem):
    return pltpu.CompilerParams(dimension_semantics=sem, vmem_limit_bytes=VMEM_LIMIT)


def _in_proj_kernel(x_ref, g_ref, w_ref, o_ref):
    h = _rms(x_ref[...], g_ref[...]).astype(BF16)
    acc = jnp.dot(h, w_ref[...], preferred_element_type=F32)
    scale = 1.0 / math.sqrt(SB_HEAD_DIM)
    o_ref[:, :SB_WIDTH] = (acc[:, :SB_WIDTH] * scale).astype(BF16)
    o_ref[:, SB_WIDTH:] = acc[:, SB_WIDTH:].astype(BF16)


def _in_proj(x2d, g, w):
    n = x2d.shape[0]
    tm = PROJ_BLOCK
    return pl.pallas_call(
        _in_proj_kernel,
        grid=(n // tm,),
        in_specs=[
            pl.BlockSpec((tm, D_MODEL), lambda i: (i, 0)),
            pl.BlockSpec((1, D_MODEL), lambda i: (0, 0)),
            pl.BlockSpec((D_MODEL, IN_PROJ_COLS), lambda i: (0, 0)),
        ],
        out_specs=pl.BlockSpec((tm, IN_PROJ_COLS), lambda i: (i, 0)),
        out_shape=jax.ShapeDtypeStruct((n, IN_PROJ_COLS), BF16),
        compiler_params=_params("arbitrary"),
        name="in_proj",
    )(x2d, g, w)


def _sb_chain(qh, kb, vb, neg_tri, c, causal):
    z = lax.dot_general(qh, kb, (((1,), (1,)), ((), ())), preferred_element_type=F32)
    sp = jnp.maximum(z, 0.0) + jnp.log(1.0 + jnp.exp(-jnp.abs(z)))
    if causal is not None:
        sp = jnp.where(causal, sp, 0.0)
    after = jnp.dot(sp.astype(BF16), neg_tri, preferred_element_type=F32)
    w = jnp.exp((z - sp) + after)
    if causal is not None:
        w = jnp.where(causal, w, 0.0)
    pv = jnp.dot(w.astype(BF16), vb, preferred_element_type=F32) * jnp.exp(c)
    return pv, c + (after[:, :1] - sp[:, :1])


def _sb_kernel(q_ref, k_ref, v_ref, tri_ref, o_ref, acc_ref, c_ref):
    tq = SB_BLOCK
    lane = lax.broadcasted_iota(I32, (tq, LANES), 1)
    neg_tri = tri_ref[...]
    row = lax.broadcasted_iota(I32, (2 * tq, tq), 0)
    col = lax.broadcasted_iota(I32, (2 * tq, tq), 1)
    causal = col < jnp.where(row >= tq, row - tq, row)

    def stacked(u):
        q = q_ref[u * tq:(u + 1) * tq, :]
        zero = jnp.zeros_like(q)
        return jnp.concatenate(
            [jnp.where(lane < SB_HEAD_DIM, q, zero), jnp.where(lane >= SB_HEAD_DIM, q, zero)], axis=0)

    def step(qs, j, c, diagonal):
        off = pl.multiple_of(j * tq, tq)
        return _sb_chain(qs, k_ref[pl.ds(off, tq), :], v_ref[pl.ds(off, tq), :], neg_tri, c,
                         causal if diagonal else None)

    blocks = []
    for u in range(SB_QBLOCKS):
        i = pl.program_id(2) * SB_QBLOCKS + u
        qs = stacked(u)
        pv, c = step(qs, i, jnp.zeros((2 * tq, LANES), F32), True)
        pv_left, c_left = step(qs, jnp.maximum(i - 1, 0), c, False)
        has_left = i > 0
        acc_ref[u] = pv + jnp.where(has_left, pv_left, 0.0)
        c_ref[u] = jnp.where(has_left, c_left, c)
        blocks.append((i, qs))

    @pl.when(jnp.max(c_ref[...]) > SB_DEAD_LOG)
    def _():
        for u, (i, qs) in enumerate(blocks):
            def alive():
                return jnp.max(c_ref[u]) > SB_DEAD_LOG

            def cond(carry):
                j, live = carry
                return jnp.logical_and(j >= 0, live)

            def body(carry):
                j, _ = carry
                pv, c = step(qs, j, c_ref[u], False)
                acc_ref[u] += pv
                c_ref[u] = c
                return j - 1, alive()

            lax.while_loop(cond, body, (i - 2, alive()))

    for u in range(SB_QBLOCKS):
        o_ref[u * tq:(u + 1) * tq, :] = jnp.where(
            lane < SB_HEAD_DIM, acc_ref[u, :tq, :], acc_ref[u, tq:, :]).astype(o_ref.dtype)


def _sb_attention(proj3, neg_tri):
    b, s, _ = proj3.shape
    tq = SB_BLOCK * SB_QBLOCKS
    pairs = SB_WIDTH // LANES
    return pl.pallas_call(
        _sb_kernel,
        grid=(b, pairs, s // tq),
        in_specs=[
            pl.BlockSpec((None, tq, LANES), lambda bi, p, i: (bi, i, p)),
            pl.BlockSpec((None, s, LANES), lambda bi, p, i: (bi, 0, pairs + p)),
            pl.BlockSpec((None, s, LANES), lambda bi, p, i: (bi, 0, 2 * pairs + p)),
            pl.BlockSpec((SB_BLOCK, SB_BLOCK), lambda bi, p, i: (0, 0)),
        ],
        out_specs=pl.BlockSpec((None, tq, LANES), lambda bi, p, i: (bi, i, p)),
        out_shape=jax.ShapeDtypeStruct((b, s, SB_WIDTH), BF16),
        scratch_shapes=[pltpu.VMEM((SB_QBLOCKS, 2 * SB_BLOCK, LANES), F32),
                        pltpu.VMEM((SB_QBLOCKS, 2 * SB_BLOCK, LANES), F32)],
        compiler_params=_params("arbitrary", "arbitrary", "arbitrary"),
        name="sb_attn",
    )(proj3, proj3, proj3, neg_tri)


def _mix_kernel(x_ref, sb_ref, u_ref, halo_ref, wp_ref, ps_ref, gsb_ref, gpool_ref, wout_ref,
                o_ref, *, seq):
    tm = x_ref.shape[0]
    t0 = (pl.program_id(0) * tm) % seq
    u = u_ref[...].astype(F32)
    halo = jnp.where(t0 == 0, 0.0, halo_ref[...].astype(F32))
    a = jnp.concatenate([halo, u], axis=0)
    pos = t0 + lax.broadcasted_iota(I32, (tm, POOL_GROUP_DIM), 0)
    outs = []
    for g, win in enumerate(POOL_WINDOWS):
        lo, hi = g * POOL_GROUP_DIM, (g + 1) * POOL_GROUP_DIM
        s = a[:, lo:hi]
        sh = 1
        while sh < win:
            s = s + pltpu.roll(s, sh, axis=0)
            sh *= 2
        cnt = jnp.minimum(pos + 1, win).astype(F32)
        outs.append(s[POOL_HALO:, :] / cnt - u[:, lo:hi])
    pooled = jnp.concatenate(outs, axis=1).astype(BF16)
    pooled = jnp.dot(pooled, wp_ref[...], preferred_element_type=F32) * ps_ref[...]
    pool_n = _rms(pooled, gpool_ref[...]).astype(BF16)
    sb_n = _rms(sb_ref[...].astype(F32), gsb_ref[...]).astype(BF16)
    y = jnp.dot(sb_n, wout_ref[:SB_WIDTH, :], preferred_element_type=F32)
    y = y + jnp.dot(pool_n, wout_ref[SB_WIDTH:, :], preferred_element_type=F32)
    o_ref[...] = x_ref[...] + y


def _mix_out(x2d, sb2d, proj2d, wp, ps, gsb, gpool, wout, seq):
    n = x2d.shape[0]
    tm = PROJ_BLOCK
    ucol = 3 * SB_WIDTH // POOL_WIDTH
    hb = tm // POOL_HALO
    return pl.pallas_call(
        functools.partial(_mix_kernel, seq=seq),
        grid=(n // tm,),
        in_specs=[
            pl.BlockSpec((tm, D_MODEL), lambda i: (i, 0)),
            pl.BlockSpec((tm, SB_WIDTH), lambda i: (i, 0)),
            pl.BlockSpec((tm, POOL_WIDTH), lambda i: (i, ucol)),
            pl.BlockSpec((POOL_HALO, POOL_WIDTH), lambda i: (jnp.maximum(i * hb - 1, 0), ucol)),
            pl.BlockSpec((POOL_WIDTH, POOL_WIDTH), lambda i: (0, 0)),
            pl.BlockSpec((1, POOL_WIDTH), lambda i: (0, 0)),
            pl.BlockSpec((1, SB_WIDTH), lambda i: (0, 0)),
            pl.BlockSpec((1, POOL_WIDTH), lambda i: (0, 0)),
            pl.BlockSpec((D_MODEL, D_MODEL), lambda i: (0, 0)),
        ],
        out_specs=pl.BlockSpec((tm, D_MODEL), lambda i: (i, 0)),
        out_shape=jax.ShapeDtypeStruct((n, D_MODEL), F32),
        compiler_params=_params("arbitrary"),
        name="mix_out",
    )(x2d, sb2d, proj2d, proj2d, wp, ps, gsb, gpool, wout)


def _mem_kv_kernel(m_ref, g_ref, w_ref, o_ref):
    h = _rms(m_ref[...], g_ref[...]).astype(BF16)
    o_ref[...] = jnp.dot(h, w_ref[...], preferred_element_type=F32).astype(BF16)


def _mem_kv(mem, g, w):
    b = mem.shape[0]
    return pl.pallas_call(
        _mem_kv_kernel,
        grid=(b,),
        in_specs=[
            pl.BlockSpec((None, N_MEM, D_MODEL), lambda i: (i, 0, 0)),
            pl.BlockSpec((1, D_MODEL), lambda i: (0, 0)),
            pl.BlockSpec((D_MODEL, 2 * D_MODEL), lambda i: (0, 0)),
        ],
        out_specs=pl.BlockSpec((None, N_MEM, 2 * D_MODEL), lambda i: (i, 0, 0)),
        out_shape=jax.ShapeDtypeStruct((b, N_MEM, 2 * D_MODEL), BF16),
        compiler_params=_params("arbitrary"),
        name="mem_kv",
    )(mem, g, w)


def _xattn_block(x, kv_ref, gq, wq_ref, wo_ref, gf, wr_ref, br, tri):
    tm = x.shape[0]
    hq = _rms(x, gq).astype(BF16)
    q = jnp.dot(hq, wq_ref[...], preferred_element_type=F32) * (1.0 / math.sqrt(MEM_HEAD_DIM))
    q = q.astype(BF16)
    outs = []
    for h in range(MEM_HEADS):
        lo, hi = h * MEM_HEAD_DIM, (h + 1) * MEM_HEAD_DIM
        s = lax.dot_general(q[:, lo:hi], kv_ref[:, lo:hi], (((1,), (1,)), ((), ())),
                            preferred_element_type=F32)
        p = jnp.exp(s - jnp.max(s, axis=-1, keepdims=True))
        denom = jnp.sum(p, axis=-1, keepdims=True)
        o = jnp.dot(p.astype(BF16), kv_ref[:, D_MODEL + lo:D_MODEL + hi], preferred_element_type=F32)
        outs.append(o / denom)
    o = jnp.concatenate(outs, axis=1).astype(BF16)
    x2 = x + jnp.dot(o, wo_ref[...], preferred_element_type=F32)
    hf = _rms(x2, gf)

    h_hi = hf.astype(BF16)
    h_lo = (hf - h_hi.astype(F32)).astype(BF16)
    hw = jnp.dot(h_hi, wr_ref[...], preferred_element_type=F32)
    lw = jnp.dot(h_lo, wr_ref[:, :LANES], preferred_element_type=F32)
    logits = (hw[:, :LANES] + (hw[:, LANES:] + lw)) + br

    lane = lax.broadcasted_iota(I32, (tm, LANES), 1).astype(F32)
    work = logits
    vals, idxs, sels = [], [], []
    for _ in range(TOP_K):
        m = jnp.max(work, axis=-1, keepdims=True)
        idx = jnp.min(jnp.where(work == m, lane, float(LANES)), axis=-1, keepdims=True)
        sel = lane == idx
        vals.append(m)
        idxs.append(idx)
        sels.append(sel)
        work = jnp.where(sel, -3e38, work)
    exps = [jnp.exp(v - vals[0]) for v in vals]
    den = exps[0] + exps[1] + exps[2] + exps[3]

    onehot = jnp.zeros((tm, LANES), F32)
    for sel in sels:
        onehot = onehot + sel.astype(F32)
    blocks = [onehot[r:r + TOK_BLOCK, :] for r in range(0, tm, TOK_BLOCK)]
    before = jnp.concatenate(
        [jnp.dot(tri, blk.astype(BF16), preferred_element_type=F32) for blk in blocks], axis=0)
    counts = [jnp.sum(blk, axis=0, keepdims=True) for blk in blocks]
    upper = (lax.broadcasted_iota(I32, (LANES, LANES), 0)
             < lax.broadcasted_iota(I32, (LANES, LANES), 1)).astype(BF16)
    starts = []
    for cnt in counts:
        seg = jnp.where(lane[:SUBLANES, :] < float(N_EXPERTS), _segment_rows(cnt), 0.0)
        seg = jnp.broadcast_to(seg, (SUBLANES, LANES)).astype(BF16)
        start = jnp.dot(seg, upper, preferred_element_type=F32)[:1, :]
        starts.append(jnp.broadcast_to(start, (TOK_BLOCK, LANES)))
    first_row = before + jnp.concatenate(starts, axis=0)
    meta = jnp.zeros((tm, LANES), F32)
    gates = jnp.zeros((tm, LANES), F32)
    for k in range(TOP_K):
        pos = jnp.sum(jnp.where(sels[k], first_row, 0.0), axis=-1, keepdims=True)
        meta = jnp.where(lane == float(k), idxs[k], meta)
        meta = jnp.where(lane == float(TOP_K + k), pos, meta)
        gates = jnp.where(lane == float(k), exps[k] / den, gates)
    return x2, hf.astype(BF16), meta.astype(I32), gates, [c.astype(I32) for c in counts]


def _xattn_kernel(x_ref, kv_ref, gq_ref, wq_ref, wo_ref, gf_ref, wr_ref, br_ref, tri_ref,
                  x2_ref, h_ref, meta_ref, meta_t_ref, gate_ref, cnt_ref):
    tb = TOK_BLOCK * XATTN_CHAIN
    for chain in range(XATTN_BLOCKS // XATTN_CHAIN):
        rows = slice(chain * tb, (chain + 1) * tb)
        x2, hf, meta, gates, counts = _xattn_block(
            x_ref[rows, :], kv_ref, gq_ref[...], wq_ref, wo_ref, gf_ref[...], wr_ref, br_ref[...],
            tri_ref[...])
        x2_ref[rows, :] = x2
        h_ref[rows, :] = hf
        meta_ref[rows, :] = meta
        meta_t_ref[:, rows] = meta.T[:SUBLANES, :]
        gate_ref[rows, :] = gates
        for r, c in enumerate(counts):
            cnt_ref[chain * XATTN_CHAIN + r] = c


def _xattn_router(x1, kv, gq, wq, wo, gf, wr, br, tri, seq):
    n = x1.shape[0]
    tm = TOK_BLOCK * XATTN_BLOCKS
    per_seq = seq // tm
    const = lambda i: (0, 0)
    return pl.pallas_call(
        _xattn_kernel,
        grid=(n // tm,),
        in_specs=[
            pl.BlockSpec((tm, D_MODEL), lambda i: (i, 0)),
            pl.BlockSpec((None, N_MEM, 2 * D_MODEL), lambda i: (i // per_seq, 0, 0)),
            pl.BlockSpec((1, D_MODEL), const),
            pl.BlockSpec((D_MODEL, D_MODEL), const),
            pl.BlockSpec((D_MODEL, D_MODEL), const),
            pl.BlockSpec((1, D_MODEL), const),
            pl.BlockSpec((D_MODEL, 2 * LANES), const),
            pl.BlockSpec((1, LANES), const),
            pl.BlockSpec((TOK_BLOCK, TOK_BLOCK), const),
        ],
        out_specs=[
            pl.BlockSpec((tm, D_MODEL), lambda i: (i, 0)),
            pl.BlockSpec((tm, D_MODEL), lambda i: (i, 0)),
            pl.BlockSpec((tm, LANES), lambda i: (i, 0)),
            pl.BlockSpec((SUBLANES, tm), lambda i: (0, i)),
            pl.BlockSpec((tm, LANES), lambda i: (i, 0)),
            pl.BlockSpec((XATTN_BLOCKS, 1, LANES), lambda i: (i, 0, 0)),
        ],
        out_shape=[
            jax.ShapeDtypeStruct((n, D_MODEL), F32),
            jax.ShapeDtypeStruct((n, D_MODEL), BF16),
            jax.ShapeDtypeStruct((n, LANES), I32),
            jax.ShapeDtypeStruct((SUBLANES, n), I32),
            jax.ShapeDtypeStruct((n, LANES), F32),
            jax.ShapeDtypeStruct((n // TOK_BLOCK, 1, LANES), I32),
        ],
        compiler_params=_params("arbitrary"),
        name="xattn_router",
    )(x1, kv, gq, wq, wo, gf, wr, br, tri)


def _segment_copy(tables, step, e, vmem_ref, hbm_ref, sem, to_hbm):
    src_ref, dst_ref, cnt_ref = tables
    at = step * N_EXPERTS + e
    n = pl.multiple_of(cnt_ref[at], SUBLANES)
    local = vmem_ref.at[pl.ds(pl.multiple_of(src_ref[at], SUBLANES), n), :]
    remote = hbm_ref.at[pl.ds(pl.multiple_of(dst_ref[at], SUBLANES), n), :]
    return pltpu.make_async_copy(local, remote, sem) if to_hbm else pltpu.make_async_copy(remote, local, sem)


def _each_segment(fn):
    for r in range(SORT_BLOCKS):
        for e in range(N_EXPERTS):
            fn(r, e)


def _dispatch_kernel(src_ref, dst_ref, cnt_ref, fill_ref, h_ref, pos_ref, xs_ref, stage_ref,
                     zero_ref, sem, fill_sem):
    tables = (src_ref, dst_ref, cnt_ref)
    step = pl.program_id(0)
    last = pl.num_programs(0) - 1
    parity = step % 2
    where = lax.broadcasted_iota(I32, (SORT_ROWS, TOK_BLOCK), 0)
    for r in range(SORT_BLOCKS):
        rows = slice(r * TOK_BLOCK, (r + 1) * TOK_BLOCK)
        pos = pos_ref[:, rows]
        hit = where == pos[TOP_K:TOP_K + 1, :]
        for k in range(1, TOP_K):
            hit = jnp.logical_or(hit, where == pos[TOP_K + k:TOP_K + k + 1, :])
        onehot = jnp.where(hit, 1.0, 0.0).astype(BF16)
        stage_ref[parity * SORT_BLOCKS + r] = jnp.dot(onehot, h_ref[rows, :],
                                                      preferred_element_type=F32)

    def copy(at_step, at_parity, r, e):
        return _segment_copy(tables, at_step * SORT_BLOCKS + r, e,
                             stage_ref.at[at_parity * SORT_BLOCKS + r], xs_ref, sem.at[at_parity], True)

    _each_segment(lambda r, e: copy(step, parity, r, e).start())

    @pl.when(step > 0)
    def _():
        _each_segment(lambda r, e: copy(step - 1, 1 - parity, r, e).wait())

    @pl.when(step == last)
    def _():
        _each_segment(lambda r, e: copy(step, parity, r, e).wait())

    def pad_copy(e):
        n = pl.multiple_of(fill_ref[N_EXPERTS + e], SUBLANES)
        at = pl.multiple_of(fill_ref[e], SUBLANES)
        return pltpu.make_async_copy(zero_ref.at[pl.ds(0, n), :], xs_ref.at[pl.ds(at, n), :], fill_sem)

    def tail_copy(t):
        at = pl.multiple_of(fill_ref[2 * N_EXPERTS] + t * ROW_BLOCK, ROW_BLOCK)
        return pltpu.make_async_copy(zero_ref, xs_ref.at[pl.ds(at, ROW_BLOCK), :], fill_sem)

    def each_pad(fn):
        def body(e, carry):
            @pl.when(fill_ref[N_EXPERTS + e] > 0)
            def _():
                fn(e)
            return carry
        lax.fori_loop(0, N_EXPERTS, body, 0)

    def each_tail(fn):
        def body(t, carry):
            fn(t)
            return carry
        lax.fori_loop(0, fill_ref[2 * N_EXPERTS + 1], body, 0)

    @pl.when(step == 0)
    def _():
        zero_ref[...] = jnp.zeros_like(zero_ref)
        each_pad(lambda e: pad_copy(e).start())
        each_tail(lambda t: tail_copy(t).start())

    @pl.when(step == last)
    def _():
        each_pad(lambda e: pad_copy(e).wait())
        each_tail(lambda t: tail_copy(t).wait())


def _dispatch(tables, fill, hf, pos, n_rows):
    n = hf.shape[0]
    tm = TOK_BLOCK * SORT_BLOCKS
    grid_spec = pltpu.PrefetchScalarGridSpec(
        num_scalar_prefetch=4,
        grid=(n // tm,),
        in_specs=[
            pl.BlockSpec((tm, D_MODEL), lambda i, *_: (i, 0)),
            pl.BlockSpec((SUBLANES, tm), lambda i, *_: (0, i)),
        ],
        out_specs=pl.BlockSpec(memory_space=pl.ANY),
        scratch_shapes=[pltpu.VMEM((2 * SORT_BLOCKS, SORT_ROWS, D_MODEL), F32),
                        pltpu.VMEM((ROW_BLOCK, D_MODEL), F32),
                        pltpu.SemaphoreType.DMA((2,)), pltpu.SemaphoreType.DMA],
    )
    return pl.pallas_call(
        _dispatch_kernel,
        grid_spec=grid_spec,
        out_shape=jax.ShapeDtypeStruct((n_rows, D_MODEL), F32),
        compiler_params=_params("arbitrary"),
        name="dispatch",
    )(*tables, fill, hf, pos)


def _expert_kernel(blk_e_ref, next_e_ref, n_used_ref, xs_ref, w1_hbm, b1_ref, w2_hbm, b2_ref,
                   ys_ref, w1f_ref, w2f_ref, w1b_ref, w2b_ref, sem):
    i = pl.program_id(0)
    used = i < n_used_ref[0]

    def fetch(e):
        return (pltpu.make_async_copy(w1_hbm.at[e], w1f_ref, sem.at[0]),
                pltpu.make_async_copy(w2_hbm.at[e], w2f_ref, sem.at[1]))

    @pl.when(used)
    def _():
        e = blk_e_ref[i]

        @pl.when(i == 0)
        def _():
            for copy in fetch(e):
                copy.start()

        @pl.when(jnp.logical_or(i == 0, e != blk_e_ref[jnp.maximum(i - 1, 0)]))
        def _():
            for copy in fetch(e):
                copy.wait()
            w1b_ref[...] = w1f_ref[...].astype(BF16)
            w2b_ref[...] = w2f_ref[...].astype(BF16)

            @pl.when(next_e_ref[i] >= 0)
            def _():
                for copy in fetch(next_e_ref[i]):
                    copy.start()

        gu = jnp.dot(xs_ref[...].astype(BF16), w1b_ref[...], preferred_element_type=F32) + b1_ref[...]
        gate = jnp.minimum(gu[:, :D_FF], SWIGLU_LIMIT)
        up = jnp.clip(gu[:, D_FF:], -SWIGLU_LIMIT, SWIGLU_LIMIT)
        act = (up + 1.0) * gate * jax.nn.sigmoid(SWIGLU_ALPHA * gate)
        ys_ref[...] = jnp.dot(act.astype(BF16), w2b_ref[...], preferred_element_type=F32) + b2_ref[...]


def _experts(blk_e, next_e, n_used, xs, w1, b1, w2, b2):
    n_rows = xs.shape[0]
    bm = ROW_BLOCK
    last_used = lambda i, nu: jnp.maximum(jnp.minimum(i, nu[0] - 1), 0)
    row = lambda i, be, ne, nu: (last_used(i, nu), 0)
    exp3 = lambda i, be, ne, nu: (be[last_used(i, nu)], 0, 0)
    grid_spec = pltpu.PrefetchScalarGridSpec(
        num_scalar_prefetch=3,
        grid=(n_rows // bm,),
        in_specs=[
            pl.BlockSpec((bm, D_MODEL), row),
            pl.BlockSpec(memory_space=pl.ANY),
            pl.BlockSpec((None, 1, 2 * D_FF), exp3),
            pl.BlockSpec(memory_space=pl.ANY),
            pl.BlockSpec((None, 1, D_MODEL), exp3),
        ],
        out_specs=pl.BlockSpec((bm, D_MODEL), row),
        scratch_shapes=[pltpu.VMEM((D_MODEL, 2 * D_FF), F32), pltpu.VMEM((D_FF, D_MODEL), F32),
                        pltpu.VMEM((D_MODEL, 2 * D_FF), BF16), pltpu.VMEM((D_FF, D_MODEL), BF16),
                        pltpu.SemaphoreType.DMA((2,))],
    )
    return pl.pallas_call(
        _expert_kernel,
        grid_spec=grid_spec,
        out_shape=jax.ShapeDtypeStruct((n_rows, D_MODEL), F32),
        input_output_aliases={3: 0},
        compiler_params=_params("arbitrary"),
        name="experts",
    )(blk_e, next_e, n_used, xs, w1, b1, w2, b2)


def _combine_kernel(src_ref, dst_ref, cnt_ref, x_ref, gate_ref, pos_ref, g_ref, ys_ref, o_ref,
                    stage_ref, sem):
    tables = (src_ref, dst_ref, cnt_ref)
    step = pl.program_id(0)
    last = pl.num_programs(0) - 1
    parity = step % 2

    def copy(at_step, at_parity, r, e):
        return _segment_copy(tables, at_step * SORT_BLOCKS + r, e,
                             stage_ref.at[at_parity * SORT_BLOCKS + r], ys_ref, sem.at[at_parity], False)

    @pl.when(step == 0)
    def _():
        stage_ref[...] = jnp.zeros_like(stage_ref)
        _each_segment(lambda r, e: copy(step, parity, r, e).start())

    @pl.when(step < last)
    def _():
        _each_segment(lambda r, e: copy(step + 1, 1 - parity, r, e).start())

    _each_segment(lambda r, e: copy(step, parity, r, e).wait())

    where = lax.broadcasted_iota(I32, (TOK_BLOCK, SORT_ROWS), 1)
    for r in range(SORT_BLOCKS):
        rows = slice(r * TOK_BLOCK, (r + 1) * TOK_BLOCK)
        pos = pos_ref[rows, :]
        gates = gate_ref[rows, :]
        weights = jnp.zeros((TOK_BLOCK, SORT_ROWS), F32)
        for k in range(TOP_K):
            weights = jnp.where(where == pos[:, TOP_K + k:TOP_K + k + 1], gates[:, k:k + 1], weights)
        y = x_ref[rows, :] + jnp.dot(weights.astype(BF16),
                                     stage_ref[parity * SORT_BLOCKS + r].astype(BF16),
                                     preferred_element_type=F32)
        o_ref[rows, :] = _rms(y, g_ref[...])


def _combine(tables, x2, gates, pos, g_final, ys):
    n = x2.shape[0]
    tc = TOK_BLOCK * SORT_BLOCKS
    grid_spec = pltpu.PrefetchScalarGridSpec(
        num_scalar_prefetch=3,
        grid=(n // tc,),
        in_specs=[
            pl.BlockSpec((tc, D_MODEL), lambda i, *_: (i, 0)),
            pl.BlockSpec((tc, LANES), lambda i, *_: (i, 0)),
            pl.BlockSpec((tc, LANES), lambda i, *_: (i, 0)),
            pl.BlockSpec((1, D_MODEL), lambda i, *_: (0, 0)),
            pl.BlockSpec(memory_space=pl.ANY),
        ],
        out_specs=pl.BlockSpec((tc, D_MODEL), lambda i, *_: (i, 0)),
        scratch_shapes=[pltpu.VMEM((2 * SORT_BLOCKS, SORT_ROWS, D_MODEL), F32),
                        pltpu.SemaphoreType.DMA((2,))],
    )
    return pl.pallas_call(
        _combine_kernel,
        grid_spec=grid_spec,
        out_shape=jax.ShapeDtypeStruct((n, D_MODEL), F32),
        compiler_params=_params("arbitrary"),
        name="combine",
    )(*tables, x2, gates, pos, g_final, ys)


def _strict_lower(n):
    r = lax.broadcasted_iota(I32, (n, n), 0)
    c = lax.broadcasted_iota(I32, (n, n), 1)
    return (r > c).astype(BF16)


def _round_up(v, m):
    return (v + m - 1) // m * m


def _routing_tables(cnt, n):
    bm = ROW_BLOCK
    n_tok_blk = n // TOK_BLOCK
    n_rows = _round_up(n * TOP_K + n_tok_blk * N_EXPERTS * SUBLANES, bm) + N_EXPERTS * bm
    n_blk = n_rows // bm
    seg = _segment_rows(cnt[:, 0, :N_EXPERTS]).astype(I32)
    local_start = jnp.cumsum(seg, axis=1) - seg
    total = jnp.sum(seg, axis=0)
    padded = _round_up(total, bm)
    pend = jnp.cumsum(padded)
    pstart = pend - padded
    global_start = pstart[None, :] + jnp.cumsum(seg, axis=0) - seg
    tables = (local_start.reshape(-1), global_start.reshape(-1), seg.reshape(-1))

    n_used = (pend[-1:] // bm).astype(I32)
    blk_row = jnp.arange(n_blk, dtype=I32) * bm
    blk_e = jnp.minimum(jnp.sum(blk_row[:, None] >= pend[None, :], axis=1), N_EXPERTS - 1).astype(I32)
    ids = jnp.arange(N_EXPERTS, dtype=I32)
    later = jnp.where((ids[None, :] > ids[:, None]) & (padded[None, :] > 0), ids[None, :], N_EXPERTS)
    follow = jnp.min(later, axis=1)
    follow = jnp.where(follow < N_EXPERTS, follow, -1)
    next_e = jnp.sum(jnp.where(blk_e[:, None] == ids[None, :], follow[None, :], 0), axis=1).astype(I32)
    fill = jnp.concatenate([pstart + total, padded - total, pend[-1:], n_blk - pend[-1:] // bm])
    return tables, fill.astype(I32), blk_e, next_e, n_used, n_rows


def _layer(x2d, mem, b, s, g_mix, w_in, g_sb_out, g_pool_out, w_pool, pool_scale, w_out,
           g_mem_q, g_mem_kv, w_mem_q, w_mem_kv, w_mem_o, g_ffn, w_router, b_router,
           w_e_in, b_e_in, w_e_out, b_e_out, g_out):
    n = b * s
    row = lambda v: v.reshape(1, -1).astype(F32)

    proj = _in_proj(x2d, row(g_mix), w_in.astype(BF16))
    sb = _sb_attention(proj.reshape(b, s, IN_PROJ_COLS), -_strict_lower(SB_BLOCK))
    groups = len(POOL_WINDOWS)
    wp = (w_pool.astype(BF16)[:, :, None, :] * jnp.eye(groups, dtype=BF16)[:, None, :, None])
    wp = wp.reshape(POOL_WIDTH, POOL_WIDTH)
    x1 = _mix_out(x2d, sb.reshape(n, SB_WIDTH), proj, wp, row(pool_scale),
                  row(g_sb_out), row(g_pool_out), w_out.astype(BF16), s)

    kv = _mem_kv(mem, row(g_mem_kv), w_mem_kv.astype(BF16))
    spare = LANES - N_EXPERTS
    wr = jnp.pad(w_router.astype(F32), ((0, 0), (0, spare)))
    wr_hi = wr.astype(BF16)
    wr_lo = (wr - wr_hi.astype(F32)).astype(BF16)
    br = jnp.pad(b_router.astype(F32).reshape(1, -1), ((0, 0), (0, spare)), constant_values=NEG_BIG)
    x2, hf, meta, meta_t, gates, cnt = _xattn_router(
        x1, kv, row(g_mem_q), w_mem_q.astype(BF16), w_mem_o.astype(BF16), row(g_ffn),
        jnp.concatenate([wr_hi, wr_lo], axis=1), br, _strict_lower(TOK_BLOCK), s)

    tables, fill, blk_e, next_e, n_used, n_rows = _routing_tables(cnt, n)
    xs = _dispatch(tables, fill, hf, meta_t, n_rows)
    ys = _experts(blk_e, next_e, n_used, xs, w_e_in, b_e_in.reshape(N_EXPERTS, 1, -1),
                  w_e_out, b_e_out.reshape(N_EXPERTS, 1, -1))
    return _combine(tables, x2, gates, meta, row(g_out), ys)


def kernel(x, mem, g_mix, w_in, g_sb_out, g_pool_out, w_pool, pool_scale, w_out, g_mem_q, g_mem_kv,
           w_mem_q, w_mem_kv, w_mem_o, g_ffn, w_router, b_router, w_expert_in, b_expert_in,
           w_expert_out, b_expert_out, g_final):
    b, s, d = x.shape
    depth = g_mix.shape[0]
    assert d == D_MODEL and depth == 1, "the final RMSNorm is fused into the single layer's combine"
    assert s % PROJ_BLOCK == 0 and s % (SB_BLOCK * SB_QBLOCKS) == 0 and s % (TOK_BLOCK * XATTN_BLOCKS) == 0
    assert (b * s) % (TOK_BLOCK * SORT_BLOCKS) == 0
    out = _layer(x.reshape(b * s, d), mem, b, s, g_mix[0], w_in[0], g_sb_out[0], g_pool_out[0],
                 w_pool[0], pool_scale[0], w_out[0], g_mem_q[0], g_mem_kv[0], w_mem_q[0],
                 w_mem_kv[0], w_mem_o[0], g_ffn[0], w_router[0], b_router[0], w_expert_in[0],
                 b_expert_in[0], w_expert_out[0], b_expert_out[0], g_final)
    return out.reshape(b, s, d)
```

```python
import functools
import math

import jax
import jax.numpy as jnp
from jax import lax
from jax.experimental import pallas as pl
from jax.experimental.pallas import tpu as pltpu

F32 = jnp.float32
BF16 = jnp.bfloat16
I32 = jnp.int32

D_MODEL = 1024
SB_HEADS = 8
SB_HEAD_DIM = 64
SB_WIDTH = SB_HEADS * SB_HEAD_DIM
POOL_WINDOWS = (2, 4, 8, 16)
POOL_GROUP_DIM = 128
POOL_WIDTH = len(POOL_WINDOWS) * POOL_GROUP_DIM
IN_PROJ_COLS = 3 * SB_WIDTH + POOL_WIDTH
N_MEM = 256
MEM_HEADS = 4
MEM_HEAD_DIM = D_MODEL // MEM_HEADS
N_EXPERTS = 32
TOP_K = 4
D_FF = D_MODEL
SWIGLU_LIMIT = 7.0
SWIGLU_ALPHA = 1.702
RMS_EPS = 1e-5

LANES = 128
SUBLANES = 8
POOL_HALO = 16
SB_BLOCK = 256
SB_QBLOCKS = 4
SB_DEAD_LOG = -110.0
ROW_BLOCK = 512
TOK_BLOCK = 512
XATTN_BLOCKS = 2
XATTN_CHAIN = 1
SORT_BLOCKS = 1
PROJ_BLOCK = 1024
SORT_ROWS = -(-(TOK_BLOCK * TOP_K + N_EXPERTS * SUBLANES) // LANES) * LANES
VMEM_LIMIT = 48 * 1024 * 1024
NEG_BIG = -1e30


def _rms(x, g):
    ms = jnp.mean(x * x, axis=-1, keepdims=True)
    return x * lax.rsqrt(ms + RMS_EPS) * g


def _segment_rows(count):
    return jnp.ceil(jnp.maximum(count, 1) / SUBLANES) * SUBLANES


def _params(*sem):
    return pltpu.CompilerParams(dimension_semantics=sem, vmem_limit_bytes=VMEM_LIMIT)


def _in_proj_kernel(x_ref, g_ref, w_ref, o_ref):
    h = _rms(x_ref[...], g_ref[...]).astype(BF16)
    acc = jnp.dot(h, w_ref[...], preferred_element_type=F32)
    scale = 1.0 / math.sqrt(SB_HEAD_DIM)
    o_ref[:, :SB_WIDTH] = (acc[:, :SB_WIDTH] * scale).astype(BF16)
    o_ref[:, SB_WIDTH:] = acc[:, SB_WIDTH:].astype(BF16)


def _in_proj(x2d, g, w):
    n = x2d.shape[0]
    tm = PROJ_BLOCK
    return pl.pallas_call(
        _in_proj_kernel,
        grid=(n // tm,),
        in_specs=[
            pl.BlockSpec((tm, D_MODEL), lambda i: (i, 0)),
            pl.BlockSpec((1, D_MODEL), lambda i: (0, 0)),
            pl.BlockSpec((D_MODEL, IN_PROJ_COLS), lambda i: (0, 0)),
        ],
        out_specs=pl.BlockSpec((tm, IN_PROJ_COLS), lambda i: (i, 0)),
        out_shape=jax.ShapeDtypeStruct((n, IN_PROJ_COLS), BF16),
        compiler_params=_params("arbitrary"),
        name="in_proj",
    )(x2d, g, w)


def _sb_chain(qh, kb, vb, neg_tri, c, causal):
    z = lax.dot_general(qh, kb, (((1,), (1,)), ((), ())), preferred_element_type=F32)
    zb = z.astype(BF16)
    spb = jnp.maximum(zb, 0.0) + jnp.log(1.0 + jnp.exp(-jnp.abs(zb)))
    if causal is not None:
        spb = jnp.where(causal, spb, 0.0)
    after = jnp.dot(spb, neg_tri, preferred_element_type=F32)
    sp = spb.astype(F32)
    w = jnp.exp((z - sp) + after)
    if causal is not None:
        w = jnp.where(causal, w, 0.0)
    pv = jnp.dot(w.astype(BF16), vb, preferred_element_type=F32) * jnp.exp(c)
    return pv, c + (after[:, :1] - sp[:, :1])


def _sb_kernel(q_ref, k_ref, v_ref, tri_ref, o_ref, acc_ref, c_ref):
    tq = SB_BLOCK
    lane = lax.broadcasted_iota(I32, (tq, LANES), 1)
    neg_tri = tri_ref[...]
    row = lax.broadcasted_iota(I32, (2 * tq, tq), 0)
    col = lax.broadcasted_iota(I32, (2 * tq, tq), 1)
    causal = col < jnp.where(row >= tq, row - tq, row)

    def stacked(u):
        q = q_ref[u * tq:(u + 1) * tq, :]
        zero = jnp.zeros_like(q)
        return jnp.concatenate(
            [jnp.where(lane < SB_HEAD_DIM, q, zero), jnp.where(lane >= SB_HEAD_DIM, q, zero)], axis=0)

    def step(qs, j, c, diagonal):
        off = pl.multiple_of(j * tq, tq)
        return _sb_chain(qs, k_ref[pl.ds(off, tq), :], v_ref[pl.ds(off, tq), :], neg_tri, c,
                         causal if diagonal else None)

    blocks = []
    for u in range(SB_QBLOCKS):
        i = pl.program_id(2) * SB_QBLOCKS + u
        qs = stacked(u)
        pv, c = step(qs, i, jnp.zeros((2 * tq, LANES), F32), True)
        pv_left, c_left = step(qs, jnp.maximum(i - 1, 0), c, False)
        has_left = i > 0
        acc_ref[u] = pv + jnp.where(has_left, pv_left, 0.0)
        c_ref[u] = jnp.where(has_left, c_left, c)
        blocks.append((i, qs))

    @pl.when(jnp.max(c_ref[...]) > SB_DEAD_LOG)
    def _():
        for u, (i, qs) in enumerate(blocks):
            def alive():
                return jnp.max(c_ref[u]) > SB_DEAD_LOG

            def cond(carry):
                j, live = carry
                return jnp.logical_and(j >= 0, live)

            def body(carry):
                j, _ = carry
                pv, c = step(qs, j, c_ref[u], False)
                acc_ref[u] += pv
                c_ref[u] = c
                return j - 1, alive()

            lax.while_loop(cond, body, (i - 2, alive()))

    for u in range(SB_QBLOCKS):
        o_ref[u * tq:(u + 1) * tq, :] = jnp.where(
            lane < SB_HEAD_DIM, acc_ref[u, :tq, :], acc_ref[u, tq:, :]).astype(o_ref.dtype)


def _sb_attention(proj3, neg_tri):
    b, s, _ = proj3.shape
    tq = SB_BLOCK * SB_QBLOCKS
    pairs = SB_WIDTH // LANES
    return pl.pallas_call(
        _sb_kernel,
        grid=(b, pairs, s // tq),
        in_specs=[
            pl.BlockSpec((None, tq, LANES), lambda bi, p, i: (bi, i, p)),
            pl.BlockSpec((None, s, LANES), lambda bi, p, i: (bi, 0, pairs + p)),
            pl.BlockSpec((None, s, LANES), lambda bi, p, i: (bi, 0, 2 * pairs + p)),
            pl.BlockSpec((SB_BLOCK, SB_BLOCK), lambda bi, p, i: (0, 0)),
        ],
        out_specs=pl.BlockSpec((None, tq, LANES), lambda bi, p, i: (bi, i, p)),
        out_shape=jax.ShapeDtypeStruct((b, s, SB_WIDTH), BF16),
        scratch_shapes=[pltpu.VMEM((SB_QBLOCKS, 2 * SB_BLOCK, LANES), F32),
                        pltpu.VMEM((SB_QBLOCKS, 2 * SB_BLOCK, LANES), F32)],
        compiler_params=_params("arbitrary", "arbitrary", "arbitrary"),
        name="sb_attn",
    )(proj3, proj3, proj3, neg_tri)


def _mix_kernel(x_ref, sb_ref, u_ref, halo_ref, wp_ref, ps_ref, gsb_ref, gpool_ref, wout_ref,
                o_ref, *, seq):
    tm = x_ref.shape[0]
    t0 = (pl.program_id(0) * tm) % seq
    u = u_ref[...].astype(F32)
    halo = jnp.where(t0 == 0, 0.0, halo_ref[...].astype(F32))
    a = jnp.concatenate([halo, u], axis=0)
    pos = t0 + lax.broadcasted_iota(I32, (tm, POOL_GROUP_DIM), 0)
    outs = []
    for g, win in enumerate(POOL_WINDOWS):
        lo, hi = g * POOL_GROUP_DIM, (g + 1) * POOL_GROUP_DIM
        s = a[:, lo:hi]
        sh = 1
        while sh < win:
            s = s + pltpu.roll(s, sh, axis=0)
            sh *= 2
        cnt = jnp.minimum(pos + 1, win).astype(F32)
        outs.append(s[POOL_HALO:, :] / cnt - u[:, lo:hi])
    pooled = jnp.concatenate(outs, axis=1).astype(BF16)
    pooled = jnp.dot(pooled, wp_ref[...], preferred_element_type=F32) * ps_ref[...]
    pool_n = _rms(pooled, gpool_ref[...]).astype(BF16)
    sb_n = _rms(sb_ref[...].astype(F32), gsb_ref[...]).astype(BF16)
    y = jnp.dot(sb_n, wout_ref[:SB_WIDTH, :], preferred_element_type=F32)
    y = y + jnp.dot(pool_n, wout_ref[SB_WIDTH:, :], preferred_element_type=F32)
    o_ref[...] = x_ref[...] + y


def _mix_out(x2d, sb2d, proj2d, wp, ps, gsb, gpool, wout, seq):
    n = x2d.shape[0]
    tm = PROJ_BLOCK
    ucol = 3 * SB_WIDTH // POOL_WIDTH
    hb = tm // POOL_HALO
    return pl.pallas_call(
        functools.partial(_mix_kernel, seq=seq),
        grid=(n // tm,),
        in_specs=[
            pl.BlockSpec((tm, D_MODEL), lambda i: (i, 0)),
            pl.BlockSpec((tm, SB_WIDTH), lambda i: (i, 0)),
            pl.BlockSpec((tm, POOL_WIDTH), lambda i: (i, ucol)),
            pl.BlockSpec((POOL_HALO, POOL_WIDTH), lambda i: (jnp.maximum(i * hb - 1, 0), ucol)),
            pl.BlockSpec((POOL_WIDTH, POOL_WIDTH), lambda i: (0, 0)),
            pl.BlockSpec((1, POOL_WIDTH), lambda i: (0, 0)),
            pl.BlockSpec((1, SB_WIDTH), lambda i: (0, 0)),
            pl.BlockSpec((1, POOL_WIDTH), lambda i: (0, 0)),
            pl.BlockSpec((D_MODEL, D_MODEL), lambda i: (0, 0)),
        ],
        out_specs=pl.BlockSpec((tm, D_MODEL), lambda i: (i, 0)),
        out_shape=jax.ShapeDtypeStruct((n, D_MODEL), F32),
        compiler_params=_params("arbitrary"),
        name="mix_out",
    )(x2d, sb2d, proj2d, proj2d, wp, ps, gsb, gpool, wout)


def _mem_kv_kernel(m_ref, g_ref, w_ref, o_ref):
    h = _rms(m_ref[...], g_ref[...]).astype(BF16)
    o_ref[...] = jnp.dot(h, w_ref[...], preferred_element_type=F32).astype(BF16)


def _mem_kv(mem, g, w):
    b = mem.shape[0]
    return pl.pallas_call(
        _mem_kv_kernel,
        grid=(b,),
        in_specs=[
            pl.BlockSpec((None, N_MEM, D_MODEL), lambda i: (i, 0, 0)),
            pl.BlockSpec((1, D_MODEL), lambda i: (0, 0)),
            pl.BlockSpec((D_MODEL, 2 * D_MODEL), lambda i: (0, 0)),
        ],
        out_specs=pl.BlockSpec((None, N_MEM, 2 * D_MODEL), lambda i: (i, 0, 0)),
        out_shape=jax.ShapeDtypeStruct((b, N_MEM, 2 * D_MODEL), BF16),
        compiler_params=_params("arbitrary"),
        name="mem_kv",
    )(mem, g, w)


def _xattn_block(x, kv_ref, gq, wq_ref, wo_ref, gf, wr_ref, br):
    tm = x.shape[0]
    hq = _rms(x, gq).astype(BF16)
    q = jnp.dot(hq, wq_ref[...], preferred_element_type=F32) * (1.0 / math.sqrt(MEM_HEAD_DIM))
    q = q.astype(BF16)
    outs = []
    for h in range(MEM_HEADS):
        lo, hi = h * MEM_HEAD_DIM, (h + 1) * MEM_HEAD_DIM
        s = lax.dot_general(q[:, lo:hi], kv_ref[:, lo:hi], (((1,), (1,)), ((), ())),
                            preferred_element_type=F32)
        p = jnp.exp(s - jnp.max(s, axis=-1, keepdims=True))
        denom = jnp.sum(p, axis=-1, keepdims=True)
        o = jnp.dot(p.astype(BF16), kv_ref[:, D_MODEL + lo:D_MODEL + hi], preferred_element_type=F32)
        outs.append(o / denom)
    o = jnp.concatenate(outs, axis=1).astype(BF16)
    x2 = x + jnp.dot(o, wo_ref[...], preferred_element_type=F32)
    hf = _rms(x2, gf)

    h_hi = hf.astype(BF16)
    h_lo = (hf - h_hi.astype(F32)).astype(BF16)
    hw = jnp.dot(h_hi, wr_ref[...], preferred_element_type=F32)
    lw = jnp.dot(h_lo, wr_ref[:, :LANES], preferred_element_type=F32)
    logits = (hw[:, :LANES] + (hw[:, LANES:] + lw)) + br
    return x2, hf.astype(BF16), logits


def _route_block(logits, tri):
    tm = logits.shape[0]
    lane = lax.broadcasted_iota(I32, (tm, LANES), 1).astype(F32)
    work = logits
    vals, idxs, sels = [], [], []
    for _ in range(TOP_K):
        m = jnp.max(work, axis=-1, keepdims=True)
        idx = jnp.min(jnp.where(work == m, lane, float(LANES)), axis=-1, keepdims=True)
        sel = lane == idx
        vals.append(m)
        idxs.append(idx)
        sels.append(sel)
        work = jnp.where(sel, -3e38, work)
    exps = [jnp.exp(v - vals[0]) for v in vals]
    den = exps[0] + exps[1] + exps[2] + exps[3]

    onehot = jnp.zeros((tm, LANES), F32)
    for sel in sels:
        onehot = onehot + sel.astype(F32)
    blocks = [onehot[r:r + TOK_BLOCK, :] for r in range(0, tm, TOK_BLOCK)]
    before = jnp.concatenate(
        [jnp.dot(tri, blk.astype(BF16), preferred_element_type=F32) for blk in blocks], axis=0)
    counts = [jnp.sum(blk, axis=0, keepdims=True) for blk in blocks]
    upper = (lax.broadcasted_iota(I32, (LANES, LANES), 0)
             < lax.broadcasted_iota(I32, (LANES, LANES), 1)).astype(BF16)
    starts = []
    for cnt in counts:
        seg = jnp.where(lane[:SUBLANES, :] < float(N_EXPERTS), _segment_rows(cnt), 0.0)
        seg = jnp.broadcast_to(seg, (SUBLANES, LANES)).astype(BF16)
        start = jnp.dot(seg, upper, preferred_element_type=F32)[:1, :]
        starts.append(jnp.broadcast_to(start, (TOK_BLOCK, LANES)))
    first_row = before + jnp.concatenate(starts, axis=0)
    meta = jnp.zeros((tm, LANES), F32)
    gates = jnp.zeros((tm, LANES), F32)
    for k in range(TOP_K):
        pos = jnp.sum(jnp.where(sels[k], first_row, 0.0), axis=-1, keepdims=True)
        meta = jnp.where(lane == float(k), idxs[k], meta)
        meta = jnp.where(lane == float(TOP_K + k), pos, meta)
        gates = jnp.where(lane == float(k), exps[k] / den, gates)
    return meta.astype(I32), gates, [c.astype(I32) for c in counts]


def _xattn_kernel(x_ref, kv_ref, gq_ref, wq_ref, wo_ref, gf_ref, wr_ref, br_ref, tri_ref,
                  x2_ref, h_ref, meta_ref, meta_t_ref, gate_ref, cnt_ref):
    tb = TOK_BLOCK * XATTN_CHAIN
    chains = []
    for chain in range(XATTN_BLOCKS // XATTN_CHAIN):
        rows = slice(chain * tb, (chain + 1) * tb)
        x2, hf, logits = _xattn_block(
            x_ref[rows, :], kv_ref, gq_ref[...], wq_ref, wo_ref, gf_ref[...], wr_ref, br_ref[...])
        x2_ref[rows, :] = x2
        h_ref[rows, :] = hf
        chains.append((rows, logits))
    for chain, (rows, logits) in enumerate(chains):
        meta, gates, counts = _route_block(logits, tri_ref[...])
        meta_ref[rows, :] = meta
        meta_t_ref[:, rows] = meta.T[:SUBLANES, :]
        gate_ref[rows, :] = gates
        for r, c in enumerate(counts):
            cnt_ref[chain * XATTN_CHAIN + r] = c


def _xattn_router(x1, kv, gq, wq, wo, gf, wr, br, tri, seq):
    n = x1.shape[0]
    tm = TOK_BLOCK * XATTN_BLOCKS
    per_seq = seq // tm
    const = lambda i: (0, 0)
    return pl.pallas_call(
        _xattn_kernel,
        grid=(n // tm,),
        in_specs=[
            pl.BlockSpec((tm, D_MODEL), lambda i: (i, 0)),
            pl.BlockSpec((None, N_MEM, 2 * D_MODEL), lambda i: (i // per_seq, 0, 0)),
            pl.BlockSpec((1, D_MODEL), const),
            pl.BlockSpec((D_MODEL, D_MODEL), const),
            pl.BlockSpec((D_MODEL, D_MODEL), const),
            pl.BlockSpec((1, D_MODEL), const),
            pl.BlockSpec((D_MODEL, 2 * LANES), const),
            pl.BlockSpec((1, LANES), const),
            pl.BlockSpec((TOK_BLOCK, TOK_BLOCK), const),
        ],
        out_specs=[
            pl.BlockSpec((tm, D_MODEL), lambda i: (i, 0)),
            pl.BlockSpec((tm, D_MODEL), lambda i: (i, 0)),
            pl.BlockSpec((tm, LANES), lambda i: (i, 0)),
            pl.BlockSpec((SUBLANES, tm), lambda i: (0, i)),
            pl.BlockSpec((tm, LANES), lambda i: (i, 0)),
            pl.BlockSpec((XATTN_BLOCKS, 1, LANES), lambda i: (i, 0, 0)),
        ],
        out_shape=[
            jax.ShapeDtypeStruct((n, D_MODEL), F32),
            jax.ShapeDtypeStruct((n, D_MODEL), BF16),
            jax.ShapeDtypeStruct((n, LANES), I32),
            jax.ShapeDtypeStruct((SUBLANES, n), I32),
            jax.ShapeDtypeStruct((n, LANES), F32),
            jax.ShapeDtypeStruct((n // TOK_BLOCK, 1, LANES), I32),
        ],
        compiler_params=_params("arbitrary"),
        name="xattn_router",
    )(x1, kv, gq, wq, wo, gf, wr, br, tri)


def _segment_copy(tables, step, e, vmem_ref, hbm_ref, sem, to_hbm):
    src_ref, dst_ref, cnt_ref = tables
    at = step * N_EXPERTS + e
    n = pl.multiple_of(cnt_ref[at], SUBLANES)
    local = vmem_ref.at[pl.ds(pl.multiple_of(src_ref[at], SUBLANES), n), :]
    remote = hbm_ref.at[pl.ds(pl.multiple_of(dst_ref[at], SUBLANES), n), :]
    return pltpu.make_async_copy(local, remote, sem) if to_hbm else pltpu.make_async_copy(remote, local, sem)


def _each_segment(fn):
    for r in range(SORT_BLOCKS):
        for e in range(N_EXPERTS):
            fn(r, e)


def _dispatch_kernel(src_ref, dst_ref, cnt_ref, fill_ref, h_ref, pos_ref, xs_ref, stage_ref,
                     zero_ref, sem, fill_sem):
    tables = (src_ref, dst_ref, cnt_ref)
    step = pl.program_id(0)
    last = pl.num_programs(0) - 1
    parity = step % 2
    where = lax.broadcasted_iota(I32, (SORT_ROWS, TOK_BLOCK), 0)
    for r in range(SORT_BLOCKS):
        rows = slice(r * TOK_BLOCK, (r + 1) * TOK_BLOCK)
        pos = pos_ref[:, rows]
        hit = where == pos[TOP_K:TOP_K + 1, :]
        for k in range(1, TOP_K):
            hit = jnp.logical_or(hit, where == pos[TOP_K + k:TOP_K + k + 1, :])
        onehot = jnp.where(hit, 1.0, 0.0).astype(BF16)
        stage_ref[parity * SORT_BLOCKS + r] = jnp.dot(onehot, h_ref[rows, :],
                                                      preferred_element_type=F32)

    def copy(at_step, at_parity, r, e):
        return _segment_copy(tables, at_step * SORT_BLOCKS + r, e,
                             stage_ref.at[at_parity * SORT_BLOCKS + r], xs_ref, sem.at[at_parity], True)

    _each_segment(lambda r, e: copy(step, parity, r, e).start())

    @pl.when(step > 0)
    def _():
        _each_segment(lambda r, e: copy(step - 1, 1 - parity, r, e).wait())

    @pl.when(step == last)
    def _():
        _each_segment(lambda r, e: copy(step, parity, r, e).wait())

    def pad_copy(e):
        n = pl.multiple_of(fill_ref[N_EXPERTS + e], SUBLANES)
        at = pl.multiple_of(fill_ref[e], SUBLANES)
        return pltpu.make_async_copy(zero_ref.at[pl.ds(0, n), :], xs_ref.at[pl.ds(at, n), :], fill_sem)

    def tail_copy(t):
        at = pl.multiple_of(fill_ref[2 * N_EXPERTS] + t * ROW_BLOCK, ROW_BLOCK)
        return pltpu.make_async_copy(zero_ref, xs_ref.at[pl.ds(at, ROW_BLOCK), :], fill_sem)

    def each_pad(fn):
        def body(e, carry):
            @pl.when(fill_ref[N_EXPERTS + e] > 0)
            def _():
                fn(e)
            return carry
        lax.fori_loop(0, N_EXPERTS, body, 0)

    def each_tail(fn):
        def body(t, carry):
            fn(t)
            return carry
        lax.fori_loop(0, fill_ref[2 * N_EXPERTS + 1], body, 0)

    @pl.when(step == 0)
    def _():
        zero_ref[...] = jnp.zeros_like(zero_ref)
        each_pad(lambda e: pad_copy(e).start())
        each_tail(lambda t: tail_copy(t).start())

    @pl.when(step == last)
    def _():
        each_pad(lambda e: pad_copy(e).wait())
        each_tail(lambda t: tail_copy(t).wait())


def _dispatch(tables, fill, hf, pos, n_rows):
    n = hf.shape[0]
    tm = TOK_BLOCK * SORT_BLOCKS
    grid_spec = pltpu.PrefetchScalarGridSpec(
        num_scalar_prefetch=4,
        grid=(n // tm,),
        in_specs=[
            pl.BlockSpec((tm, D_MODEL), lambda i, *_: (i, 0)),
            pl.BlockSpec((SUBLANES, tm), lambda i, *_: (0, i)),
        ],
        out_specs=pl.BlockSpec(memory_space=pl.ANY),
        scratch_shapes=[pltpu.VMEM((2 * SORT_BLOCKS, SORT_ROWS, D_MODEL), F32),
                        pltpu.VMEM((ROW_BLOCK, D_MODEL), F32),
                        pltpu.SemaphoreType.DMA((2,)), pltpu.SemaphoreType.DMA],
    )
    return pl.pallas_call(
        _dispatch_kernel,
        grid_spec=grid_spec,
        out_shape=jax.ShapeDtypeStruct((n_rows, D_MODEL), F32),
        compiler_params=_params("arbitrary"),
        name="dispatch",
    )(*tables, fill, hf, pos)


def _expert_kernel(blk_e_ref, next_e_ref, n_used_ref, xs_ref, w1_hbm, b1_ref, w2_hbm, b2_ref,
                   ys_ref, w1f_ref, w2f_ref, w1b_ref, w2b_ref, sem):
    i = pl.program_id(0)
    used = i < n_used_ref[0]

    def fetch(e):
        return (pltpu.make_async_copy(w1_hbm.at[e], w1f_ref, sem.at[0]),
                pltpu.make_async_copy(w2_hbm.at[e], w2f_ref, sem.at[1]))

    @pl.when(used)
    def _():
        e = blk_e_ref[i]

        @pl.when(i == 0)
        def _():
            for copy in fetch(e):
                copy.start()

        @pl.when(jnp.logical_or(i == 0, e != blk_e_ref[jnp.maximum(i - 1, 0)]))
        def _():
            for copy in fetch(e):
                copy.wait()
            w1b_ref[...] = w1f_ref[...].astype(BF16)
            w2b_ref[...] = w2f_ref[...].astype(BF16)

            @pl.when(next_e_ref[i] >= 0)
            def _():
                for copy in fetch(next_e_ref[i]):
                    copy.start()

        gu = jnp.dot(xs_ref[...].astype(BF16), w1b_ref[...], preferred_element_type=F32) + b1_ref[...]
        gate = jnp.minimum(gu[:, :D_FF], SWIGLU_LIMIT)
        up = jnp.clip(gu[:, D_FF:], -SWIGLU_LIMIT, SWIGLU_LIMIT)
        act = (up + 1.0) * gate * jax.nn.sigmoid(SWIGLU_ALPHA * gate)
        ys_ref[...] = jnp.dot(act.astype(BF16), w2b_ref[...], preferred_element_type=F32) + b2_ref[...]


def _experts(blk_e, next_e, n_used, xs, w1, b1, w2, b2):
    n_rows = xs.shape[0]
    bm = ROW_BLOCK
    last_used = lambda i, nu: jnp.maximum(jnp.minimum(i, nu[0] - 1), 0)
    row = lambda i, be, ne, nu: (last_used(i, nu), 0)
    exp3 = lambda i, be, ne, nu: (be[last_used(i, nu)], 0, 0)
    grid_spec = pltpu.PrefetchScalarGridSpec(
        num_scalar_prefetch=3,
        grid=(n_rows // bm,),
        in_specs=[
            pl.BlockSpec((bm, D_MODEL), row),
            pl.BlockSpec(memory_space=pl.ANY),
            pl.BlockSpec((None, 1, 2 * D_FF), exp3),
            pl.BlockSpec(memory_space=pl.ANY),
            pl.BlockSpec((None, 1, D_MODEL), exp3),
        ],
        out_specs=pl.BlockSpec((bm, D_MODEL), row),
        scratch_shapes=[pltpu.VMEM((D_MODEL, 2 * D_FF), F32), pltpu.VMEM((D_FF, D_MODEL), F32),
                        pltpu.VMEM((D_MODEL, 2 * D_FF), BF16), pltpu.VMEM((D_FF, D_MODEL), BF16),
                        pltpu.SemaphoreType.DMA((2,))],
    )
    return pl.pallas_call(
        _expert_kernel,
        grid_spec=grid_spec,
        out_shape=jax.ShapeDtypeStruct((n_rows, D_MODEL), F32),
        input_output_aliases={3: 0},
        compiler_params=_params("arbitrary"),
        name="experts",
    )(blk_e, next_e, n_used, xs, w1, b1, w2, b2)


def _combine_kernel(src_ref, dst_ref, cnt_ref, x_ref, gate_ref, pos_ref, g_ref, ys_ref, o_ref,
                    stage_ref, sem):
    tables = (src_ref, dst_ref, cnt_ref)
    step = pl.program_id(0)
    last = pl.num_programs(0) - 1
    parity = step % 2

    def copy(at_step, at_parity, r, e):
        return _segment_copy(tables, at_step * SORT_BLOCKS + r, e,
                             stage_ref.at[at_parity * SORT_BLOCKS + r], ys_ref, sem.at[at_parity], False)

    @pl.when(step == 0)
    def _():
        stage_ref[...] = jnp.zeros_like(stage_ref)
        _each_segment(lambda r, e: copy(step, parity, r, e).start())

    @pl.when(step < last)
    def _():
        _each_segment(lambda r, e: copy(step + 1, 1 - parity, r, e).start())

    _each_segment(lambda r, e: copy(step, parity, r, e).wait())

    where = lax.broadcasted_iota(I32, (TOK_BLOCK, SORT_ROWS), 1)
    for r in range(SORT_BLOCKS):
        rows = slice(r * TOK_BLOCK, (r + 1) * TOK_BLOCK)
        pos = pos_ref[rows, :]
        gates = gate_ref[rows, :]
        weights = jnp.zeros((TOK_BLOCK, SORT_ROWS), F32)
        for k in range(TOP_K):
            weights = jnp.where(where == pos[:, TOP_K + k:TOP_K + k + 1], gates[:, k:k + 1], weights)
        y = x_ref[rows, :] + jnp.dot(weights.astype(BF16),
                                     stage_ref[parity * SORT_BLOCKS + r].astype(BF16),
                                     preferred_element_type=F32)
        o_ref[rows, :] = _rms(y, g_ref[...])


def _combine(tables, x2, gates, pos, g_final, ys):
    n = x2.shape[0]
    tc = TOK_BLOCK * SORT_BLOCKS
    grid_spec = pltpu.PrefetchScalarGridSpec(
        num_scalar_prefetch=3,
        grid=(n // tc,),
        in_specs=[
            pl.BlockSpec((tc, D_MODEL), lambda i, *_: (i, 0)),
            pl.BlockSpec((tc, LANES), lambda i, *_: (i, 0)),
            pl.BlockSpec((tc, LANES), lambda i, *_: (i, 0)),
            pl.BlockSpec((1, D_MODEL), lambda i, *_: (0, 0)),
            pl.BlockSpec(memory_space=pl.ANY),
        ],
        out_specs=pl.BlockSpec((tc, D_MODEL), lambda i, *_: (i, 0)),
        scratch_shapes=[pltpu.VMEM((2 * SORT_BLOCKS, SORT_ROWS, D_MODEL), F32),
                        pltpu.SemaphoreType.DMA((2,))],
    )
    return pl.pallas_call(
        _combine_kernel,
        grid_spec=grid_spec,
        out_shape=jax.ShapeDtypeStruct((n, D_MODEL), F32),
        compiler_params=_params("arbitrary"),
        name="combine",
    )(*tables, x2, gates, pos, g_final, ys)


def _strict_lower(n):
    r = lax.broadcasted_iota(I32, (n, n), 0)
    c = lax.broadcasted_iota(I32, (n, n), 1)
    return (r > c).astype(BF16)


def _round_up(v, m):
    return (v + m - 1) // m * m


def _routing_tables(cnt, n):
    bm = ROW_BLOCK
    n_tok_blk = n // TOK_BLOCK
    n_rows = _round_up(n * TOP_K + n_tok_blk * N_EXPERTS * SUBLANES, bm) + N_EXPERTS * bm
    n_blk = n_rows // bm
    seg = _segment_rows(cnt[:, 0, :N_EXPERTS]).astype(I32)
    local_start = jnp.cumsum(seg, axis=1) - seg
    total = jnp.sum(seg, axis=0)
    padded = _round_up(total, bm)
    pend = jnp.cumsum(padded)
    pstart = pend - padded
    global_start = pstart[None, :] + jnp.cumsum(seg, axis=0) - seg
    tables = (local_start.reshape(-1), global_start.reshape(-1), seg.reshape(-1))

    n_used = (pend[-1:] // bm).astype(I32)
    blk_row = jnp.arange(n_blk, dtype=I32) * bm
    blk_e = jnp.minimum(jnp.sum(blk_row[:, None] >= pend[None, :], axis=1), N_EXPERTS - 1).astype(I32)
    ids = jnp.arange(N_EXPERTS, dtype=I32)
    later = jnp.where((ids[None, :] > ids[:, None]) & (padded[None, :] > 0), ids[None, :], N_EXPERTS)
    follow = jnp.min(later, axis=1)
    follow = jnp.where(follow < N_EXPERTS, follow, -1)
    next_e = jnp.sum(jnp.where(blk_e[:, None] == ids[None, :], follow[None, :], 0), axis=1).astype(I32)
    fill = jnp.concatenate([pstart + total, padded - total, pend[-1:], n_blk - pend[-1:] // bm])
    return tables, fill.astype(I32), blk_e, next_e, n_used, n_rows


def _layer(x2d, mem, b, s, g_mix, w_in, g_sb_out, g_pool_out, w_pool, pool_scale, w_out,
           g_mem_q, g_mem_kv, w_mem_q, w_mem_kv, w_mem_o, g_ffn, w_router, b_router,
           w_e_in, b_e_in, w_e_out, b_e_out, g_out):
    n = b * s
    row = lambda v: v.reshape(1, -1).astype(F32)

    proj = _in_proj(x2d, row(g_mix), w_in.astype(BF16))
    sb = _sb_attention(proj.reshape(b, s, IN_PROJ_COLS), -_strict_lower(SB_BLOCK))
    groups = len(POOL_WINDOWS)
    wp = (w_pool.astype(BF16)[:, :, None, :] * jnp.eye(groups, dtype=BF16)[:, None, :, None])
    wp = wp.reshape(POOL_WIDTH, POOL_WIDTH)
    x1 = _mix_out(x2d, sb.reshape(n, SB_WIDTH), proj, wp, row(pool_scale),
                  row(g_sb_out), row(g_pool_out), w_out.astype(BF16), s)

    kv = _mem_kv(mem, row(g_mem_kv), w_mem_kv.astype(BF16))
    spare = LANES - N_EXPERTS
    wr = jnp.pad(w_router.astype(F32), ((0, 0), (0, spare)))
    wr_hi = wr.astype(BF16)
    wr_lo = (wr - wr_hi.astype(F32)).astype(BF16)
    br = jnp.pad(b_router.astype(F32).reshape(1, -1), ((0, 0), (0, spare)), constant_values=NEG_BIG)
    x2, hf, meta, meta_t, gates, cnt = _xattn_router(
        x1, kv, row(g_mem_q), w_mem_q.astype(BF16), w_mem_o.astype(BF16), row(g_ffn),
        jnp.concatenate([wr_hi, wr_lo], axis=1), br, _strict_lower(TOK_BLOCK), s)

    tables, fill, blk_e, next_e, n_used, n_rows = _routing_tables(cnt, n)
    xs = _dispatch(tables, fill, hf, meta_t, n_rows)
    ys = _experts(blk_e, next_e, n_used, xs, w_e_in, b_e_in.reshape(N_EXPERTS, 1, -1),
                  w_e_out, b_e_out.reshape(N_EXPERTS, 1, -1))
    return _combine(tables, x2, gates, meta, row(g_out), ys)


def kernel(x, mem, g_mix, w_in, g_sb_out, g_pool_out, w_pool, pool_scale, w_out, g_mem_q, g_mem_kv,
           w_mem_q, w_mem_kv, w_mem_o, g_ffn, w_router, b_router, w_expert_in, b_expert_in,
           w_expert_out, b_expert_out, g_final):
    b, s, d = x.shape
    depth = g_mix.shape[0]
    assert d == D_MODEL and depth == 1, "the final RMSNorm is fused into the single layer's combine"
    assert s % PROJ_BLOCK == 0 and s % (SB_BLOCK * SB_QBLOCKS) == 0 and s % (TOK_BLOCK * XATTN_BLOCKS) == 0
    assert (b * s) % (TOK_BLOCK * SORT_BLOCKS) == 0
    out = _layer(x.reshape(b * s, d), mem, b, s, g_mix[0], w_in[0], g_sb_out[0], g_pool_out[0],
                 w_pool[0], pool_scale[0], w_out[0], g_mem_q[0], g_mem_kv[0], w_mem_q[0],
                 w_mem_kv[0], w_mem_o[0], g_ffn[0], w_router[0], b_router[0], w_expert_in[0],
                 b_expert_in[0], w_expert_out[0], b_expert_out[0], g_final)
    return out.reshape(b, s, d)
```

```python
import functools
import math

import jax
import jax.numpy as jnp
from jax import lax
from jax.experimental import pallas as pl
from jax.experimental.pallas import tpu as pltpu

F32 = jnp.float32
BF16 = jnp.bfloat16
I32 = jnp.int32

D_MODEL = 1024
SB_HEADS = 8
SB_HEAD_DIM = 64
SB_WIDTH = SB_HEADS * SB_HEAD_DIM
POOL_WINDOWS = (2, 4, 8, 16)
POOL_GROUP_DIM = 128
POOL_WIDTH = len(POOL_WINDOWS) * POOL_GROUP_DIM
IN_PROJ_COLS = 3 * SB_WIDTH + POOL_WIDTH
N_MEM = 256
MEM_HEADS = 4
MEM_HEAD_DIM = D_MODEL // MEM_HEADS
N_EXPERTS = 32
TOP_K = 4
D_FF = D_MODEL
SWIGLU_LIMIT = 7.0
SWIGLU_ALPHA = 1.702
RMS_EPS = 1e-5

LANES = 128
SUBLANES = 8
POOL_HALO = 16
SB_BLOCK = 256
SB_QBLOCKS = 8
SB_DEAD_LOG = -110.0
ROW_BLOCK = 512
TOK_BLOCK = 512
XATTN_BLOCKS = 2
XATTN_CHAIN = 1
SORT_BLOCKS = 1
PROJ_BLOCK = 1024
SORT_ROWS = -(-(TOK_BLOCK * TOP_K + N_EXPERTS * SUBLANES) // LANES) * LANES
VMEM_LIMIT = 48 * 1024 * 1024
NEG_BIG = -1e30


def _rms(x, g):
    ms = jnp.mean(x * x, axis=-1, keepdims=True)
    return x * lax.rsqrt(ms + RMS_EPS) * g


def _segment_rows(count):
    return jnp.ceil(jnp.maximum(count, 1) / SUBLANES) * SUBLANES


def _params(*sem):
    return pltpu.CompilerParams(dimension_semantics=sem, vmem_limit_bytes=VMEM_LIMIT)


def _in_proj_kernel(x_ref, g_ref, w_ref, o_ref):
    h = _rms(x_ref[...], g_ref[...]).astype(BF16)
    acc = jnp.dot(h, w_ref[...], preferred_element_type=F32)
    scale = 1.0 / math.sqrt(SB_HEAD_DIM)
    o_ref[:, :SB_WIDTH] = (acc[:, :SB_WIDTH] * scale).astype(BF16)
    o_ref[:, SB_WIDTH:] = acc[:, SB_WIDTH:].astype(BF16)


def _in_proj(x2d, g, w):
    n = x2d.shape[0]
    tm = PROJ_BLOCK
    return pl.pallas_call(
        _in_proj_kernel,
        grid=(n // tm,),
        in_specs=[
            pl.BlockSpec((tm, D_MODEL), lambda i: (i, 0)),
            pl.BlockSpec((1, D_MODEL), lambda i: (0, 0)),
            pl.BlockSpec((D_MODEL, IN_PROJ_COLS), lambda i: (0, 0)),
        ],
        out_specs=pl.BlockSpec((tm, IN_PROJ_COLS), lambda i: (i, 0)),
        out_shape=jax.ShapeDtypeStruct((n, IN_PROJ_COLS), BF16),
        compiler_params=_params("arbitrary"),
        name="in_proj",
    )(x2d, g, w)


def _sb_chain(qh, kb, vb, neg_tri, c, causal):
    z = lax.dot_general(qh, kb, (((1,), (1,)), ((), ())), preferred_element_type=F32)
    zb = z.astype(BF16)
    spb = jnp.maximum(zb, 0.0) + jnp.log(1.0 + jnp.exp(-jnp.abs(zb)))
    if causal is not None:
        spb = jnp.where(causal, spb, 0.0)
    after = jnp.dot(spb, neg_tri, preferred_element_type=F32)
    sp = spb.astype(F32)
    w = jnp.exp((z - sp) + after)
    if causal is not None:
        w = jnp.where(causal, w, 0.0)
    pv = jnp.dot(w.astype(BF16), vb, preferred_element_type=F32) * jnp.exp(c)
    return pv, c + (after[:, :1] - sp[:, :1])


def _sb_kernel(q_ref, k_ref, v_ref, tri_ref, o_ref, acc_ref, c_ref):
    tq = SB_BLOCK
    lane = lax.broadcasted_iota(I32, (tq, LANES), 1)
    neg_tri = tri_ref[...]
    row = lax.broadcasted_iota(I32, (2 * tq, tq), 0)
    col = lax.broadcasted_iota(I32, (2 * tq, tq), 1)
    causal = col < jnp.where(row >= tq, row - tq, row)

    def stacked(u):
        q = q_ref[u * tq:(u + 1) * tq, :]
        zero = jnp.zeros_like(q)
        return jnp.concatenate(
            [jnp.where(lane < SB_HEAD_DIM, q, zero), jnp.where(lane >= SB_HEAD_DIM, q, zero)], axis=0)

    def step(qs, j, c, diagonal):
        off = pl.multiple_of(j * tq, tq)
        return _sb_chain(qs, k_ref[pl.ds(off, tq), :], v_ref[pl.ds(off, tq), :], neg_tri, c,
                         causal if diagonal else None)

    blocks = []
    for u in range(SB_QBLOCKS):
        i = pl.program_id(2) * SB_QBLOCKS + u
        qs = stacked(u)
        pv, c = step(qs, i, jnp.zeros((2 * tq, LANES), F32), True)
        pv_left, c_left = step(qs, jnp.maximum(i - 1, 0), c, False)
        has_left = i > 0
        acc_ref[u] = pv + jnp.where(has_left, pv_left, 0.0)
        c_ref[u] = jnp.where(has_left, c_left, c)
        blocks.append((i, qs))

    @pl.when(jnp.max(c_ref[...]) > SB_DEAD_LOG)
    def _():
        for u, (i, qs) in enumerate(blocks):
            def alive():
                return jnp.max(c_ref[u]) > SB_DEAD_LOG

            def cond(carry):
                j, live = carry
                return jnp.logical_and(j >= 0, live)

            def body(carry):
                j, _ = carry
                pv, c = step(qs, j, c_ref[u], False)
                acc_ref[u] += pv
                c_ref[u] = c
                return j - 1, alive()

            lax.while_loop(cond, body, (i - 2, alive()))

    for u in range(SB_QBLOCKS):
        o_ref[u * tq:(u + 1) * tq, :] = jnp.where(
            lane < SB_HEAD_DIM, acc_ref[u, :tq, :], acc_ref[u, tq:, :]).astype(o_ref.dtype)


def _sb_attention(proj3, neg_tri):
    b, s, _ = proj3.shape
    tq = SB_BLOCK * SB_QBLOCKS
    pairs = SB_WIDTH // LANES
    return pl.pallas_call(
        _sb_kernel,
        grid=(b, pairs, s // tq),
        in_specs=[
            pl.BlockSpec((None, tq, LANES), lambda bi, p, i: (bi, i, p)),
            pl.BlockSpec((None, s, LANES), lambda bi, p, i: (bi, 0, pairs + p)),
            pl.BlockSpec((None, s, LANES), lambda bi, p, i: (bi, 0, 2 * pairs + p)),
            pl.BlockSpec((SB_BLOCK, SB_BLOCK), lambda bi, p, i: (0, 0)),
        ],
        out_specs=pl.BlockSpec((None, tq, LANES), lambda bi, p, i: (bi, i, p)),
        out_shape=jax.ShapeDtypeStruct((b, s, SB_WIDTH), BF16),
        scratch_shapes=[pltpu.VMEM((SB_QBLOCKS, 2 * SB_BLOCK, LANES), F32),
                        pltpu.VMEM((SB_QBLOCKS, 2 * SB_BLOCK, LANES), F32)],
        compiler_params=_params("arbitrary", "arbitrary", "arbitrary"),
        name="sb_attn",
    )(proj3, proj3, proj3, neg_tri)


def _mix_kernel(x_ref, sb_ref, u_ref, halo_ref, wp_ref, ps_ref, gsb_ref, gpool_ref, wout_ref,
                o_ref, *, seq):
    tm = x_ref.shape[0]
    t0 = (pl.program_id(0) * tm) % seq
    u = u_ref[...].astype(F32)
    halo = jnp.where(t0 == 0, 0.0, halo_ref[...].astype(F32))
    a = jnp.concatenate([halo, u], axis=0)
    pos = t0 + lax.broadcasted_iota(I32, (tm, POOL_GROUP_DIM), 0)
    outs = []
    for g, win in enumerate(POOL_WINDOWS):
        lo, hi = g * POOL_GROUP_DIM, (g + 1) * POOL_GROUP_DIM
        s = a[:, lo:hi]
        sh = 1
        while sh < win:
            s = s + pltpu.roll(s, sh, axis=0)
            sh *= 2
        cnt = jnp.minimum(pos + 1, win).astype(F32)
        outs.append(s[POOL_HALO:, :] / cnt - u[:, lo:hi])
    pooled = jnp.concatenate(outs, axis=1).astype(BF16)
    pooled = jnp.dot(pooled, wp_ref[...], preferred_element_type=F32) * ps_ref[...]
    pool_n = _rms(pooled, gpool_ref[...]).astype(BF16)
    sb_n = _rms(sb_ref[...].astype(F32), gsb_ref[...]).astype(BF16)
    y = jnp.dot(sb_n, wout_ref[:SB_WIDTH, :], preferred_element_type=F32)
    y = y + jnp.dot(pool_n, wout_ref[SB_WIDTH:, :], preferred_element_type=F32)
    o_ref[...] = x_ref[...] + y


def _mix_out(x2d, sb2d, proj2d, wp, ps, gsb, gpool, wout, seq):
    n = x2d.shape[0]
    tm = PROJ_BLOCK
    ucol = 3 * SB_WIDTH // POOL_WIDTH
    hb = tm // POOL_HALO
    return pl.pallas_call(
        functools.partial(_mix_kernel, seq=seq),
        grid=(n // tm,),
        in_specs=[
            pl.BlockSpec((tm, D_MODEL), lambda i: (i, 0)),
            pl.BlockSpec((tm, SB_WIDTH), lambda i: (i, 0)),
            pl.BlockSpec((tm, POOL_WIDTH), lambda i: (i, ucol)),
            pl.BlockSpec((POOL_HALO, POOL_WIDTH), lambda i: (jnp.maximum(i * hb - 1, 0), ucol)),
            pl.BlockSpec((POOL_WIDTH, POOL_WIDTH), lambda i: (0, 0)),
            pl.BlockSpec((1, POOL_WIDTH), lambda i: (0, 0)),
            pl.BlockSpec((1, SB_WIDTH), lambda i: (0, 0)),
            pl.BlockSpec((1, POOL_WIDTH), lambda i: (0, 0)),
            pl.BlockSpec((D_MODEL, D_MODEL), lambda i: (0, 0)),
        ],
        out_specs=pl.BlockSpec((tm, D_MODEL), lambda i: (i, 0)),
        out_shape=jax.ShapeDtypeStruct((n, D_MODEL), F32),
        compiler_params=_params("arbitrary"),
        name="mix_out",
    )(x2d, sb2d, proj2d, proj2d, wp, ps, gsb, gpool, wout)


def _mem_kv_kernel(m_ref, g_ref, w_ref, o_ref):
    h = _rms(m_ref[...], g_ref[...]).astype(BF16)
    o_ref[...] = jnp.dot(h, w_ref[...], preferred_element_type=F32).astype(BF16)


def _mem_kv(mem, g, w):
    b = mem.shape[0]
    return pl.pallas_call(
        _mem_kv_kernel,
        grid=(b,),
        in_specs=[
            pl.BlockSpec((None, N_MEM, D_MODEL), lambda i: (i, 0, 0)),
            pl.BlockSpec((1, D_MODEL), lambda i: (0, 0)),
            pl.BlockSpec((D_MODEL, 2 * D_MODEL), lambda i: (0, 0)),
        ],
        out_specs=pl.BlockSpec((None, N_MEM, 2 * D_MODEL), lambda i: (i, 0, 0)),
        out_shape=jax.ShapeDtypeStruct((b, N_MEM, 2 * D_MODEL), BF16),
        compiler_params=_params("arbitrary"),
        name="mem_kv",
    )(mem, g, w)


def _xattn_block(x, kv_ref, gq, wq_ref, wo_ref, gf, wr_ref, br):
    tm = x.shape[0]
    hq = _rms(x, gq).astype(BF16)
    q = jnp.dot(hq, wq_ref[...], preferred_element_type=F32) * (1.0 / math.sqrt(MEM_HEAD_DIM))
    q = q.astype(BF16)
    outs = []
    for h in range(MEM_HEADS):
        lo, hi = h * MEM_HEAD_DIM, (h + 1) * MEM_HEAD_DIM
        s = lax.dot_general(q[:, lo:hi], kv_ref[:, lo:hi], (((1,), (1,)), ((), ())),
                            preferred_element_type=F32)
        p = jnp.exp(s - jnp.max(s, axis=-1, keepdims=True))
        denom = jnp.sum(p, axis=-1, keepdims=True)
        o = jnp.dot(p.astype(BF16), kv_ref[:, D_MODEL + lo:D_MODEL + hi], preferred_element_type=F32)
        outs.append(o / denom)
    o = jnp.concatenate(outs, axis=1).astype(BF16)
    x2 = x + jnp.dot(o, wo_ref[...], preferred_element_type=F32)
    hf = _rms(x2, gf)

    h_hi = hf.astype(BF16)
    h_lo = (hf - h_hi.astype(F32)).astype(BF16)
    hw = jnp.dot(h_hi, wr_ref[...], preferred_element_type=F32)
    lw = jnp.dot(h_lo, wr_ref[:, :LANES], preferred_element_type=F32)
    logits = (hw[:, :LANES] + (hw[:, LANES:] + lw)) + br
    return x2, hf.astype(BF16), logits


def _route_block(logits, tri):
    tm = logits.shape[0]
    lane = lax.broadcasted_iota(I32, (tm, LANES), 1).astype(F32)
    work = logits
    vals, idxs, sels = [], [], []
    for _ in range(TOP_K):
        m = jnp.max(work, axis=-1, keepdims=True)
        idx = jnp.min(jnp.where(work == m, lane, float(LANES)), axis=-1, keepdims=True)
        sel = lane == idx
        vals.append(m)
        idxs.append(idx)
        sels.append(sel)
        work = jnp.where(sel, -3e38, work)
    exps = [jnp.exp(v - vals[0]) for v in vals]
    den = exps[0] + exps[1] + exps[2] + exps[3]

    onehot = jnp.zeros((tm, LANES), F32)
    for sel in sels:
        onehot = onehot + sel.astype(F32)
    blocks = [onehot[r:r + TOK_BLOCK, :] for r in range(0, tm, TOK_BLOCK)]
    before = jnp.concatenate(
        [jnp.dot(tri, blk.astype(BF16), preferred_element_type=F32) for blk in blocks], axis=0)
    counts = [jnp.sum(blk, axis=0, keepdims=True) for blk in blocks]
    upper = (lax.broadcasted_iota(I32, (LANES, LANES), 0)
             < lax.broadcasted_iota(I32, (LANES, LANES), 1)).astype(BF16)
    starts = []
    for cnt in counts:
        seg = jnp.where(lane[:SUBLANES, :] < float(N_EXPERTS), _segment_rows(cnt), 0.0)
        seg = jnp.broadcast_to(seg, (SUBLANES, LANES)).astype(BF16)
        start = jnp.dot(seg, upper, preferred_element_type=F32)[:1, :]
        starts.append(jnp.broadcast_to(start, (TOK_BLOCK, LANES)))
    first_row = before + jnp.concatenate(starts, axis=0)
    meta = jnp.zeros((tm, LANES), F32)
    gates = jnp.zeros((tm, LANES), F32)
    for k in range(TOP_K):
        pos = jnp.sum(jnp.where(sels[k], first_row, 0.0), axis=-1, keepdims=True)
        meta = jnp.where(lane == float(k), idxs[k], meta)
        meta = jnp.where(lane == float(TOP_K + k), pos, meta)
        gates = jnp.where(lane == float(k), exps[k] / den, gates)
    return meta.astype(I32), gates, [c.astype(I32) for c in counts]


def _xattn_kernel(x_ref, kv_ref, gq_ref, wq_ref, wo_ref, gf_ref, wr_ref, br_ref, tri_ref,
                  x2_ref, h_ref, meta_ref, meta_t_ref, gate_ref, cnt_ref):
    tb = TOK_BLOCK * XATTN_CHAIN
    chains = []
    for chain in range(XATTN_BLOCKS // XATTN_CHAIN):
        rows = slice(chain * tb, (chain + 1) * tb)
        x2, hf, logits = _xattn_block(
            x_ref[rows, :], kv_ref, gq_ref[...], wq_ref, wo_ref, gf_ref[...], wr_ref, br_ref[...])
        x2_ref[rows, :] = x2
        h_ref[rows, :] = hf
        chains.append((rows, logits))
    for chain, (rows, logits) in enumerate(chains):
        meta, gates, counts = _route_block(logits, tri_ref[...])
        meta_ref[rows, :] = meta
        meta_t_ref[:, rows] = meta.T[:SUBLANES, :]
        gate_ref[rows, :] = gates
        for r, c in enumerate(counts):
            cnt_ref[chain * XATTN_CHAIN + r] = c


def _xattn_router(x1, kv, gq, wq, wo, gf, wr, br, tri, seq):
    n = x1.shape[0]
    tm = TOK_BLOCK * XATTN_BLOCKS
    per_seq = seq // tm
    const = lambda i: (0, 0)
    return pl.pallas_call(
        _xattn_kernel,
        grid=(n // tm,),
        in_specs=[
            pl.BlockSpec((tm, D_MODEL), lambda i: (i, 0)),
            pl.BlockSpec((None, N_MEM, 2 * D_MODEL), lambda i: (i // per_seq, 0, 0)),
            pl.BlockSpec((1, D_MODEL), const),
            pl.BlockSpec((D_MODEL, D_MODEL), const),
            pl.BlockSpec((D_MODEL, D_MODEL), const),
            pl.BlockSpec((1, D_MODEL), const),
            pl.BlockSpec((D_MODEL, 2 * LANES), const),
            pl.BlockSpec((1, LANES), const),
            pl.BlockSpec((TOK_BLOCK, TOK_BLOCK), const),
        ],
        out_specs=[
            pl.BlockSpec((tm, D_MODEL), lambda i: (i, 0)),
            pl.BlockSpec((tm, D_MODEL), lambda i: (i, 0)),
            pl.BlockSpec((tm, LANES), lambda i: (i, 0)),
            pl.BlockSpec((SUBLANES, tm), lambda i: (0, i)),
            pl.BlockSpec((tm, LANES), lambda i: (i, 0)),
            pl.BlockSpec((XATTN_BLOCKS, 1, LANES), lambda i: (i, 0, 0)),
        ],
        out_shape=[
            jax.ShapeDtypeStruct((n, D_MODEL), F32),
            jax.ShapeDtypeStruct((n, D_MODEL), BF16),
            jax.ShapeDtypeStruct((n, LANES), I32),
            jax.ShapeDtypeStruct((SUBLANES, n), I32),
            jax.ShapeDtypeStruct((n, LANES), F32),
            jax.ShapeDtypeStruct((n // TOK_BLOCK, 1, LANES), I32),
        ],
        compiler_params=_params("arbitrary"),
        name="xattn_router",
    )(x1, kv, gq, wq, wo, gf, wr, br, tri)


def _segment_copy(tables, step, e, vmem_ref, hbm_ref, sem, to_hbm):
    src_ref, dst_ref, cnt_ref = tables
    at = step * N_EXPERTS + e
    n = pl.multiple_of(cnt_ref[at], SUBLANES)
    local = vmem_ref.at[pl.ds(pl.multiple_of(src_ref[at], SUBLANES), n), :]
    remote = hbm_ref.at[pl.ds(pl.multiple_of(dst_ref[at], SUBLANES), n), :]
    return pltpu.make_async_copy(local, remote, sem) if to_hbm else pltpu.make_async_copy(remote, local, sem)


def _each_segment(fn):
    for r in range(SORT_BLOCKS):
        for e in range(N_EXPERTS):
            fn(r, e)


def _dispatch_kernel(src_ref, dst_ref, cnt_ref, fill_ref, h_ref, pos_ref, xs_ref, stage_ref,
                     zero_ref, sem, fill_sem):
    tables = (src_ref, dst_ref, cnt_ref)
    step = pl.program_id(0)
    last = pl.num_programs(0) - 1
    parity = step % 2
    where = lax.broadcasted_iota(I32, (SORT_ROWS, TOK_BLOCK), 0)
    for r in range(SORT_BLOCKS):
        rows = slice(r * TOK_BLOCK, (r + 1) * TOK_BLOCK)
        pos = pos_ref[:, rows]
        hit = where == pos[TOP_K:TOP_K + 1, :]
        for k in range(1, TOP_K):
            hit = jnp.logical_or(hit, where == pos[TOP_K + k:TOP_K + k + 1, :])
        onehot = jnp.where(hit, 1.0, 0.0).astype(BF16)
        stage_ref[parity * SORT_BLOCKS + r] = jnp.dot(onehot, h_ref[rows, :],
                                                      preferred_element_type=F32)

    def copy(at_step, at_parity, r, e):
        return _segment_copy(tables, at_step * SORT_BLOCKS + r, e,
                             stage_ref.at[at_parity * SORT_BLOCKS + r], xs_ref, sem.at[at_parity], True)

    _each_segment(lambda r, e: copy(step, parity, r, e).start())

    @pl.when(step > 0)
    def _():
        _each_segment(lambda r, e: copy(step - 1, 1 - parity, r, e).wait())

    @pl.when(step == last)
    def _():
        _each_segment(lambda r, e: copy(step, parity, r, e).wait())

    def pad_copy(e):
        n = pl.multiple_of(fill_ref[N_EXPERTS + e], SUBLANES)
        at = pl.multiple_of(fill_ref[e], SUBLANES)
        return pltpu.make_async_copy(zero_ref.at[pl.ds(0, n), :], xs_ref.at[pl.ds(at, n), :], fill_sem)

    def tail_copy(t):
        at = pl.multiple_of(fill_ref[2 * N_EXPERTS] + t * ROW_BLOCK, ROW_BLOCK)
        return pltpu.make_async_copy(zero_ref, xs_ref.at[pl.ds(at, ROW_BLOCK), :], fill_sem)

    def each_pad(fn):
        def body(e, carry):
            @pl.when(fill_ref[N_EXPERTS + e] > 0)
            def _():
                fn(e)
            return carry
        lax.fori_loop(0, N_EXPERTS, body, 0)

    def each_tail(fn):
        def body(t, carry):
            fn(t)
            return carry
        lax.fori_loop(0, fill_ref[2 * N_EXPERTS + 1], body, 0)

    @pl.when(step == 0)
    def _():
        zero_ref[...] = jnp.zeros_like(zero_ref)
        each_pad(lambda e: pad_copy(e).start())
        each_tail(lambda t: tail_copy(t).start())

    @pl.when(step == last)
    def _():
        each_pad(lambda e: pad_copy(e).wait())
        each_tail(lambda t: tail_copy(t).wait())


def _dispatch(tables, fill, hf, pos, n_rows):
    n = hf.shape[0]
    tm = TOK_BLOCK * SORT_BLOCKS
    grid_spec = pltpu.PrefetchScalarGridSpec(
        num_scalar_prefetch=4,
        grid=(n // tm,),
        in_specs=[
            pl.BlockSpec((tm, D_MODEL), lambda i, *_: (i, 0)),
            pl.BlockSpec((SUBLANES, tm), lambda i, *_: (0, i)),
        ],
        out_specs=pl.BlockSpec(memory_space=pl.ANY),
        scratch_shapes=[pltpu.VMEM((2 * SORT_BLOCKS, SORT_ROWS, D_MODEL), F32),
                        pltpu.VMEM((ROW_BLOCK, D_MODEL), F32),
                        pltpu.SemaphoreType.DMA((2,)), pltpu.SemaphoreType.DMA],
    )
    return pl.pallas_call(
        _dispatch_kernel,
        grid_spec=grid_spec,
        out_shape=jax.ShapeDtypeStruct((n_rows, D_MODEL), F32),
        compiler_params=_params("arbitrary"),
        name="dispatch",
    )(*tables, fill, hf, pos)


def _expert_kernel(blk_e_ref, next_e_ref, n_used_ref, xs_ref, w1_hbm, b1_ref, w2_hbm, b2_ref,
                   ys_ref, w1f_ref, w2f_ref, w1b_ref, w2b_ref, sem):
    i = pl.program_id(0)
    used = i < n_used_ref[0]

    def fetch(e):
        return (pltpu.make_async_copy(w1_hbm.at[e], w1f_ref, sem.at[0]),
                pltpu.make_async_copy(w2_hbm.at[e], w2f_ref, sem.at[1]))

    @pl.when(used)
    def _():
        e = blk_e_ref[i]

        @pl.when(i == 0)
        def _():
            for copy in fetch(e):
                copy.start()

        @pl.when(jnp.logical_or(i == 0, e != blk_e_ref[jnp.maximum(i - 1, 0)]))
        def _():
            for copy in fetch(e):
                copy.wait()
            w1b_ref[...] = w1f_ref[...].astype(BF16)
            w2b_ref[...] = w2f_ref[...].astype(BF16)

            @pl.when(next_e_ref[i] >= 0)
            def _():
                for copy in fetch(next_e_ref[i]):
                    copy.start()

        gu = jnp.dot(xs_ref[...].astype(BF16), w1b_ref[...], preferred_element_type=F32) + b1_ref[...]
        gate = jnp.minimum(gu[:, :D_FF], SWIGLU_LIMIT)
        up = jnp.clip(gu[:, D_FF:], -SWIGLU_LIMIT, SWIGLU_LIMIT)
        act = (up + 1.0) * gate * jax.nn.sigmoid(SWIGLU_ALPHA * gate)
        ys_ref[...] = jnp.dot(act.astype(BF16), w2b_ref[...], preferred_element_type=F32) + b2_ref[...]


def _experts(blk_e, next_e, n_used, xs, w1, b1, w2, b2):
    n_rows = xs.shape[0]
    bm = ROW_BLOCK
    last_used = lambda i, nu: jnp.maximum(jnp.minimum(i, nu[0] - 1), 0)
    row = lambda i, be, ne, nu: (last_used(i, nu), 0)
    exp3 = lambda i, be, ne, nu: (be[last_used(i, nu)], 0, 0)
    grid_spec = pltpu.PrefetchScalarGridSpec(
        num_scalar_prefetch=3,
        grid=(n_rows // bm,),
        in_specs=[
            pl.BlockSpec((bm, D_MODEL), row),
            pl.BlockSpec(memory_space=pl.ANY),
            pl.BlockSpec((None, 1, 2 * D_FF), exp3),
            pl.BlockSpec(memory_space=pl.ANY),
            pl.BlockSpec((None, 1, D_MODEL), exp3),
        ],
        out_specs=pl.BlockSpec((bm, D_MODEL), row),
        scratch_shapes=[pltpu.VMEM((D_MODEL, 2 * D_FF), F32), pltpu.VMEM((D_FF, D_MODEL), F32),
                        pltpu.VMEM((D_MODEL, 2 * D_FF), BF16), pltpu.VMEM((D_FF, D_MODEL), BF16),
                        pltpu.SemaphoreType.DMA((2,))],
    )
    return pl.pallas_call(
        _expert_kernel,
        grid_spec=grid_spec,
        out_shape=jax.ShapeDtypeStruct((n_rows, D_MODEL), F32),
        input_output_aliases={3: 0},
        compiler_params=_params("arbitrary"),
        name="experts",
    )(blk_e, next_e, n_used, xs, w1, b1, w2, b2)


def _combine_kernel(src_ref, dst_ref, cnt_ref, x_ref, gate_ref, pos_ref, g_ref, ys_ref, o_ref,
                    stage_ref, sem):
    tables = (src_ref, dst_ref, cnt_ref)
    step = pl.program_id(0)
    last = pl.num_programs(0) - 1
    parity = step % 2

    def copy(at_step, at_parity, r, e):
        return _segment_copy(tables, at_step * SORT_BLOCKS + r, e,
                             stage_ref.at[at_parity * SORT_BLOCKS + r], ys_ref, sem.at[at_parity], False)

    @pl.when(step == 0)
    def _():
        stage_ref[...] = jnp.zeros_like(stage_ref)
        _each_segment(lambda r, e: copy(step, parity, r, e).start())

    @pl.when(step < last)
    def _():
        _each_segment(lambda r, e: copy(step + 1, 1 - parity, r, e).start())

    _each_segment(lambda r, e: copy(step, parity, r, e).wait())

    where = lax.broadcasted_iota(I32, (TOK_BLOCK, SORT_ROWS), 1)
    for r in range(SORT_BLOCKS):
        rows = slice(r * TOK_BLOCK, (r + 1) * TOK_BLOCK)
        pos = pos_ref[rows, :]
        gates = gate_ref[rows, :]
        weights = jnp.zeros((TOK_BLOCK, SORT_ROWS), F32)
        for k in range(TOP_K):
            weights = jnp.where(where == pos[:, TOP_K + k:TOP_K + k + 1], gates[:, k:k + 1], weights)
        y = x_ref[rows, :] + jnp.dot(weights.astype(BF16),
                                     stage_ref[parity * SORT_BLOCKS + r].astype(BF16),
                                     preferred_element_type=F32)
        o_ref[rows, :] = _rms(y, g_ref[...])


def _combine(tables, x2, gates, pos, g_final, ys):
    n = x2.shape[0]
    tc = TOK_BLOCK * SORT_BLOCKS
    grid_spec = pltpu.PrefetchScalarGridSpec(
        num_scalar_prefetch=3,
        grid=(n // tc,),
        in_specs=[
            pl.BlockSpec((tc, D_MODEL), lambda i, *_: (i, 0)),
            pl.BlockSpec((tc, LANES), lambda i, *_: (i, 0)),
            pl.BlockSpec((tc, LANES), lambda i, *_: (i, 0)),
            pl.BlockSpec((1, D_MODEL), lambda i, *_: (0, 0)),
            pl.BlockSpec(memory_space=pl.ANY),
        ],
        out_specs=pl.BlockSpec((tc, D_MODEL), lambda i, *_: (i, 0)),
        scratch_shapes=[pltpu.VMEM((2 * SORT_BLOCKS, SORT_ROWS, D_MODEL), F32),
                        pltpu.SemaphoreType.DMA((2,))],
    )
    return pl.pallas_call(
        _combine_kernel,
        grid_spec=grid_spec,
        out_shape=jax.ShapeDtypeStruct((n, D_MODEL), F32),
        compiler_params=_params("arbitrary"),
        name="combine",
    )(*tables, x2, gates, pos, g_final, ys)


def _strict_lower(n):
    r = lax.broadcasted_iota(I32, (n, n), 0)
    c = lax.broadcasted_iota(I32, (n, n), 1)
    return (r > c).astype(BF16)


def _round_up(v, m):
    return (v + m - 1) // m * m


def _routing_tables(cnt, n):
    bm = ROW_BLOCK
    n_tok_blk = n // TOK_BLOCK
    n_rows = _round_up(n * TOP_K + n_tok_blk * N_EXPERTS * SUBLANES, bm) + N_EXPERTS * bm
    n_blk = n_rows // bm
    seg = _segment_rows(cnt[:, 0, :N_EXPERTS]).astype(I32)
    local_start = jnp.cumsum(seg, axis=1) - seg
    total = jnp.sum(seg, axis=0)
    padded = _round_up(total, bm)
    pend = jnp.cumsum(padded)
    pstart = pend - padded
    global_start = pstart[None, :] + jnp.cumsum(seg, axis=0) - seg
    tables = (local_start.reshape(-1), global_start.reshape(-1), seg.reshape(-1))

    n_used = (pend[-1:] // bm).astype(I32)
    blk_row = jnp.arange(n_blk, dtype=I32) * bm
    blk_e = jnp.minimum(jnp.sum(blk_row[:, None] >= pend[None, :], axis=1), N_EXPERTS - 1).astype(I32)
    ids = jnp.arange(N_EXPERTS, dtype=I32)
    later = jnp.where((ids[None, :] > ids[:, None]) & (padded[None, :] > 0), ids[None, :], N_EXPERTS)
    follow = jnp.min(later, axis=1)
    follow = jnp.where(follow < N_EXPERTS, follow, -1)
    next_e = jnp.sum(jnp.where(blk_e[:, None] == ids[None, :], follow[None, :], 0), axis=1).astype(I32)
    fill = jnp.concatenate([pstart + total, padded - total, pend[-1:], n_blk - pend[-1:] // bm])
    return tables, fill.astype(I32), blk_e, next_e, n_used, n_rows


def _layer(x2d, mem, b, s, g_mix, w_in, g_sb_out, g_pool_out, w_pool, pool_scale, w_out,
           g_mem_q, g_mem_kv, w_mem_q, w_mem_kv, w_mem_o, g_ffn, w_router, b_router,
           w_e_in, b_e_in, w_e_out, b_e_out, g_out):
    n = b * s
    row = lambda v: v.reshape(1, -1).astype(F32)

    proj = _in_proj(x2d, row(g_mix), w_in.astype(BF16))
    sb = _sb_attention(proj.reshape(b, s, IN_PROJ_COLS), -_strict_lower(SB_BLOCK))
    groups = len(POOL_WINDOWS)
    wp = (w_pool.astype(BF16)[:, :, None, :] * jnp.eye(groups, dtype=BF16)[:, None, :, None])
    wp = wp.reshape(POOL_WIDTH, POOL_WIDTH)
    x1 = _mix_out(x2d, sb.reshape(n, SB_WIDTH), proj, wp, row(pool_scale),
                  row(g_sb_out), row(g_pool_out), w_out.astype(BF16), s)

    kv = _mem_kv(mem, row(g_mem_kv), w_mem_kv.astype(BF16))
    spare = LANES - N_EXPERTS
    wr = jnp.pad(w_router.astype(F32), ((0, 0), (0, spare)))
    wr_hi = wr.astype(BF16)
    wr_lo = (wr - wr_hi.astype(F32)).astype(BF16)
    br = jnp.pad(b_router.astype(F32).reshape(1, -1), ((0, 0), (0, spare)), constant_values=NEG_BIG)
    x2, hf, meta, meta_t, gates, cnt = _xattn_router(
        x1, kv, row(g_mem_q), w_mem_q.astype(BF16), w_mem_o.astype(BF16), row(g_ffn),
        jnp.concatenate([wr_hi, wr_lo], axis=1), br, _strict_lower(TOK_BLOCK), s)

    tables, fill, blk_e, next_e, n_used, n_rows = _routing_tables(cnt, n)
    xs = _dispatch(tables, fill, hf, meta_t, n_rows)
    ys = _experts(blk_e, next_e, n_used, xs, w_e_in, b_e_in.reshape(N_EXPERTS, 1, -1),
                  w_e_out, b_e_out.reshape(N_EXPERTS, 1, -1))
    return _combine(tables, x2, gates, meta, row(g_out), ys)


def kernel(x, mem, g_mix, w_in, g_sb_out, g_pool_out, w_pool, pool_scale, w_out, g_mem_q, g_mem_kv,
           w_mem_q, w_mem_kv, w_mem_o, g_ffn, w_router, b_router, w_expert_in, b_expert_in,
           w_expert_out, b_expert_out, g_final):
    b, s, d = x.shape
    depth = g_mix.shape[0]
    assert d == D_MODEL and depth == 1, "the final RMSNorm is fused into the single layer's combine"
    assert s % PROJ_BLOCK == 0 and s % (SB_BLOCK * SB_QBLOCKS) == 0 and s % (TOK_BLOCK * XATTN_BLOCKS) == 0
    assert (b * s) % (TOK_BLOCK * SORT_BLOCKS) == 0
    out = _layer(x.reshape(b * s, d), mem, b, s, g_mix[0], w_in[0], g_sb_out[0], g_pool_out[0],
                 w_pool[0], pool_scale[0], w_out[0], g_mem_q[0], g_mem_kv[0], w_mem_q[0],
                 w_mem_kv[0], w_mem_o[0], g_ffn[0], w_router[0], b_router[0], w_expert_in[0],
                 b_expert_in[0], w_expert_out[0], b_expert_out[0], g_final)
    return out.reshape(b, s, d)
```

```python
import functools
import math

import jax
import jax.numpy as jnp
from jax import lax
from jax.experimental import pallas as pl
from jax.experimental.pallas import tpu as pltpu

F32 = jnp.float32
BF16 = jnp.bfloat16
I32 = jnp.int32

D_MODEL = 1024
SB_HEADS = 8
SB_HEAD_DIM = 64
SB_WIDTH = SB_HEADS * SB_HEAD_DIM
POOL_WINDOWS = (2, 4, 8, 16)
POOL_GROUP_DIM = 128
POOL_WIDTH = len(POOL_WINDOWS) * POOL_GROUP_DIM
IN_PROJ_COLS = 3 * SB_WIDTH + POOL_WIDTH
N_MEM = 256
MEM_HEADS = 4
MEM_HEAD_DIM = D_MODEL // MEM_HEADS
N_EXPERTS = 32
TOP_K = 4
D_FF = D_MODEL
SWIGLU_LIMIT = 7.0
SWIGLU_ALPHA = 1.702
RMS_EPS = 1e-5

LANES = 128
SUBLANES = 8
POOL_HALO = 16
SB_BLOCK = 256
SB_QBLOCKS = 4
SB_DEAD_LOG = -110.0
ROW_BLOCK = 512
TOK_BLOCK = 512
XATTN_BLOCKS = 2
XATTN_CHAIN = 1
SORT_BLOCKS = 1
PROJ_BLOCK = 1024
SORT_ROWS = -(-(TOK_BLOCK * TOP_K + N_EXPERTS * SUBLANES) // LANES) * LANES
VMEM_LIMIT = 48 * 1024 * 1024
NEG_BIG = -1e30


def _rms(x, g):
    ms = jnp.mean(x * x, axis=-1, keepdims=True)
    return x * lax.rsqrt(ms + RMS_EPS) * g


def _segment_rows(count):
    return jnp.ceil(jnp.maximum(count, 1) / SUBLANES) * SUBLANES


def _params(*sem):
    return pltpu.CompilerParams(dimension_semantics=sem, vmem_limit_bytes=VMEM_LIMIT)


def _in_proj_kernel(x_ref, g_ref, w_ref, o_ref):
    h = _rms(x_ref[...], g_ref[...]).astype(BF16)
    acc = jnp.dot(h, w_ref[...], preferred_element_type=F32)
    scale = 1.0 / math.sqrt(SB_HEAD_DIM)
    o_ref[:, :SB_WIDTH] = (acc[:, :SB_WIDTH] * scale).astype(BF16)
    o_ref[:, SB_WIDTH:] = acc[:, SB_WIDTH:].astype(BF16)


def _in_proj(x2d, g, w):
    n = x2d.shape[0]
    tm = PROJ_BLOCK
    return pl.pallas_call(
        _in_proj_kernel,
        grid=(n // tm,),
        in_specs=[
            pl.BlockSpec((tm, D_MODEL), lambda i: (i, 0)),
            pl.BlockSpec((1, D_MODEL), lambda i: (0, 0)),
            pl.BlockSpec((D_MODEL, IN_PROJ_COLS), lambda i: (0, 0)),
        ],
        out_specs=pl.BlockSpec((tm, IN_PROJ_COLS), lambda i: (i, 0)),
        out_shape=jax.ShapeDtypeStruct((n, IN_PROJ_COLS), BF16),
        compiler_params=_params("arbitrary"),
        name="in_proj",
    )(x2d, g, w)


def _sb_chain(qh, kb, vb, neg_tri, c, causal):
    z = lax.dot_general(qh, kb, (((1,), (1,)), ((), ())), preferred_element_type=F32)
    zb = z.astype(BF16)
    spb = jnp.maximum(zb, 0.0) + jnp.log(1.0 + jnp.exp(-jnp.abs(zb)))
    if causal is not None:
        spb = jnp.where(causal, spb, 0.0)
    after = jnp.dot(spb, neg_tri, preferred_element_type=F32)
    sp = spb.astype(F32)
    w = jnp.exp((z - sp) + after)
    if causal is not None:
        w = jnp.where(causal, w, 0.0)
    pv = jnp.dot(w.astype(BF16), vb, preferred_element_type=F32) * jnp.exp(c)
    return pv, c + (after[:, :1] - sp[:, :1])


def _sb_kernel(q_ref, k_ref, v_ref, tri_ref, o_ref, acc_ref, c_ref):
    tq = SB_BLOCK
    lane = lax.broadcasted_iota(I32, (tq, LANES), 1)
    neg_tri = tri_ref[...]
    row = lax.broadcasted_iota(I32, (2 * tq, tq), 0)
    col = lax.broadcasted_iota(I32, (2 * tq, tq), 1)
    causal = col < jnp.where(row >= tq, row - tq, row)

    def stacked(u):
        q = q_ref[u * tq:(u + 1) * tq, :]
        zero = jnp.zeros_like(q)
        return jnp.concatenate(
            [jnp.where(lane < SB_HEAD_DIM, q, zero), jnp.where(lane >= SB_HEAD_DIM, q, zero)], axis=0)

    def step(qs, j, c, diagonal):
        off = pl.multiple_of(j * tq, tq)
        return _sb_chain(qs, k_ref[pl.ds(off, tq), :], v_ref[pl.ds(off, tq), :], neg_tri, c,
                         causal if diagonal else None)

    blocks = []
    for u in range(SB_QBLOCKS):
        i = pl.program_id(2) * SB_QBLOCKS + u
        qs = stacked(u)
        pv, c = step(qs, i, jnp.zeros((2 * tq, LANES), F32), True)
        pv_left, c_left = step(qs, jnp.maximum(i - 1, 0), c, False)
        has_left = i > 0
        acc_ref[u] = pv + jnp.where(has_left, pv_left, 0.0)
        c_ref[u] = jnp.where(has_left, c_left, c)
        blocks.append((i, qs))

    @pl.when(jnp.max(c_ref[...]) > SB_DEAD_LOG)
    def _():
        for u, (i, qs) in enumerate(blocks):
            def alive():
                return jnp.max(c_ref[u]) > SB_DEAD_LOG

            def cond(carry):
                j, live = carry
                return jnp.logical_and(j >= 0, live)

            def body(carry):
                j, _ = carry
                pv, c = step(qs, j, c_ref[u], False)
                acc_ref[u] += pv
                c_ref[u] = c
                return j - 1, alive()

            lax.while_loop(cond, body, (i - 2, alive()))

    for u in range(SB_QBLOCKS):
        o_ref[u * tq:(u + 1) * tq, :] = jnp.where(
            lane < SB_HEAD_DIM, acc_ref[u, :tq, :], acc_ref[u, tq:, :]).astype(o_ref.dtype)


def _sb_attention(proj3, neg_tri):
    b, s, _ = proj3.shape
    tq = SB_BLOCK * SB_QBLOCKS
    pairs = SB_WIDTH // LANES
    return pl.pallas_call(
        _sb_kernel,
        grid=(b, pairs, s // tq),
        in_specs=[
            pl.BlockSpec((None, tq, LANES), lambda bi, p, i: (bi, i, p)),
            pl.BlockSpec((None, s, LANES), lambda bi, p, i: (bi, 0, pairs + p)),
            pl.BlockSpec((None, s, LANES), lambda bi, p, i: (bi, 0, 2 * pairs + p)),
            pl.BlockSpec((SB_BLOCK, SB_BLOCK), lambda bi, p, i: (0, 0)),
        ],
        out_specs=pl.BlockSpec((None, tq, LANES), lambda bi, p, i: (bi, i, p)),
        out_shape=jax.ShapeDtypeStruct((b, s, SB_WIDTH), BF16),
        scratch_shapes=[pltpu.VMEM((SB_QBLOCKS, 2 * SB_BLOCK, LANES), F32),
                        pltpu.VMEM((SB_QBLOCKS, 2 * SB_BLOCK, LANES), F32)],
        compiler_params=_params("arbitrary", "arbitrary", "arbitrary"),
        name="sb_attn",
    )(proj3, proj3, proj3, neg_tri)


def _mix_kernel(x_ref, sb_ref, u_ref, halo_ref, wp_ref, ps_ref, gsb_ref, gpool_ref, wout_ref,
                o_ref, *, seq):
    tm = x_ref.shape[0]
    t0 = (pl.program_id(0) * tm) % seq
    u = u_ref[...].astype(F32)
    halo = jnp.where(t0 == 0, 0.0, halo_ref[...].astype(F32))
    a = jnp.concatenate([halo, u], axis=0)
    pos = t0 + lax.broadcasted_iota(I32, (tm, POOL_GROUP_DIM), 0)
    outs = []
    for g, win in enumerate(POOL_WINDOWS):
        lo, hi = g * POOL_GROUP_DIM, (g + 1) * POOL_GROUP_DIM
        s = a[:, lo:hi]
        sh = 1
        while sh < win:
            s = s + pltpu.roll(s, sh, axis=0)
            sh *= 2
        cnt = jnp.minimum(pos + 1, win).astype(F32)
        outs.append(s[POOL_HALO:, :] / cnt - u[:, lo:hi])
    pooled = jnp.concatenate(outs, axis=1).astype(BF16)
    pooled = jnp.dot(pooled, wp_ref[...], preferred_element_type=F32) * ps_ref[...]
    pool_n = _rms(pooled, gpool_ref[...]).astype(BF16)
    sb_n = _rms(sb_ref[...].astype(F32), gsb_ref[...]).astype(BF16)
    y = jnp.dot(sb_n, wout_ref[:SB_WIDTH, :], preferred_element_type=F32)
    y = y + jnp.dot(pool_n, wout_ref[SB_WIDTH:, :], preferred_element_type=F32)
    o_ref[...] = x_ref[...] + y


def _mix_out(x2d, sb2d, proj2d, wp, ps, gsb, gpool, wout, seq):
    n = x2d.shape[0]
    tm = PROJ_BLOCK
    ucol = 3 * SB_WIDTH // POOL_WIDTH
    hb = tm // POOL_HALO
    return pl.pallas_call(
        functools.partial(_mix_kernel, seq=seq),
        grid=(n // tm,),
        in_specs=[
            pl.BlockSpec((tm, D_MODEL), lambda i: (i, 0)),
            pl.BlockSpec((tm, SB_WIDTH), lambda i: (i, 0)),
            pl.BlockSpec((tm, POOL_WIDTH), lambda i: (i, ucol)),
            pl.BlockSpec((POOL_HALO, POOL_WIDTH), lambda i: (jnp.maximum(i * hb - 1, 0), ucol)),
            pl.BlockSpec((POOL_WIDTH, POOL_WIDTH), lambda i: (0, 0)),
            pl.BlockSpec((1, POOL_WIDTH), lambda i: (0, 0)),
            pl.BlockSpec((1, SB_WIDTH), lambda i: (0, 0)),
            pl.BlockSpec((1, POOL_WIDTH), lambda i: (0, 0)),
            pl.BlockSpec((D_MODEL, D_MODEL), lambda i: (0, 0)),
        ],
        out_specs=pl.BlockSpec((tm, D_MODEL), lambda i: (i, 0)),
        out_shape=jax.ShapeDtypeStruct((n, D_MODEL), F32),
        compiler_params=_params("arbitrary"),
        name="mix_out",
    )(x2d, sb2d, proj2d, proj2d, wp, ps, gsb, gpool, wout)


def _mem_kv_kernel(m_ref, g_ref, w_ref, o_ref):
    h = _rms(m_ref[...], g_ref[...]).astype(BF16)
    o_ref[...] = jnp.dot(h, w_ref[...], preferred_element_type=F32).astype(BF16)


def _mem_kv(mem, g, w):
    b = mem.shape[0]
    return pl.pallas_call(
        _mem_kv_kernel,
        grid=(b,),
        in_specs=[
            pl.BlockSpec((None, N_MEM, D_MODEL), lambda i: (i, 0, 0)),
            pl.BlockSpec((1, D_MODEL), lambda i: (0, 0)),
            pl.BlockSpec((D_MODEL, 2 * D_MODEL), lambda i: (0, 0)),
        ],
        out_specs=pl.BlockSpec((None, N_MEM, 2 * D_MODEL), lambda i: (i, 0, 0)),
        out_shape=jax.ShapeDtypeStruct((b, N_MEM, 2 * D_MODEL), BF16),
        compiler_params=_params("arbitrary"),
        name="mem_kv",
    )(mem, g, w)


def _xattn_block(x, kv_ref, gq, wq_ref, wo_ref, gf, wr_ref, br):
    tm = x.shape[0]
    hq = _rms(x, gq).astype(BF16)
    q = jnp.dot(hq, wq_ref[...], preferred_element_type=F32) * (1.0 / math.sqrt(MEM_HEAD_DIM))
    q = q.astype(BF16)
    outs = []
    for h in range(MEM_HEADS):
        lo, hi = h * MEM_HEAD_DIM, (h + 1) * MEM_HEAD_DIM
        s = lax.dot_general(q[:, lo:hi], kv_ref[:, lo:hi], (((1,), (1,)), ((), ())),
                            preferred_element_type=F32)
        p = jnp.exp(s - jnp.max(s, axis=-1, keepdims=True))
        denom = jnp.sum(p, axis=-1, keepdims=True)
        o = jnp.dot(p.astype(BF16), kv_ref[:, D_MODEL + lo:D_MODEL + hi], preferred_element_type=F32)
        outs.append(o / denom)
    o = jnp.concatenate(outs, axis=1).astype(BF16)
    x2 = x + jnp.dot(o, wo_ref[...], preferred_element_type=F32)
    hf = _rms(x2, gf)

    h_hi = hf.astype(BF16)
    h_lo = (hf - h_hi.astype(F32)).astype(BF16)
    hw = jnp.dot(h_hi, wr_ref[...], preferred_element_type=F32)
    lw = jnp.dot(h_lo, wr_ref[:, :LANES], preferred_element_type=F32)
    logits = (hw[:, :LANES] + (hw[:, LANES:] + lw)) + br
    return x2, hf.astype(BF16), logits


def _route_block(logits, tri):
    tm = logits.shape[0]
    lane = lax.broadcasted_iota(I32, (tm, LANES), 1).astype(F32)
    work = logits
    vals, idxs, sels = [], [], []
    for _ in range(TOP_K):
        m = jnp.max(work, axis=-1, keepdims=True)
        idx = jnp.min(jnp.where(work == m, lane, float(LANES)), axis=-1, keepdims=True)
        sel = lane == idx
        vals.append(m)
        idxs.append(idx)
        sels.append(sel)
        work = jnp.where(sel, -3e38, work)
    exps = [jnp.exp(v - vals[0]) for v in vals]
    den = exps[0] + exps[1] + exps[2] + exps[3]

    onehot = jnp.zeros((tm, LANES), F32)
    for sel in sels:
        onehot = onehot + sel.astype(F32)
    blocks = [onehot[r:r + TOK_BLOCK, :] for r in range(0, tm, TOK_BLOCK)]
    before = jnp.concatenate(
        [jnp.dot(tri, blk.astype(BF16), preferred_element_type=F32) for blk in blocks], axis=0)
    counts = [jnp.sum(blk, axis=0, keepdims=True) for blk in blocks]
    upper = (lax.broadcasted_iota(I32, (LANES, LANES), 0)
             < lax.broadcasted_iota(I32, (LANES, LANES), 1)).astype(BF16)
    starts = []
    for cnt in counts:
        seg = jnp.where(lane[:SUBLANES, :] < float(N_EXPERTS), _segment_rows(cnt), 0.0)
        seg = jnp.broadcast_to(seg, (SUBLANES, LANES)).astype(BF16)
        start = jnp.dot(seg, upper, preferred_element_type=F32)[:1, :]
        starts.append(jnp.broadcast_to(start, (TOK_BLOCK, LANES)))
    first_row = before + jnp.concatenate(starts, axis=0)
    meta = jnp.zeros((tm, LANES), F32)
    gates = jnp.zeros((tm, LANES), F32)
    for k in range(TOP_K):
        pos = jnp.sum(jnp.where(sels[k], first_row, 0.0), axis=-1, keepdims=True)
        meta = jnp.where(lane == float(k), idxs[k], meta)
        meta = jnp.where(lane == float(TOP_K + k), pos, meta)
        gates = jnp.where(lane == float(k), exps[k] / den, gates)
    return meta.astype(I32), gates, [c.astype(I32) for c in counts]


def _xattn_kernel(x_ref, kv_ref, gq_ref, wq_ref, wo_ref, gf_ref, wr_ref, br_ref, tri_ref,
                  x2_ref, h_ref, meta_ref, meta_t_ref, gate_ref, cnt_ref):
    tb = TOK_BLOCK * XATTN_CHAIN
    chains = []
    for chain in range(XATTN_BLOCKS // XATTN_CHAIN):
        rows = slice(chain * tb, (chain + 1) * tb)
        x2, hf, logits = _xattn_block(
            x_ref[rows, :], kv_ref, gq_ref[...], wq_ref, wo_ref, gf_ref[...], wr_ref, br_ref[...])
        x2_ref[rows, :] = x2
        h_ref[rows, :] = hf
        chains.append((rows, logits))
    for chain, (rows, logits) in enumerate(chains):
        meta, gates, counts = _route_block(logits, tri_ref[...])
        meta_ref[rows, :] = meta
        meta_t_ref[:, rows] = meta.T[:SUBLANES, :]
        gate_ref[rows, :] = gates
        for r, c in enumerate(counts):
            cnt_ref[chain * XATTN_CHAIN + r] = c


def _xattn_router(x1, kv, gq, wq, wo, gf, wr, br, tri, seq):
    n = x1.shape[0]
    tm = TOK_BLOCK * XATTN_BLOCKS
    per_seq = seq // tm
    const = lambda i: (0, 0)
    return pl.pallas_call(
        _xattn_kernel,
        grid=(n // tm,),
        in_specs=[
            pl.BlockSpec((tm, D_MODEL), lambda i: (i, 0)),
            pl.BlockSpec((None, N_MEM, 2 * D_MODEL), lambda i: (i // per_seq, 0, 0)),
            pl.BlockSpec((1, D_MODEL), const),
            pl.BlockSpec((D_MODEL, D_MODEL), const),
            pl.BlockSpec((D_MODEL, D_MODEL), const),
            pl.BlockSpec((1, D_MODEL), const),
            pl.BlockSpec((D_MODEL, 2 * LANES), const),
            pl.BlockSpec((1, LANES), const),
            pl.BlockSpec((TOK_BLOCK, TOK_BLOCK), const),
        ],
        out_specs=[
            pl.BlockSpec((tm, D_MODEL), lambda i: (i, 0)),
            pl.BlockSpec((tm, D_MODEL), lambda i: (i, 0)),
            pl.BlockSpec((tm, LANES), lambda i: (i, 0)),
            pl.BlockSpec((SUBLANES, tm), lambda i: (0, i)),
            pl.BlockSpec((tm, LANES), lambda i: (i, 0)),
            pl.BlockSpec((XATTN_BLOCKS, 1, LANES), lambda i: (i, 0, 0)),
        ],
        out_shape=[
            jax.ShapeDtypeStruct((n, D_MODEL), F32),
            jax.ShapeDtypeStruct((n, D_MODEL), BF16),
            jax.ShapeDtypeStruct((n, LANES), I32),
            jax.ShapeDtypeStruct((SUBLANES, n), I32),
            jax.ShapeDtypeStruct((n, LANES), F32),
            jax.ShapeDtypeStruct((n // TOK_BLOCK, 1, LANES), I32),
        ],
        compiler_params=_params("arbitrary"),
        name="xattn_router",
    )(x1, kv, gq, wq, wo, gf, wr, br, tri)


def _segment_copy(tables, step, e, vmem_ref, hbm_ref, sem, to_hbm):
    src_ref, dst_ref, cnt_ref = tables
    at = step * N_EXPERTS + e
    n = pl.multiple_of(cnt_ref[at], SUBLANES)
    local = vmem_ref.at[pl.ds(pl.multiple_of(src_ref[at], SUBLANES), n), :]
    remote = hbm_ref.at[pl.ds(pl.multiple_of(dst_ref[at], SUBLANES), n), :]
    return pltpu.make_async_copy(local, remote, sem) if to_hbm else pltpu.make_async_copy(remote, local, sem)


def _each_segment(fn):
    for r in range(SORT_BLOCKS):
        for e in range(N_EXPERTS):
            fn(r, e)


def _dispatch_kernel(src_ref, dst_ref, cnt_ref, fill_ref, h_ref, pos_ref, xs_ref, stage_ref,
                     zero_ref, sem, fill_sem):
    tables = (src_ref, dst_ref, cnt_ref)
    step = pl.program_id(0)
    last = pl.num_programs(0) - 1
    parity = step % 2
    where = lax.broadcasted_iota(I32, (SORT_ROWS, TOK_BLOCK), 0)
    for r in range(SORT_BLOCKS):
        rows = slice(r * TOK_BLOCK, (r + 1) * TOK_BLOCK)
        pos = pos_ref[:, rows]
        hit = where == pos[TOP_K:TOP_K + 1, :]
        for k in range(1, TOP_K):
            hit = jnp.logical_or(hit, where == pos[TOP_K + k:TOP_K + k + 1, :])
        onehot = jnp.where(hit, 1.0, 0.0).astype(BF16)
        stage_ref[parity * SORT_BLOCKS + r] = jnp.dot(onehot, h_ref[rows, :],
                                                      preferred_element_type=F32)

    def copy(at_step, at_parity, r, e):
        return _segment_copy(tables, at_step * SORT_BLOCKS + r, e,
                             stage_ref.at[at_parity * SORT_BLOCKS + r], xs_ref, sem.at[at_parity], True)

    _each_segment(lambda r, e: copy(step, parity, r, e).start(priority=e % 2))

    @pl.when(step > 0)
    def _():
        _each_segment(lambda r, e: copy(step - 1, 1 - parity, r, e).wait())

    @pl.when(step == last)
    def _():
        _each_segment(lambda r, e: copy(step, parity, r, e).wait())

    def pad_copy(e):
        n = pl.multiple_of(fill_ref[N_EXPERTS + e], SUBLANES)
        at = pl.multiple_of(fill_ref[e], SUBLANES)
        return pltpu.make_async_copy(zero_ref.at[pl.ds(0, n), :], xs_ref.at[pl.ds(at, n), :], fill_sem)

    def tail_copy(t):
        at = pl.multiple_of(fill_ref[2 * N_EXPERTS] + t * ROW_BLOCK, ROW_BLOCK)
        return pltpu.make_async_copy(zero_ref, xs_ref.at[pl.ds(at, ROW_BLOCK), :], fill_sem)

    def each_pad(fn):
        def body(e, carry):
            @pl.when(fill_ref[N_EXPERTS + e] > 0)
            def _():
                fn(e)
            return carry
        lax.fori_loop(0, N_EXPERTS, body, 0)

    def each_tail(fn):
        def body(t, carry):
            fn(t)
            return carry
        lax.fori_loop(0, fill_ref[2 * N_EXPERTS + 1], body, 0)

    @pl.when(step == 0)
    def _():
        zero_ref[...] = jnp.zeros_like(zero_ref)
        each_pad(lambda e: pad_copy(e).start())
        each_tail(lambda t: tail_copy(t).start())

    @pl.when(step == last)
    def _():
        each_pad(lambda e: pad_copy(e).wait())
        each_tail(lambda t: tail_copy(t).wait())


def _dispatch(tables, fill, hf, pos, n_rows):
    n = hf.shape[0]
    tm = TOK_BLOCK * SORT_BLOCKS
    grid_spec = pltpu.PrefetchScalarGridSpec(
        num_scalar_prefetch=4,
        grid=(n // tm,),
        in_specs=[
            pl.BlockSpec((tm, D_MODEL), lambda i, *_: (i, 0)),
            pl.BlockSpec((SUBLANES, tm), lambda i, *_: (0, i)),
        ],
        out_specs=pl.BlockSpec(memory_space=pl.ANY),
        scratch_shapes=[pltpu.VMEM((2 * SORT_BLOCKS, SORT_ROWS, D_MODEL), F32),
                        pltpu.VMEM((ROW_BLOCK, D_MODEL), F32),
                        pltpu.SemaphoreType.DMA((2,)), pltpu.SemaphoreType.DMA],
    )
    return pl.pallas_call(
        _dispatch_kernel,
        grid_spec=grid_spec,
        out_shape=jax.ShapeDtypeStruct((n_rows, D_MODEL), F32),
        compiler_params=_params("arbitrary"),
        name="dispatch",
    )(*tables, fill, hf, pos)


def _expert_kernel(blk_e_ref, next_e_ref, n_used_ref, xs_ref, w1_hbm, b1_ref, w2_hbm, b2_ref,
                   ys_ref, w1f_ref, w2f_ref, w1b_ref, w2b_ref, sem):
    i = pl.program_id(0)
    used = i < n_used_ref[0]

    def fetch(e):
        return (pltpu.make_async_copy(w1_hbm.at[e], w1f_ref, sem.at[0]),
                pltpu.make_async_copy(w2_hbm.at[e], w2f_ref, sem.at[1]))

    @pl.when(used)
    def _():
        e = blk_e_ref[i]

        @pl.when(i == 0)
        def _():
            for copy in fetch(e):
                copy.start()

        @pl.when(jnp.logical_or(i == 0, e != blk_e_ref[jnp.maximum(i - 1, 0)]))
        def _():
            for copy in fetch(e):
                copy.wait()
            w1b_ref[...] = w1f_ref[...].astype(BF16)
            w2b_ref[...] = w2f_ref[...].astype(BF16)

            @pl.when(next_e_ref[i] >= 0)
            def _():
                for copy in fetch(next_e_ref[i]):
                    copy.start()

        gu = jnp.dot(xs_ref[...].astype(BF16), w1b_ref[...], preferred_element_type=F32) + b1_ref[...]
        gate = jnp.minimum(gu[:, :D_FF], SWIGLU_LIMIT)
        up = jnp.clip(gu[:, D_FF:], -SWIGLU_LIMIT, SWIGLU_LIMIT)
        act = (up + 1.0) * gate * jax.nn.sigmoid(SWIGLU_ALPHA * gate)
        ys_ref[...] = jnp.dot(act.astype(BF16), w2b_ref[...], preferred_element_type=F32) + b2_ref[...]


def _experts(blk_e, next_e, n_used, xs, w1, b1, w2, b2):
    n_rows = xs.shape[0]
    bm = ROW_BLOCK
    last_used = lambda i, nu: jnp.maximum(jnp.minimum(i, nu[0] - 1), 0)
    row = lambda i, be, ne, nu: (last_used(i, nu), 0)
    exp3 = lambda i, be, ne, nu: (be[last_used(i, nu)], 0, 0)
    grid_spec = pltpu.PrefetchScalarGridSpec(
        num_scalar_prefetch=3,
        grid=(n_rows // bm,),
        in_specs=[
            pl.BlockSpec((bm, D_MODEL), row),
            pl.BlockSpec(memory_space=pl.ANY),
            pl.BlockSpec((None, 1, 2 * D_FF), exp3),
            pl.BlockSpec(memory_space=pl.ANY),
            pl.BlockSpec((None, 1, D_MODEL), exp3),
        ],
        out_specs=pl.BlockSpec((bm, D_MODEL), row),
        scratch_shapes=[pltpu.VMEM((D_MODEL, 2 * D_FF), F32), pltpu.VMEM((D_FF, D_MODEL), F32),
                        pltpu.VMEM((D_MODEL, 2 * D_FF), BF16), pltpu.VMEM((D_FF, D_MODEL), BF16),
                        pltpu.SemaphoreType.DMA((2,))],
    )
    return pl.pallas_call(
        _expert_kernel,
        grid_spec=grid_spec,
        out_shape=jax.ShapeDtypeStruct((n_rows, D_MODEL), F32),
        input_output_aliases={3: 0},
        compiler_params=_params("arbitrary"),
        name="experts",
    )(blk_e, next_e, n_used, xs, w1, b1, w2, b2)


def _combine_kernel(src_ref, dst_ref, cnt_ref, x_ref, gate_ref, pos_ref, g_ref, ys_ref, o_ref,
                    stage_ref, sem):
    tables = (src_ref, dst_ref, cnt_ref)
    step = pl.program_id(0)
    last = pl.num_programs(0) - 1
    parity = step % 2

    def copy(at_step, at_parity, r, e):
        return _segment_copy(tables, at_step * SORT_BLOCKS + r, e,
                             stage_ref.at[at_parity * SORT_BLOCKS + r], ys_ref, sem.at[at_parity], False)

    @pl.when(step == 0)
    def _():
        stage_ref[...] = jnp.zeros_like(stage_ref)
        _each_segment(lambda r, e: copy(step, parity, r, e).start(priority=e % 2))

    @pl.when(step < last)
    def _():
        _each_segment(lambda r, e: copy(step + 1, 1 - parity, r, e).start(priority=e % 2))

    _each_segment(lambda r, e: copy(step, parity, r, e).wait())

    where = lax.broadcasted_iota(I32, (TOK_BLOCK, SORT_ROWS), 1)
    for r in range(SORT_BLOCKS):
        rows = slice(r * TOK_BLOCK, (r + 1) * TOK_BLOCK)
        pos = pos_ref[rows, :]
        gates = gate_ref[rows, :]
        weights = jnp.zeros((TOK_BLOCK, SORT_ROWS), F32)
        for k in range(TOP_K):
            weights = jnp.where(where == pos[:, TOP_K + k:TOP_K + k + 1], gates[:, k:k + 1], weights)
        y = x_ref[rows, :] + jnp.dot(weights.astype(BF16),
                                     stage_ref[parity * SORT_BLOCKS + r].astype(BF16),
                                     preferred_element_type=F32)
        o_ref[rows, :] = _rms(y, g_ref[...])


def _combine(tables, x2, gates, pos, g_final, ys):
    n = x2.shape[0]
    tc = TOK_BLOCK * SORT_BLOCKS
    grid_spec = pltpu.PrefetchScalarGridSpec(
        num_scalar_prefetch=3,
        grid=(n // tc,),
        in_specs=[
            pl.BlockSpec((tc, D_MODEL), lambda i, *_: (i, 0)),
            pl.BlockSpec((tc, LANES), lambda i, *_: (i, 0)),
            pl.BlockSpec((tc, LANES), lambda i, *_: (i, 0)),
            pl.BlockSpec((1, D_MODEL), lambda i, *_: (0, 0)),
            pl.BlockSpec(memory_space=pl.ANY),
        ],
        out_specs=pl.BlockSpec((tc, D_MODEL), lambda i, *_: (i, 0)),
        scratch_shapes=[pltpu.VMEM((2 * SORT_BLOCKS, SORT_ROWS, D_MODEL), F32),
                        pltpu.SemaphoreType.DMA((2,))],
    )
    return pl.pallas_call(
        _combine_kernel,
        grid_spec=grid_spec,
        out_shape=jax.ShapeDtypeStruct((n, D_MODEL), F32),
        compiler_params=_params("arbitrary"),
        name="combine",
    )(*tables, x2, gates, pos, g_final, ys)


def _strict_lower(n):
    r = lax.broadcasted_iota(I32, (n, n), 0)
    c = lax.broadcasted_iota(I32, (n, n), 1)
    return (r > c).astype(BF16)


def _round_up(v, m):
    return (v + m - 1) // m * m


def _routing_tables(cnt, n):
    bm = ROW_BLOCK
    n_tok_blk = n // TOK_BLOCK
    n_rows = _round_up(n * TOP_K + n_tok_blk * N_EXPERTS * SUBLANES, bm) + N_EXPERTS * bm
    n_blk = n_rows // bm
    seg = _segment_rows(cnt[:, 0, :N_EXPERTS]).astype(I32)
    local_start = jnp.cumsum(seg, axis=1) - seg
    total = jnp.sum(seg, axis=0)
    padded = _round_up(total, bm)
    pend = jnp.cumsum(padded)
    pstart = pend - padded
    global_start = pstart[None, :] + jnp.cumsum(seg, axis=0) - seg
    tables = (local_start.reshape(-1), global_start.reshape(-1), seg.reshape(-1))

    n_used = (pend[-1:] // bm).astype(I32)
    blk_row = jnp.arange(n_blk, dtype=I32) * bm
    blk_e = jnp.minimum(jnp.sum(blk_row[:, None] >= pend[None, :], axis=1), N_EXPERTS - 1).astype(I32)
    ids = jnp.arange(N_EXPERTS, dtype=I32)
    later = jnp.where((ids[None, :] > ids[:, None]) & (padded[None, :] > 0), ids[None, :], N_EXPERTS)
    follow = jnp.min(later, axis=1)
    follow = jnp.where(follow < N_EXPERTS, follow, -1)
    next_e = jnp.sum(jnp.where(blk_e[:, None] == ids[None, :], follow[None, :], 0), axis=1).astype(I32)
    fill = jnp.concatenate([pstart + total, padded - total, pend[-1:], n_blk - pend[-1:] // bm])
    return tables, fill.astype(I32), blk_e, next_e, n_used, n_rows


def _layer(x2d, mem, b, s, g_mix, w_in, g_sb_out, g_pool_out, w_pool, pool_scale, w_out,
           g_mem_q, g_mem_kv, w_mem_q, w_mem_kv, w_mem_o, g_ffn, w_router, b_router,
           w_e_in, b_e_in, w_e_out, b_e_out, g_out):
    n = b * s
    row = lambda v: v.reshape(1, -1).astype(F32)

    proj = _in_proj(x2d, row(g_mix), w_in.astype(BF16))
    sb = _sb_attention(proj.reshape(b, s, IN_PROJ_COLS), -_strict_lower(SB_BLOCK))
    groups = len(POOL_WINDOWS)
    wp = (w_pool.astype(BF16)[:, :, None, :] * jnp.eye(groups, dtype=BF16)[:, None, :, None])
    wp = wp.reshape(POOL_WIDTH, POOL_WIDTH)
    x1 = _mix_out(x2d, sb.reshape(n, SB_WIDTH), proj, wp, row(pool_scale),
                  row(g_sb_out), row(g_pool_out), w_out.astype(BF16), s)

    kv = _mem_kv(mem, row(g_mem_kv), w_mem_kv.astype(BF16))
    spare = LANES - N_EXPERTS
    wr = jnp.pad(w_router.astype(F32), ((0, 0), (0, spare)))
    wr_hi = wr.astype(BF16)
    wr_lo = (wr - wr_hi.astype(F32)).astype(BF16)
    br = jnp.pad(b_router.astype(F32).reshape(1, -1), ((0, 0), (0, spare)), constant_values=NEG_BIG)
    x2, hf, meta, meta_t, gates, cnt = _xattn_router(
        x1, kv, row(g_mem_q), w_mem_q.astype(BF16), w_mem_o.astype(BF16), row(g_ffn),
        jnp.concatenate([wr_hi, wr_lo], axis=1), br, _strict_lower(TOK_BLOCK), s)

    tables, fill, blk_e, next_e, n_used, n_rows = _routing_tables(cnt, n)
    xs = _dispatch(tables, fill, hf, meta_t, n_rows)
    ys = _experts(blk_e, next_e, n_used, xs, w_e_in, b_e_in.reshape(N_EXPERTS, 1, -1),
                  w_e_out, b_e_out.reshape(N_EXPERTS, 1, -1))
    return _combine(tables, x2, gates, meta, row(g_out), ys)


def kernel(x, mem, g_mix, w_in, g_sb_out, g_pool_out, w_pool, pool_scale, w_out, g_mem_q, g_mem_kv,
           w_mem_q, w_mem_kv, w_mem_o, g_ffn, w_router, b_router, w_expert_in, b_expert_in,
           w_expert_out, b_expert_out, g_final):
    b, s, d = x.shape
    depth = g_mix.shape[0]
    assert d == D_MODEL and depth == 1, "the final RMSNorm is fused into the single layer's combine"
    assert s % PROJ_BLOCK == 0 and s % (SB_BLOCK * SB_QBLOCKS) == 0 and s % (TOK_BLOCK * XATTN_BLOCKS) == 0
    assert (b * s) % (TOK_BLOCK * SORT_BLOCKS) == 0
    out = _layer(x.reshape(b * s, d), mem, b, s, g_mix[0], w_in[0], g_sb_out[0], g_pool_out[0],
                 w_pool[0], pool_scale[0], w_out[0], g_mem_q[0], g_mem_kv[0], w_mem_q[0],
                 w_mem_kv[0], w_mem_o[0], g_ffn[0], w_router[0], b_router[0], w_expert_in[0],
                 b_expert_in[0], w_expert_out[0], b_expert_out[0], g_final)
    return out.reshape(b, s, d)
```
